```python
import math
import jax, jax.numpy as jnp
from jax import lax
import numpy as np

D_MODEL = 2048
BATCH = 4
SEQ = 4096
DEPTH = 4

GRID_W = 64
CTX_LEN = 256
HEAD_DIM = 128
MIX_HEADS = D_MODEL // HEAD_DIM
SGU_GROUPS = MIX_HEADS // 2
SGU_GROUP_DIM = HEAD_DIM
SGU_W = SGU_GROUPS * SGU_GROUP_DIM
CHUNK = 128
NA_HEADS = MIX_HEADS - SGU_GROUPS
NA_W = NA_HEADS * HEAD_DIM
NA_KH = 8
NA_KW = 16
DIFF_HEADS = (MIX_HEADS * 3) // 4
DIFF_QK_DIM = HEAD_DIM // 2
DIFF_V_DIM = HEAD_DIM
DIFF_QK_W = DIFF_HEADS * 2 * DIFF_QK_DIM
DIFF_V_W = DIFF_HEADS * DIFF_V_DIM
FNET_GROUPS = MIX_HEADS - DIFF_HEADS
FNET_GROUP_DIM = HEAD_DIM
FNET_W = FNET_GROUPS * FNET_GROUP_DIM
IN_WIDTH = 2 * SGU_W + 3 * NA_W
MIX_W = SGU_W + NA_W
D_FF = 4 * D_MODEL
ROPE_THETA = 10000.0
ROPE_AXIS_DIM = DIFF_QK_DIM // 2
Q_BLOCK = 128
N_EVEN = (DEPTH + 1) // 2
N_ODD = DEPTH // 2
RMS_EPS = 1e-6
LN_EPS = 1e-5
ADA_CHUNKS = 6

kernel_name = "hybrid_dit_sgu_natten_diffattn_fnet"

F32 = jnp.float32


def rmsnorm(x, g):
    xf = x.astype(F32)
    y = xf * lax.rsqrt(jnp.mean(jnp.square(xf), axis=-1, keepdims=True) + RMS_EPS)
    return (y * g.astype(F32)).astype(x.dtype)


def modulate(h, shift, scale):
    return h * (1 + scale) + shift


def sq_relu_mlp(h, w1, w2):
    return jnp.square(jax.nn.relu(h @ w1)) @ w2


def chunk_sgu(u, v, g, w_s, b_s):
    bsz, n, _ = u.shape
    shp = (bsz, n // CHUNK, CHUNK, SGU_GROUPS, SGU_GROUP_DIM)
    vf = v.reshape(shp).astype(F32)
    mu = jnp.mean(vf, axis=-1, keepdims=True)
    var = jnp.mean(jnp.square(vf - mu), axis=-1, keepdims=True)
    vn = ((vf - mu) * lax.rsqrt(var + LN_EPS) * g.reshape(SGU_GROUPS, SGU_GROUP_DIM).astype(F32)).astype(v.dtype)
    mixed = jnp.einsum('gpq,bnqgc->bnpgc', w_s, vn) + b_s.T[:, :, None]
    return (u.reshape(shp) * mixed).reshape(bsz, n, SGU_W)


def softmax_attend(q, k, v):
    s = jnp.einsum('bqhd,bkhd->bhqk', q, k).astype(F32)
    p = jax.nn.softmax(s, axis=-1).astype(v.dtype)
    return jnp.einsum('bhqk,bkhd->bqhd', p, v)


def neighbourhood_attention(q, k, v, k_ctx, v_ctx, rows, rpb):
    bsz, n, _ = q.shape
    kh = min(NA_KH, rows)
    kw = NA_KW
    grid = (bsz, rows, GRID_W, NA_HEADS, HEAD_DIM)
    qg = q.reshape(grid) * (HEAD_DIM ** -0.5)
    kg = k.reshape(grid)
    vg = v.reshape(grid)
    col = jnp.arange(GRID_W)
    col_start = jnp.clip(col - kw // 2, 0, GRID_W - kw)
    col_mask = (col[None, :] >= col_start[:, None]) & (col[None, :] < col_start[:, None] + kw)
    dx = jnp.clip(col[None, :] - col[:, None], -(kw - 1), kw - 1) + (kw - 1)
    rpb_x = rpb.astype(F32)[:, :, dx]
    n_loc = kh * GRID_W

    def row_block(r):
        rs = jnp.clip(r - kh // 2, 0, rows - kh)
        kb = lax.dynamic_slice_in_dim(kg, rs, kh, axis=1)
        vb = lax.dynamic_slice_in_dim(vg, rs, kh, axis=1)
        qr = lax.dynamic_index_in_dim(qg, r, axis=1, keepdims=False)
        dy = rs + jnp.arange(kh) - r + (NA_KH - 1)
        bias = jnp.take(rpb_x, dy, axis=1).transpose(0, 2, 1, 3)
        bias = jnp.where(col_mask[:, None, :], bias, -jnp.inf)
        s_loc = jnp.einsum('bqhd,bawhd->bhqaw', qr, kb).astype(F32) + bias[None]
        s_ctx = jnp.einsum('bqhd,bchd->bhqc', qr, k_ctx).astype(F32)
        s = jnp.concatenate([s_loc.reshape(bsz, NA_HEADS, GRID_W, n_loc), s_ctx], axis=-1)
        p = jax.nn.softmax(s, axis=-1).astype(v.dtype)
        p_loc = p[..., :n_loc].reshape(bsz, NA_HEADS, GRID_W, kh, GRID_W)
        p_ctx = p[..., n_loc:]
        return (jnp.einsum('bhqaw,bawhd->bqhd', p_loc, vb)
                + jnp.einsum('bhqc,bchd->bqhd', p_ctx, v_ctx))

    o = lax.map(row_block, jnp.arange(rows))
    return o.transpose(1, 0, 2, 3, 4).reshape(bsz, n, NA_W)


def axial_rope_tables(n):
    t = jnp.arange(n)
    row = (t // GRID_W).astype(F32)
    col = (t % GRID_W).astype(F32)
    inv = ROPE_THETA ** (-jnp.arange(0, ROPE_AXIS_DIM, 2, dtype=F32) / ROPE_AXIS_DIM)
    ang_r = row[:, None] * inv
    ang_c = col[:, None] * inv
    return (jnp.cos(ang_r), jnp.sin(ang_r), jnp.cos(ang_c), jnp.sin(ang_c))


def _rotate(z, cos, sin):
    half = z.shape[-1] // 2
    z1, z2 = z[..., :half], z[..., half:]
    return jnp.concatenate([z1 * cos - z2 * sin, z2 * cos + z1 * sin], axis=-1)


def apply_axial_rope(t, tables):
    cos_r, sin_r, cos_c, sin_c = [a[None, :, None, None, :] for a in tables]
    tf = t.astype(F32)
    out = jnp.concatenate([_rotate(tf[..., :ROPE_AXIS_DIM], cos_r, sin_r),
                           _rotate(tf[..., ROPE_AXIS_DIM:], cos_c, sin_c)], axis=-1)
    return out.astype(t.dtype)


def diff_attend(q, k, v, lam):
    s = jnp.einsum('bqhmd,bkhmd->bhmqk', q, k).astype(F32)
    p = jax.nn.softmax(s, axis=-1)
    a = (p[:, :, 0] - lam * p[:, :, 1]).astype(v.dtype)
    return jnp.einsum('bhqk,bkhd->bqhd', a, v)


def fourier_mix(f):
    bsz, n, _ = f.shape
    fg = f.reshape(bsz, n, FNET_GROUPS, FNET_GROUP_DIM).astype(F32)
    y = jnp.real(jnp.fft.fft2(fg, axes=(1, 3), norm='ortho'))
    return y.astype(f.dtype).reshape(bsz, n, FNET_W)


def even_mixer(p_lat, p_ctx, rows, sgu_g, sgu_w, sgu_b, rpb, need_ctx):
    cuts = [SGU_W, 2 * SGU_W, 2 * SGU_W + NA_W, 2 * SGU_W + 2 * NA_W]
    u, v, q, k, vv = jnp.split(p_lat, cuts, axis=-1)
    uc, vc, qc, kc, vvc = jnp.split(p_ctx, cuts, axis=-1)
    bsz, nc, _ = p_ctx.shape
    kc_h = kc.reshape(bsz, nc, NA_HEADS, HEAD_DIM)
    vc_h = vvc.reshape(bsz, nc, NA_HEADS, HEAD_DIM)
    a_lat = chunk_sgu(u, v, sgu_g, sgu_w, sgu_b)
    b_lat = neighbourhood_attention(q, k, vv, kc_h, vc_h, rows, rpb)
    m_lat = jnp.concatenate([a_lat, b_lat], axis=-1)
    if not need_ctx:
        return m_lat, None
    a_ctx = chunk_sgu(uc, vc, sgu_g, sgu_w, sgu_b)
    qc_h = qc.reshape(bsz, nc, NA_HEADS, HEAD_DIM) * (HEAD_DIM ** -0.5)
    b_ctx = softmax_attend(qc_h, kc_h, vc_h).reshape(bsz, nc, NA_W)
    return m_lat, jnp.concatenate([a_ctx, b_ctx], axis=-1)


def odd_mixer(p_lat, p_ctx, tables, lq1, lk1, lq2, lk2, sub_g, lam_init, need_ctx):
    cuts = [DIFF_QK_W, 2 * DIFF_QK_W, 2 * DIFF_QK_W + DIFF_V_W]
    q, k, v, f = jnp.split(p_lat, cuts, axis=-1)
    qc, kc, vc, fc = jnp.split(p_ctx, cuts, axis=-1)
    bsz, n, _ = p_lat.shape
    nc = p_ctx.shape[1]
    scale = DIFF_QK_DIM ** -0.5
    h5 = lambda t, m: t.reshape(bsz, m, DIFF_HEADS, 2, DIFF_QK_DIM)
    q_lat = apply_axial_rope(h5(q, n), tables) * scale
    k_lat = apply_axial_rope(h5(k, n), tables)
    v_lat = v.reshape(bsz, n, DIFF_HEADS, DIFF_V_DIM)
    k_c = h5(kc, nc)
    v_c = vc.reshape(bsz, nc, DIFF_HEADS, DIFF_V_DIM)
    lam = (jnp.exp(jnp.sum(lq1.astype(F32) * lk1.astype(F32)))
           - jnp.exp(jnp.sum(lq2.astype(F32) * lk2.astype(F32))) + lam_init)
    k_all = jnp.concatenate([k_lat, k_c], axis=1)
    v_all = jnp.concatenate([v_lat, v_c], axis=1)
    qb = q_lat.reshape(bsz, n // Q_BLOCK, Q_BLOCK, DIFF_HEADS, 2, DIFF_QK_DIM).swapaxes(0, 1)
    o = lax.map(lambda qq: diff_attend(qq, k_all, v_all, lam), qb)
    o = o.swapaxes(0, 1).reshape(bsz, n, DIFF_HEADS, DIFF_V_DIM)
    c_lat = (rmsnorm(o, sub_g) * (1 - lam_init)).reshape(bsz, n, DIFF_V_W)
    m_lat = jnp.concatenate([c_lat, fourier_mix(f)], axis=-1)
    if not need_ctx:
        return m_lat, None
    o_c = diff_attend(h5(qc, nc) * scale, k_c, v_c, lam)
    c_ctx = (rmsnorm(o_c, sub_g) * (1 - lam_init)).reshape(bsz, nc, DIFF_V_W)
    return m_lat, jnp.concatenate([c_ctx, fourier_mix(fc)], axis=-1)


def setup_inputs(seed: int = 0) -> dict:
    key = jax.random.key(seed)
    ks = jax.random.split(key, 24)
    nrm = lambda k, shp, s: jax.random.normal(k, shp, F32) * s
    gain = lambda k, shp: 1.0 + nrm(k, shp, 0.05)
    return {
        'x': nrm(ks[0], (BATCH, SEQ, D_MODEL), 1.0),
        'c': nrm(ks[1], (BATCH, D_MODEL), 1.0),
        'ctx': nrm(ks[2], (BATCH, CTX_LEN, D_MODEL), 1.0),
        'c_ctx': nrm(ks[3], (D_MODEL,), 1.0),
        'ada_w': nrm(ks[4], (DEPTH, D_MODEL, ADA_CHUNKS * D_MODEL), 0.5 * D_MODEL ** -0.5),
        'ada_b': nrm(ks[5], (DEPTH, ADA_CHUNKS * D_MODEL), 0.01),
        'norm1_g': gain(ks[6], (DEPTH, D_MODEL)),
        'norm2_g': gain(ks[7], (DEPTH, D_MODEL)),
        'w_in': nrm(ks[8], (DEPTH, D_MODEL, IN_WIDTH), D_MODEL ** -0.5),
        'w_out': nrm(ks[9], (DEPTH, MIX_W, D_MODEL), MIX_W ** -0.5),
        'sgu_norm_g': gain(ks[10], (N_EVEN, SGU_W)),
        'sgu_w': nrm(ks[11], (N_EVEN, SGU_GROUPS, CHUNK, CHUNK), CHUNK ** -0.5),
        'sgu_b': gain(ks[12], (N_EVEN, SGU_GROUPS, CHUNK)),
        'na_rpb': nrm(ks[13], (N_EVEN, NA_HEADS, 2 * NA_KH - 1, 2 * NA_KW - 1), 0.1),
        'diff_lq1': nrm(ks[14], (N_ODD, DIFF_QK_DIM), 0.1),
        'diff_lk1': nrm(ks[15], (N_ODD, DIFF_QK_DIM), 0.1),
        'diff_lq2': nrm(ks[16], (N_ODD, DIFF_QK_DIM), 0.1),
        'diff_lk2': nrm(ks[17], (N_ODD, DIFF_QK_DIM), 0.1),
        'diff_subln_g': gain(ks[18], (N_ODD, DIFF_V_DIM)),
        'mlp_w1': nrm(ks[19], (DEPTH, D_MODEL, D_FF), D_MODEL ** -0.5),
        'mlp_w2': nrm(ks[20], (DEPTH, D_FF, D_MODEL), D_FF ** -0.5),
        'final_g': gain(ks[21], (D_MODEL,)),
    }


def reference(x, c, ctx, c_ctx, ada_w, ada_b, norm1_g, norm2_g, w_in, w_out, sgu_norm_g, sgu_w, sgu_b,
              na_rpb, diff_lq1, diff_lk1, diff_lq2, diff_lk2, diff_subln_g, mlp_w1, mlp_w2, final_g):
    n = x.shape[1]
    rows = n // GRID_W
    tables = axial_rope_tables(n)
    x_lat, x_ctx = x, ctx
    s_lat = jax.nn.silu(c)
    s_ctx = jax.nn.silu(c_ctx)
    for l in range(DEPTH):
        need_ctx = l < DEPTH - 1
        sh1, sc1, g1, sh2, sc2, g2 = jnp.split(s_lat @ ada_w[l] + ada_b[l], ADA_CHUNKS, axis=-1)
        csh1, csc1, cg1, csh2, csc2, cg2 = jnp.split(s_ctx @ ada_w[l] + ada_b[l], ADA_CHUNKS, axis=-1)
        h_lat = modulate(rmsnorm(x_lat, norm1_g[l]), sh1[:, None], sc1[:, None])
        h_ctx = modulate(rmsnorm(x_ctx, norm1_g[l]), csh1, csc1)
        p_lat = h_lat @ w_in[l]
        p_ctx = h_ctx @ w_in[l]
        if l % 2 == 0:
            i = l // 2
            m_lat, m_ctx = even_mixer(p_lat, p_ctx, rows, sgu_norm_g[i], sgu_w[i], sgu_b[i], na_rpb[i], need_ctx)
        else:
            i = l // 2
            lam_init = 0.8 - 0.6 * math.exp(-0.3 * l)
            m_lat, m_ctx = odd_mixer(p_lat, p_ctx, tables, diff_lq1[i], diff_lk1[i], diff_lq2[i], diff_lk2[i],
                                     diff_subln_g[i], lam_init, need_ctx)
        x_lat = x_lat + g1[:, None] * (m_lat @ w_out[l])
        h2 = modulate(rmsnorm(x_lat, norm2_g[l]), sh2[:, None], sc2[:, None])
        x_lat = x_lat + g2[:, None] * sq_relu_mlp(h2, mlp_w1[l], mlp_w2[l])
        if need_ctx:
            x_ctx = x_ctx + cg1 * (m_ctx @ w_out[l])
            h2c = modulate(rmsnorm(x_ctx, norm2_g[l]), csh2, csc2)
            x_ctx = x_ctx + cg2 * sq_relu_mlp(h2c, mlp_w1[l], mlp_w2[l])
    return rmsnorm(x_lat, final_g)
```

```python
import functools
import math

import numpy as np
import jax
import jax.numpy as jnp
from jax import lax
from jax.experimental import pallas as pl
from jax.experimental.pallas import tpu as pltpu

F32 = jnp.float32
BF16 = jnp.bfloat16

GRID_W = 64
HEAD_DIM = 128
CHUNK = 128
NA_KH = 8
NA_KW = 16
ROPE_THETA = 10000.0
Q_ROWS = 4
K_ROWS = Q_ROWS + NA_KH
RMS_EPS = 1e-6
LN_EPS = 1e-5
ADA_CHUNKS = 6
MOD_ROWS = 16

VMEM_LIMIT = 56 * 1024 * 1024

_NT = (((1,), (1,)), ((), ()))


def _params(*sem):
    return pltpu.CompilerParams(dimension_semantics=sem, vmem_limit_bytes=VMEM_LIMIT)


def _tile(n, pref):
    if n <= pref:
        return n
    t = (pref // 128) * 128
    while n % t:
        t -= 128
    assert t > 0, (n, pref)
    return t


def _ada_kernel(s_ref, w_ref, b_ref, o_ref):
    s = s_ref[...]
    a = (s * jax.nn.sigmoid(s)).astype(BF16)
    o_ref[...] = jnp.dot(a, w_ref[...].astype(BF16), preferred_element_type=F32) + b_ref[...]


def _ada_table(cond, ada_w, ada_b):
    depth, d, n = ada_w.shape
    tn = _tile(n, 1024)
    return pl.pallas_call(
        _ada_kernel,
        out_shape=jax.ShapeDtypeStruct((depth, MOD_ROWS, n), F32),
        grid=(depth, n // tn),
        in_specs=[
            pl.BlockSpec((MOD_ROWS, d), lambda l, j: (0, 0)),
            pl.BlockSpec((None, d, tn), lambda l, j: (l, 0, j)),
            pl.BlockSpec((None, 1, tn), lambda l, j: (l, 0, j)),
        ],
        out_specs=pl.BlockSpec((None, MOD_ROWS, tn), lambda l, j: (l, 0, j)),
        compiler_params=_params("parallel", "parallel"),
        name="ada_table",
    )(cond, ada_w, ada_b.reshape(depth, 1, n))


def _norm_mod_rows(x_ref, g_ref, sh_ref, sc_ref, h_ref):
    rows = 64
    g = g_ref[...]
    sc1 = 1.0 + sc_ref[...]
    sh = sh_ref[...]

    def body(r, carry):
        sl = pl.ds(pl.multiple_of(r * rows, rows), rows)
        xv = x_ref[sl, :]
        ms = jnp.mean(xv * xv, axis=-1, keepdims=True)
        y = xv * lax.rsqrt(ms + RMS_EPS) * g
        h_ref[sl, :] = (y * sc1 + sh).astype(h_ref.dtype)
        return carry

    lax.fori_loop(0, x_ref.shape[0] // rows, body, 0)


def _mod_spec(d, chunk, row_fn):
    return pl.BlockSpec((None, 1, d), lambda i, j: (row_fn(i), 0, chunk))


def _in_kernel(x_ref, g_ref, sh_ref, sc_ref, w_ref, o_ref, h_scr):
    @pl.when(pl.program_id(1) == 0)
    def _():
        _norm_mod_rows(x_ref, g_ref, sh_ref, sc_ref, h_scr)

    o_ref[...] = jnp.dot(h_scr[...], w_ref[...], preferred_element_type=F32).astype(o_ref.dtype)


def _in_proj(x, mods, g, w, row_fn, tm):
    m, d = x.shape
    n = w.shape[1]
    tn = _tile(n, 1024)
    return pl.pallas_call(
        _in_kernel,
        out_shape=jax.ShapeDtypeStruct((m, n), BF16),
        grid=(m // tm, n // tn),
        in_specs=[
            pl.BlockSpec((tm, d), lambda i, j: (i, 0)),
            pl.BlockSpec((1, d), lambda i, j: (0, 0)),
            _mod_spec(d, 0, row_fn),
            _mod_spec(d, 1, row_fn),
            pl.BlockSpec((d, tn), lambda i, j: (0, j)),
        ],
        out_specs=pl.BlockSpec((tm, tn), lambda i, j: (i, j)),
        scratch_shapes=[pltpu.VMEM((tm, d), BF16)],
        compiler_params=_params("parallel", "arbitrary"),
        name="in_proj",
    )(x, g.reshape(1, d), mods, mods, w)


def _out_kernel(x_ref, ma_ref, mb_ref, wa_ref, wb_ref, gate_ref, o_ref):
    y = jnp.dot(ma_ref[...], wa_ref[...], preferred_element_type=F32)
    y = y + jnp.dot(mb_ref[...], wb_ref[...], preferred_element_type=F32)
    o_ref[...] = x_ref[...] + gate_ref[...] * y


def _out_proj(x, ma, mb, w, mods, row_fn, tm):
    m, d = x.shape
    ka, kb = ma.shape[1], mb.shape[1]
    assert ka % kb == 0 and w.shape == (ka + kb, d)
    return pl.pallas_call(
        _out_kernel,
        out_shape=jax.ShapeDtypeStruct((m, d), F32),
        grid=(m // tm, 1),
        in_specs=[
            pl.BlockSpec((tm, d), lambda i, j: (i, 0)),
            pl.BlockSpec((tm, ka), lambda i, j: (i, 0)),
            pl.BlockSpec((tm, kb), lambda i, j: (i, 0)),
            pl.BlockSpec((ka, d), lambda i, j: (0, 0)),
            pl.BlockSpec((kb, d), lambda i, j: (ka // kb, 0)),
            _mod_spec(d, 2, row_fn),
        ],
        out_specs=pl.BlockSpec((tm, d), lambda i, j: (i, 0)),
        compiler_params=_params("parallel", "arbitrary"),
        name="out_proj",
    )(x, ma, mb, w, w, mods)


def _mlp_kernel(x_ref, g_ref, sh_ref, sc_ref, gate_ref, w1_ref, w2_ref, fg_ref, o_ref, h_scr, acc_scr, *, final_norm):
    f = pl.program_id(1)

    @pl.when(f == 0)
    def _():
        _norm_mod_rows(x_ref, g_ref, sh_ref, sc_ref, h_scr)

    a = jnp.dot(h_scr[...], w1_ref[...], preferred_element_type=F32)
    a = jnp.square(jnp.maximum(a, 0.0)).astype(BF16)
    y = jnp.dot(a, w2_ref[...], preferred_element_type=F32)

    @pl.when(f == 0)
    def _():
        acc_scr[...] = y

    @pl.when(f > 0)
    def _():
        acc_scr[...] += y

    @pl.when(f == pl.num_programs(1) - 1)
    def _():
        out = x_ref[...] + gate_ref[...] * acc_scr[...]
        if final_norm:
            ms = jnp.mean(out * out, axis=-1, keepdims=True)
            out = out * lax.rsqrt(ms + RMS_EPS) * fg_ref[...]
        o_ref[...] = out


def _mlp(x, mods, g, w1, w2, final_g, row_fn, tm, final_norm):
    m, d = x.shape
    dff = w1.shape[1]
    tf = _tile(dff, 1024)
    return pl.pallas_call(
        functools.partial(_mlp_kernel, final_norm=final_norm),
        out_shape=jax.ShapeDtypeStruct((m, d), F32),
        grid=(m // tm, dff // tf),
        in_specs=[
            pl.BlockSpec((tm, d), lambda i, j: (i, 0)),
            pl.BlockSpec((1, d), lambda i, j: (0, 0)),
            _mod_spec(d, 3, row_fn),
            _mod_spec(d, 4, row_fn),
            _mod_spec(d, 5, row_fn),
            pl.BlockSpec((d, tf), lambda i, j: (0, j)),
            pl.BlockSpec((tf, d), lambda i, j: (j, 0)),
            pl.BlockSpec((1, d), lambda i, j: (0, 0)),
        ],
        out_specs=pl.BlockSpec((tm, d), lambda i, j: (i, 0)),
        scratch_shapes=[pltpu.VMEM((tm, d), BF16), pltpu.VMEM((tm, d), F32)],
        compiler_params=_params("parallel", "arbitrary"),
        name="mlp",
    )(x, g.reshape(1, d), mods, mods, mods, w1, w2, final_g.reshape(1, d))


def _sgu_kernel(u_ref, v_ref, gam_ref, ws_ref, b_ref, o_ref, *, groups):
    for c in range(u_ref.shape[0] // CHUNK):
        rs = slice(c * CHUNK, (c + 1) * CHUNK)
        for g in range(groups):
            cs = slice(g * HEAD_DIM, (g + 1) * HEAD_DIM)
            v = v_ref[rs, cs].astype(F32)
            mu = jnp.mean(v, axis=-1, keepdims=True)
            dv = v - mu
            var = jnp.mean(dv * dv, axis=-1, keepdims=True)
            vn = (dv * lax.rsqrt(var + LN_EPS) * gam_ref[:, cs]).astype(BF16)
            mixed = jnp.dot(ws_ref[g], vn, preferred_element_type=F32) + b_ref[g]
            o_ref[rs, cs] = (u_ref[rs, cs].astype(F32) * mixed).astype(o_ref.dtype)


def _sgu(p, gam, ws, b_full, sgu_w):
    m = p.shape[0]
    groups = sgu_w // HEAD_DIM
    tr = _tile(m, 512)
    return pl.pallas_call(
        functools.partial(_sgu_kernel, groups=groups),
        out_shape=jax.ShapeDtypeStruct((m, sgu_w), BF16),
        grid=(m // tr,),
        in_specs=[
            pl.BlockSpec((tr, sgu_w), lambda i: (i, 0)),
            pl.BlockSpec((tr, sgu_w), lambda i: (i, 1)),
            pl.BlockSpec((1, sgu_w), lambda i: (0, 0)),
            pl.BlockSpec((groups, CHUNK, CHUNK), lambda i: (0, 0, 0)),
            pl.BlockSpec((groups, CHUNK, HEAD_DIM), lambda i: (0, 0, 0)),
        ],
        out_specs=pl.BlockSpec((tr, sgu_w), lambda i: (i, 0)),
        compiler_params=_params("parallel"),
        name="sgu",
    )(p, p, gam.reshape(1, sgu_w), ws, b_full)


def _softmax_pv(scores, values, q_rows):
    m = scores[0].max(axis=-1, keepdims=True)
    for s in scores[1:]:
        m = jnp.maximum(m, s.max(axis=-1, keepdims=True))
    l = jnp.zeros((q_rows, 1), F32)
    o = None
    for s, v in zip(scores, values):
        p = jnp.exp(s - m)
        l = l + p.sum(axis=-1, keepdims=True)
        pv = jnp.dot(p.astype(BF16), v, preferred_element_type=F32)
        o = pv if o is None else o + pv
    return o / l


def _na_kernel(q_ref, k0_ref, k1_ref, k2_ref, v0_ref, v1_ref, v2_ref, kc_ref, vc_ref, bias_ref, o_ref, *, heads):
    qb = q_ref.shape[0]
    scale = HEAD_DIM ** -0.5
    for h in range(heads):
        cs = slice(h * HEAD_DIM, (h + 1) * HEAD_DIM)
        q = (q_ref[:, cs].astype(F32) * scale).astype(BF16)
        scores, values = [], []
        for j, (k_ref, v_ref) in enumerate(((k0_ref, v0_ref), (k1_ref, v1_ref), (k2_ref, v2_ref))):
            s = lax.dot_general(q, k_ref[:, cs], _NT, preferred_element_type=F32)
            scores.append(s + bias_ref[h, :, j * qb:(j + 1) * qb])
            values.append(v_ref[:, cs])
        scores.append(lax.dot_general(q, kc_ref[:, cs], _NT, preferred_element_type=F32))
        values.append(vc_ref[:, cs])
        o_ref[:, cs] = _softmax_pv(scores, values, qb).astype(o_ref.dtype)


def _na_bias(rpb, rows):
    heads = rpb.shape[0]
    kw = NA_KW
    col = jnp.arange(GRID_W)
    col_start = jnp.clip(col - kw // 2, 0, GRID_W - kw)
    col_mask = (col[None, :] >= col_start[:, None]) & (col[None, :] < col_start[:, None] + kw)
    dx = jnp.clip(col[None, :] - col[:, None], -(kw - 1), kw - 1) + (kw - 1)
    rpb_x = jnp.where(col_mask[None, None], rpb.astype(F32)[:, :, dx], -jnp.inf)
    nblk = rows // Q_ROWS
    dy_idx = np.zeros((3, Q_ROWS, K_ROWS), np.int32)
    valid = np.zeros((3, Q_ROWS, K_ROWS), bool)
    for t, kblk in enumerate((0, 1, nblk - 1)):
        ks = min(max(Q_ROWS * kblk - NA_KH // 2, 0), rows - K_ROWS)
        for i in range(Q_ROWS):
            r = Q_ROWS * kblk + i
            rs = min(max(r - NA_KH // 2, 0), rows - NA_KH)
            for j in range(K_ROWS):
                kr = ks + j
                valid[t, i, j] = rs <= kr < rs + NA_KH
                dy_idx[t, i, j] = min(max(kr - r + NA_KH - 1, 0), 2 * NA_KH - 2)
    b = rpb_x[:, dy_idx]
    b = jnp.where(jnp.asarray(valid)[None, :, :, :, None, None], b, -jnp.inf)
    b = b.transpose(1, 0, 2, 4, 3, 5)
    return b.reshape(3, heads, Q_ROWS * GRID_W, K_ROWS * GRID_W)


def _na_lat(p_lat, p_ctx, bias, bsz, seq, ctx_len, na_w, col0):
    heads = na_w // HEAD_DIM
    qb = Q_ROWS * GRID_W
    nblk = seq // qb
    assert nblk >= 4 and K_ROWS == 3 * Q_ROWS

    def kblock(j):
        return lambda b, k: (b * nblk + jnp.clip(k - 1, 0, nblk - 3) + j, col0 + 1)

    def vblock(j):
        return lambda b, k: (b * nblk + jnp.clip(k - 1, 0, nblk - 3) + j, col0 + 2)

    def btype(b, k):
        return (jnp.where(k == 0, 0, jnp.where(k == nblk - 1, 2, 1)), 0, 0, 0)

    return pl.pallas_call(
        functools.partial(_na_kernel, heads=heads),
        out_shape=jax.ShapeDtypeStruct((bsz * seq, na_w), BF16),
        grid=(bsz, nblk),
        in_specs=[
            pl.BlockSpec((qb, na_w), lambda b, k: (b * nblk + k, col0)),
            pl.BlockSpec((qb, na_w), kblock(0)),
            pl.BlockSpec((qb, na_w), kblock(1)),
            pl.BlockSpec((qb, na_w), kblock(2)),
            pl.BlockSpec((qb, na_w), vblock(0)),
            pl.BlockSpec((qb, na_w), vblock(1)),
            pl.BlockSpec((qb, na_w), vblock(2)),
            pl.BlockSpec((ctx_len, na_w), lambda b, k: (b, col0 + 1)),
            pl.BlockSpec((ctx_len, na_w), lambda b, k: (b, col0 + 2)),
            pl.BlockSpec((None, heads, qb, K_ROWS * GRID_W), btype),
        ],
        out_specs=pl.BlockSpec((qb, na_w), lambda b, k: (b * nblk + k, 0)),
        compiler_params=_params("parallel", "arbitrary"),
        name="na_lat",
    )(p_lat, p_lat, p_lat, p_lat, p_lat, p_lat, p_lat, p_ctx, p_ctx, bias)


def _attn_ctx_kernel(q_ref, k_ref, v_ref, o_ref, *, heads):
    scale = HEAD_DIM ** -0.5
    for h in range(heads):
        cs = slice(h * HEAD_DIM, (h + 1) * HEAD_DIM)
        q = (q_ref[:, cs].astype(F32) * scale).astype(BF16)
        s = lax.dot_general(q, k_ref[:, cs], _NT, preferred_element_type=F32)
        o_ref[:, cs] = _softmax_pv([s], [v_ref[:, cs]], q_ref.shape[0]).astype(o_ref.dtype)


def _na_ctx(p_ctx, bsz, ctx_len, na_w, col0):
    heads = na_w // HEAD_DIM
    return pl.pallas_call(
        functools.partial(_attn_ctx_kernel, heads=heads),
        out_shape=jax.ShapeDtypeStruct((bsz * ctx_len, na_w), BF16),
        grid=(bsz,),
        in_specs=[pl.BlockSpec((ctx_len, na_w), lambda b, c=c: (b, col0 + c)) for c in range(3)],
        out_specs=pl.BlockSpec((ctx_len, na_w), lambda b: (b, 0)),
        compiler_params=_params("parallel"),
        name="na_ctx",
    )(p_ctx, p_ctx, p_ctx)


def _swap16(x):
    lane = lax.broadcasted_iota(jnp.int32, x.shape, 1)
    up = pltpu.roll(x, HEAD_DIM - 16, 1)
    down = pltpu.roll(x, 16, 1)
    return jnp.where((lane & 16) == 0, up, down)


def _diff_kernel(*refs, n_lat, rope, lam_init):
    if rope:
        (q_ref, k_ref, v_ref, kc_ref, vc_ref, cosq_ref, sinq_ref, cos_ref, sin_ref, lam_ref, g_ref,
         o_ref, k_scr, vt_scr) = refs
    else:
        q_ref, kc_ref, vc_ref, lam_ref, g_ref, o_ref, k_scr, vt_scr = refs
    n_ctx = kc_ref.shape[0]

    @pl.when(pl.program_id(2) == 0)
    def _():
        step = 512
        for c in range(n_lat // step):
            rs = slice(c * step, (c + 1) * step)
            k = k_ref[rs, :].astype(F32)
            k_scr[rs, :] = (k * cos_ref[rs, :] + _swap16(k) * sin_ref[rs, :]).astype(BF16)
            vt_scr[:, rs] = v_ref[rs, :].astype(F32).T.astype(BF16)
        k_scr[n_lat:n_lat + n_ctx, :] = kc_ref[...]
        vt_scr[:, n_lat:n_lat + n_ctx] = vc_ref[...].astype(F32).T.astype(BF16)

    q = q_ref[...].astype(F32)
    if rope:
        q = q * cosq_ref[...] + _swap16(q) * sinq_ref[...]
    else:
        q = q * (HEAD_DIM // 2) ** -0.5
    lane = lax.broadcasted_iota(jnp.int32, q.shape, 1)
    half = HEAD_DIM // 2
    qa = jnp.where(lane < half, q, 0.0).astype(BF16)
    qb = jnp.where(lane >= half, q, 0.0).astype(BF16)

    lp = lam_ref[...]
    d1 = jnp.sum(lp[0:1, :] * lp[1:2, :], axis=-1, keepdims=True)
    d2 = jnp.sum(lp[2:3, :] * lp[3:4, :], axis=-1, keepdims=True)
    lam = jnp.exp(d1) - jnp.exp(d2) + lam_init

    keys = k_scr[...]
    s1 = lax.dot_general(keys, qa, _NT, preferred_element_type=F32)
    s2 = lax.dot_general(keys, qb, _NT, preferred_element_type=F32)
    p1 = jnp.exp(s1 - s1.max(axis=0, keepdims=True))
    p2 = jnp.exp(s2 - s2.max(axis=0, keepdims=True))
    c1 = 1.0 / p1.sum(axis=0, keepdims=True)
    c2 = lam / p2.sum(axis=0, keepdims=True)
    a = (p1 * c1 - p2 * c2).astype(BF16)
    ot = jnp.dot(vt_scr[...], a, preferred_element_type=F32)
    ot = ot * lax.rsqrt(jnp.mean(ot * ot, axis=0, keepdims=True) + RMS_EPS)
    o_ref[...] = ((ot.T * g_ref[...]) * (1.0 - lam_init)).astype(o_ref.dtype)


def _diff_lat(p_lat, p_ctx, tabs, lam_p, sub_g, bsz, seq, ctx_len, heads, lam_init):
    tq = 256
    nq = seq // tq
    hd = HEAD_DIM
    nk = seq + ctx_len
    cosq, sinq, cos, sin = tabs
    return pl.pallas_call(
        functools.partial(_diff_kernel, n_lat=seq, rope=True, lam_init=lam_init),
        out_shape=jax.ShapeDtypeStruct((bsz * seq, heads * hd), BF16),
        grid=(bsz, heads, nq),
        in_specs=[
            pl.BlockSpec((tq, hd), lambda b, h, i: (b * nq + i, h)),
            pl.BlockSpec((seq, hd), lambda b, h, i: (b, heads + h)),
            pl.BlockSpec((seq, hd), lambda b, h, i: (b, 2 * heads + h)),
            pl.BlockSpec((ctx_len, hd), lambda b, h, i: (b, heads + h)),
            pl.BlockSpec((ctx_len, hd), lambda b, h, i: (b, 2 * heads + h)),
            pl.BlockSpec((tq, hd), lambda b, h, i: (i, 0)),
            pl.BlockSpec((tq, hd), lambda b, h, i: (i, 0)),
            pl.BlockSpec((seq, hd), lambda b, h, i: (0, 0)),
            pl.BlockSpec((seq, hd), lambda b, h, i: (0, 0)),
            pl.BlockSpec((4, hd // 2), lambda b, h, i: (0, 0)),
            pl.BlockSpec((1, hd), lambda b, h, i: (0, 0)),
        ],
        out_specs=pl.BlockSpec((tq, hd), lambda b, h, i: (b * nq + i, h)),
        scratch_shapes=[pltpu.VMEM((nk, hd), BF16), pltpu.VMEM((hd, nk), BF16)],
        compiler_params=_params("parallel", "parallel", "arbitrary"),
        name="diff_lat",
    )(p_lat, p_lat, p_lat, p_ctx, p_ctx, cosq, sinq, cos, sin, lam_p, sub_g.reshape(1, hd))


def _diff_ctx(p_ctx, lam_p, sub_g, bsz, ctx_len, heads, lam_init):
    hd = HEAD_DIM
    return pl.pallas_call(
        functools.partial(_diff_kernel, n_lat=0, rope=False, lam_init=lam_init),
        out_shape=jax.ShapeDtypeStruct((bsz * ctx_len, heads * hd), BF16),
        grid=(bsz, heads, 1),
        in_specs=[
            pl.BlockSpec((ctx_len, hd), lambda b, h, i: (b, h)),
            pl.BlockSpec((ctx_len, hd), lambda b, h, i: (b, heads + h)),
            pl.BlockSpec((ctx_len, hd), lambda b, h, i: (b, 2 * heads + h)),
            pl.BlockSpec((4, hd // 2), lambda b, h, i: (0, 0)),
            pl.BlockSpec((1, hd), lambda b, h, i: (0, 0)),
        ],
        out_specs=pl.BlockSpec((ctx_len, hd), lambda b, h, i: (b, h)),
        scratch_shapes=[pltpu.VMEM((ctx_len, hd), BF16), pltpu.VMEM((hd, ctx_len), BF16)],
        compiler_params=_params("parallel", "parallel", "arbitrary"),
        name="diff_ctx",
    )(p_ctx, p_ctx, p_ctx, lam_p, sub_g.reshape(1, hd))


def _rope_tables(seq):
    axis = HEAD_DIM // 4
    t = jnp.arange(seq)
    row = (t // GRID_W).astype(F32)
    col = (t % GRID_W).astype(F32)
    inv = ROPE_THETA ** (-jnp.arange(0, axis, 2, dtype=F32) / axis)
    ang_r = row[:, None] * inv
    ang_c = col[:, None] * inv
    cr, sr, cc, sc = jnp.cos(ang_r), jnp.sin(ang_r), jnp.cos(ang_c), jnp.sin(ang_c)
    cos = jnp.concatenate([cr, cr, cc, cc] * 2, axis=-1)
    sin = jnp.concatenate([-sr, sr, -sc, sc] * 2, axis=-1)
    scale = (HEAD_DIM // 2) ** -0.5
    return cos * scale, sin * scale, cos, sin


def _fnet_kernel(f_ref, cn_ref, sn_ref, cc_ref, sc_ref, o_ref, a_scr, b_scr, *, norm):
    @pl.when(pl.program_id(1) == 0)
    def _():
        step = 512 if f_ref.shape[0] % 512 == 0 else f_ref.shape[0]
        for c in range(f_ref.shape[0] // step):
            rs = slice(c * step, (c + 1) * step)
            f = f_ref[rs, :]
            a_scr[rs, :] = jnp.dot(f, cc_ref[...], preferred_element_type=F32).astype(BF16)
            b_scr[rs, :] = jnp.dot(f, sc_ref[...], preferred_element_type=F32).astype(BF16)

    y = jnp.dot(cn_ref[...], a_scr[...], preferred_element_type=F32)
    y = y - jnp.dot(sn_ref[...], b_scr[...], preferred_element_type=F32)
    o_ref[...] = (y * norm).astype(o_ref.dtype)


def _fnet(p, cn, sn, ccb, scb, bsz, n, fw, col):
    tr = _tile(n, 512)
    norm = 1.0 / math.sqrt(n * HEAD_DIM)
    return pl.pallas_call(
        functools.partial(_fnet_kernel, norm=norm),
        out_shape=jax.ShapeDtypeStruct((bsz * n, fw), BF16),
        grid=(bsz, n // tr),
        in_specs=[
            pl.BlockSpec((n, fw), lambda b, i: (b, col)),
            pl.BlockSpec((tr, n), lambda b, i: (i, 0)),
            pl.BlockSpec((tr, n), lambda b, i: (i, 0)),
            pl.BlockSpec((fw, fw), lambda b, i: (0, 0)),
            pl.BlockSpec((fw, fw), lambda b, i: (0, 0)),
        ],
        out_specs=pl.BlockSpec((tr, fw), lambda b, i: (b * (n // tr) + i, 0)),
        scratch_shapes=[pltpu.VMEM((n, fw), BF16), pltpu.VMEM((n, fw), BF16)],
        compiler_params=_params("parallel", "arbitrary"),
        name="fnet",
    )(p, cn, sn, ccb, scb)


def _dft_tables(n):
    if n <= 256:
        idx = np.outer(np.arange(n), np.arange(n)) % n
        ang = 2.0 * np.pi * idx / n
        return jnp.asarray(np.cos(ang), F32).astype(BF16), jnp.asarray(np.sin(ang), F32).astype(BF16)
    r = int(round(math.sqrt(n)))
    assert r * r == n
    k = np.arange(n)
    t = np.arange(r)
    ang_a = 2.0 * np.pi * (np.outer(t, k) % r) / r
    ang_b = 2.0 * np.pi * (np.outer(t, k) % n) / n
    ca, sa = jnp.asarray(np.cos(ang_a), F32)[:, None, :], jnp.asarray(np.sin(ang_a), F32)[:, None, :]
    cb, sb = jnp.asarray(np.cos(ang_b), F32)[None, :, :], jnp.asarray(np.sin(ang_b), F32)[None, :, :]
    cos = (ca * cb - sa * sb).reshape(n, n).astype(BF16)
    sin = (sa * cb + ca * sb).reshape(n, n).astype(BF16)
    return cos, sin


def _channel_dft_tables(fw):
    idx = np.outer(np.arange(HEAD_DIM), np.arange(HEAD_DIM)) % HEAD_DIM
    ang = 2.0 * np.pi * idx / HEAD_DIM
    eye = np.eye(fw // HEAD_DIM)
    return (jnp.asarray(np.kron(eye, np.cos(ang)), F32).astype(BF16),
            jnp.asarray(np.kron(eye, np.sin(ang)), F32).astype(BF16))


def kernel(x, c, ctx, c_ctx, ada_w, ada_b, norm1_g, norm2_g, w_in, w_out, sgu_norm_g, sgu_w, sgu_b, na_rpb,
           diff_lq1, diff_lk1, diff_lq2, diff_lk2, diff_subln_g, mlp_w1, mlp_w2, final_g):
    bsz, seq, d = x.shape
    ctx_len = ctx.shape[1]
    depth = ada_w.shape[0]
    rows = seq // GRID_W
    mix_heads = d // HEAD_DIM
    sgu_w_dim = (mix_heads // 2) * HEAD_DIM
    na_w = d - sgu_w_dim
    diff_heads = (mix_heads * 3) // 4
    fnet_w = d - diff_heads * HEAD_DIM
    assert sgu_w_dim == na_w and bsz + 1 <= MOD_ROWS

    cond = jnp.zeros((MOD_ROWS, d), F32).at[:bsz].set(c).at[bsz].set(c_ctx)
    mods_all = _ada_table(cond, ada_w, ada_b)

    w_in_b, w_out_b = w_in.astype(BF16), w_out.astype(BF16)
    w1_b, w2_b = mlp_w1.astype(BF16), mlp_w2.astype(BF16)
    sgu_w_b = sgu_w.astype(BF16)

    tm_lat = _tile(seq, 512)
    tm_ctx = _tile(bsz * ctx_len, 512)
    lat_row = lambda i: (i * tm_lat) // seq
    ctx_row = lambda i: bsz

    x_lat = x.reshape(bsz * seq, d)
    x_ctx = ctx.reshape(bsz * ctx_len, d)

    if depth > 1:
        rope_tabs = _rope_tables(seq)
        cn, sn = _dft_tables(seq)
        cn_c, sn_c = _dft_tables(ctx_len)
        ccb, scb = _channel_dft_tables(fnet_w)

    for l in range(depth):
        need_ctx = l < depth - 1
        mods = mods_all[l].reshape(MOD_ROWS, 1, ADA_CHUNKS * d)
        p_lat = _in_proj(x_lat, mods, norm1_g[l], w_in_b[l], lat_row, tm_lat)
        p_ctx = _in_proj(x_ctx, mods, norm1_g[l], w_in_b[l], ctx_row, tm_ctx)
        i = l // 2
        ma_ctx = mb_ctx = None
        if l % 2 == 0:
            b_full = jnp.broadcast_to(sgu_b[i][:, :, None], sgu_b[i].shape + (HEAD_DIM,)).astype(F32)
            bias = _na_bias(na_rpb[i], rows)
            ma_lat = _sgu(p_lat, sgu_norm_g[i], sgu_w_b[i], b_full, sgu_w_dim)
            mb_lat = _na_lat(p_lat, p_ctx, bias, bsz, seq, ctx_len, na_w, 2)
            if need_ctx:
                ma_ctx = _sgu(p_ctx, sgu_norm_g[i], sgu_w_b[i], b_full, sgu_w_dim)
                mb_ctx = _na_ctx(p_ctx, bsz, ctx_len, na_w, 2)
        else:
            lam_init = 0.8 - 0.6 * math.exp(-0.3 * l)
            lam_p = jnp.stack([diff_lq1[i], diff_lk1[i], diff_lq2[i], diff_lk2[i]]).astype(F32)
            fcol = (3 * diff_heads * HEAD_DIM) // fnet_w
            ma_lat = _diff_lat(p_lat, p_ctx, rope_tabs, lam_p, diff_subln_g[i], bsz, seq, ctx_len, diff_heads, lam_init)
            mb_lat = _fnet(p_lat, cn, sn, ccb, scb, bsz, seq, fnet_w, fcol)
            if need_ctx:
                ma_ctx = _diff_ctx(p_ctx, lam_p, diff_subln_g[i], bsz, ctx_len, diff_heads, lam_init)
                mb_ctx = _fnet(p_ctx, cn_c, sn_c, ccb, scb, bsz, ctx_len, fnet_w, fcol)
        x_lat = _out_proj(x_lat, ma_lat, mb_lat, w_out_b[l], mods, lat_row, tm_lat)
        x_lat = _mlp(x_lat, mods, norm2_g[l], w1_b[l], w2_b[l], final_g, lat_row, tm_lat, final_norm=not need_ctx)
        if need_ctx:
            x_ctx = _out_proj(x_ctx, ma_ctx, mb_ctx, w_out_b[l], mods, ctx_row, tm_ctx)
            x_ctx = _mlp(x_ctx, mods, norm2_g[l], w1_b[l], w2_b[l], final_g, ctx_row, tm_ctx, final_norm=False)
    return x_lat.reshape(bsz, seq, d)
```

```python
import functools
import math

import numpy as np
import jax
import jax.numpy as jnp
from jax import lax
from jax.experimental import pallas as pl
from jax.experimental.pallas import tpu as pltpu

F32 = jnp.float32
BF16 = jnp.bfloat16

GRID_W = 64
HEAD_DIM = 128
CHUNK = 128
NA_KH = 8
NA_KW = 16
ROPE_THETA = 10000.0
Q_ROWS = 4
K_ROWS = Q_ROWS + NA_KH
RMS_EPS = 1e-6
LN_EPS = 1e-5
ADA_CHUNKS = 6
MOD_ROWS = 16

VMEM_LIMIT = 56 * 1024 * 1024

_NT = (((1,), (1,)), ((), ()))

_Q_SCALE = (HEAD_DIM // 2) ** -0.5 * math.log2(math.e)
DIFF_TQ = 512
DIFF_TK = 256
DIFF_AHEAD = 2


def _params(*sem):
    return pltpu.CompilerParams(dimension_semantics=sem, vmem_limit_bytes=VMEM_LIMIT)


def _tile(n, pref):
    if n <= pref:
        return n
    t = (pref // 128) * 128
    while n % t:
        t -= 128
    assert t > 0, (n, pref)
    return t


def _ada_kernel(s_ref, w_ref, b_ref, o_ref):
    s = s_ref[...]
    a = (s * jax.nn.sigmoid(s)).astype(BF16)
    o_ref[...] = jnp.dot(a, w_ref[...].astype(BF16), preferred_element_type=F32) + b_ref[...]


def _ada_table(cond, ada_w, ada_b):
    depth, d, n = ada_w.shape
    tn = _tile(n, 1024)
    return pl.pallas_call(
        _ada_kernel,
        out_shape=jax.ShapeDtypeStruct((depth, MOD_ROWS, n), F32),
        grid=(depth, n // tn),
        in_specs=[
            pl.BlockSpec((MOD_ROWS, d), lambda l, j: (0, 0)),
            pl.BlockSpec((None, d, tn), lambda l, j: (l, 0, j)),
            pl.BlockSpec((None, 1, tn), lambda l, j: (l, 0, j)),
        ],
        out_specs=pl.BlockSpec((None, MOD_ROWS, tn), lambda l, j: (l, 0, j)),
        compiler_params=_params("parallel", "parallel"),
        name="ada_table",
    )(cond, ada_w, ada_b.reshape(depth, 1, n))


def _norm_mod_rows(x_ref, g_ref, sh_ref, sc_ref, h_ref):
    rows = 64
    g = g_ref[...]
    sc1 = 1.0 + sc_ref[...]
    sh = sh_ref[...]

    def body(r, carry):
        sl = pl.ds(pl.multiple_of(r * rows, rows), rows)
        xv = x_ref[sl, :]
        ms = jnp.mean(xv * xv, axis=-1, keepdims=True)
        y = xv * lax.rsqrt(ms + RMS_EPS) * g
        h_ref[sl, :] = (y * sc1 + sh).astype(h_ref.dtype)
        return carry

    lax.fori_loop(0, x_ref.shape[0] // rows, body, 0)


def _mod_spec(d, layer, chunk, row_fn):
    return pl.BlockSpec((None, None, 1, d), lambda i, j: (layer, row_fn(i), 0, chunk))


def _layer_vec_spec(d, layer):
    return pl.BlockSpec((None, 1, d), lambda i, j: (layer, 0, 0))


def _in_kernel(x_ref, g_ref, sh_ref, sc_ref, w_ref, o_ref, h_scr):
    @pl.when(pl.program_id(1) == 0)
    def _():
        _norm_mod_rows(x_ref, g_ref, sh_ref, sc_ref, h_scr)

    o_ref[...] = jnp.dot(h_scr[...], w_ref[...], preferred_element_type=F32).astype(o_ref.dtype)


def _in_proj(x, mods, g, w, layer, row_fn, tm):
    m, d = x.shape
    n = w.shape[2]
    tn = _tile(n, 1024)
    return pl.pallas_call(
        _in_kernel,
        out_shape=jax.ShapeDtypeStruct((m, n), BF16),
        grid=(m // tm, n // tn),
        in_specs=[
            pl.BlockSpec((tm, d), lambda i, j: (i, 0)),
            _layer_vec_spec(d, layer),
            _mod_spec(d, layer, 0, row_fn),
            _mod_spec(d, layer, 1, row_fn),
            pl.BlockSpec((None, d, tn), lambda i, j: (layer, 0, j)),
        ],
        out_specs=pl.BlockSpec((tm, tn), lambda i, j: (i, j)),
        scratch_shapes=[pltpu.VMEM((tm, d), BF16)],
        compiler_params=_params("parallel", "arbitrary"),
        name="in_proj",
    )(x, g, mods, mods, w)


def _out_kernel(x_ref, ma_ref, mb_ref, wa_ref, wb_ref, gate_ref, o_ref):
    y = jnp.dot(ma_ref[...], wa_ref[...], preferred_element_type=F32)
    y = y + jnp.dot(mb_ref[...], wb_ref[...], preferred_element_type=F32)
    o_ref[...] = x_ref[...] + gate_ref[...] * y


def _out_proj(x, ma, mb, w, mods, layer, row_fn, tm):
    m, d = x.shape
    ka, kb = ma.shape[1], mb.shape[1]
    assert ka % kb == 0 and w.shape[1:] == (ka + kb, d)
    return pl.pallas_call(
        _out_kernel,
        out_shape=jax.ShapeDtypeStruct((m, d), F32),
        grid=(m // tm, 1),
        in_specs=[
            pl.BlockSpec((tm, d), lambda i, j: (i, 0)),
            pl.BlockSpec((tm, ka), lambda i, j: (i, 0)),
            pl.BlockSpec((tm, kb), lambda i, j: (i, 0)),
            pl.BlockSpec((None, ka, d), lambda i, j: (layer, 0, 0)),
            pl.BlockSpec((None, kb, d), lambda i, j: (layer, ka // kb, 0)),
            _mod_spec(d, layer, 2, row_fn),
        ],
        out_specs=pl.BlockSpec((tm, d), lambda i, j: (i, 0)),
        compiler_params=_params("parallel", "arbitrary"),
        name="out_proj",
    )(x, ma, mb, w, w, mods)


def _mlp_kernel(x_ref, g_ref, sh_ref, sc_ref, gate_ref, w1_ref, w2_ref, fg_ref, o_ref, h_scr, acc_scr, *, final_norm):
    f = pl.program_id(1)

    @pl.when(f == 0)
    def _():
        _norm_mod_rows(x_ref, g_ref, sh_ref, sc_ref, h_scr)

    a = jnp.dot(h_scr[...], w1_ref[...], preferred_element_type=F32)
    a = jnp.square(jnp.maximum(a, 0.0)).astype(BF16)
    y = jnp.dot(a, w2_ref[...], preferred_element_type=F32)

    @pl.when(f == 0)
    def _():
        acc_scr[...] = y

    @pl.when(f > 0)
    def _():
        acc_scr[...] += y

    @pl.when(f == pl.num_programs(1) - 1)
    def _():
        out = x_ref[...] + gate_ref[...] * acc_scr[...]
        if final_norm:
            ms = jnp.mean(out * out, axis=-1, keepdims=True)
            out = out * lax.rsqrt(ms + RMS_EPS) * fg_ref[...]
        o_ref[...] = out


def _mlp(x, mods, g, w1, w2, final_g, layer, row_fn, tm, final_norm):
    m, d = x.shape
    dff = w1.shape[2]
    tf = _tile(dff, 1024)
    return pl.pallas_call(
        functools.partial(_mlp_kernel, final_norm=final_norm),
        out_shape=jax.ShapeDtypeStruct((m, d), F32),
        grid=(m // tm, dff // tf),
        in_specs=[
            pl.BlockSpec((tm, d), lambda i, j: (i, 0)),
            _layer_vec_spec(d, layer),
            _mod_spec(d, layer, 3, row_fn),
            _mod_spec(d, layer, 4, row_fn),
            _mod_spec(d, layer, 5, row_fn),
            pl.BlockSpec((None, d, tf), lambda i, j: (layer, 0, j)),
            pl.BlockSpec((None, tf, d), lambda i, j: (layer, j, 0)),
            pl.BlockSpec((1, d), lambda i, j: (0, 0)),
        ],
        out_specs=pl.BlockSpec((tm, d), lambda i, j: (i, 0)),
        scratch_shapes=[pltpu.VMEM((tm, d), BF16), pltpu.VMEM((tm, d), F32)],
        compiler_params=_params("parallel", "arbitrary"),
        name="mlp",
    )(x, g, mods, mods, mods, w1, w2, final_g.reshape(1, d))


def _sgu_kernel(u_ref, v_ref, gam_ref, ws_ref, b_ref, o_ref, *, groups):
    for c in range(u_ref.shape[0] // CHUNK):
        rs = slice(c * CHUNK, (c + 1) * CHUNK)
        for g in range(groups):
            cs = slice(g * HEAD_DIM, (g + 1) * HEAD_DIM)
            v = v_ref[rs, cs].astype(F32)
            mu = jnp.mean(v, axis=-1, keepdims=True)
            dv = v - mu
            var = jnp.mean(dv * dv, axis=-1, keepdims=True)
            vn = (dv * lax.rsqrt(var + LN_EPS) * gam_ref[:, cs]).astype(BF16)
            mixed = jnp.dot(ws_ref[g], vn, preferred_element_type=F32) + b_ref[g]
            o_ref[rs, cs] = (u_ref[rs, cs].astype(F32) * mixed).astype(o_ref.dtype)


def _sgu(p, gam, ws, b_full, layer, sgu_w):
    m = p.shape[0]
    groups = sgu_w // HEAD_DIM
    tr = _tile(m, 512)
    return pl.pallas_call(
        functools.partial(_sgu_kernel, groups=groups),
        out_shape=jax.ShapeDtypeStruct((m, sgu_w), BF16),
        grid=(m // tr,),
        in_specs=[
            pl.BlockSpec((tr, sgu_w), lambda i: (i, 0)),
            pl.BlockSpec((tr, sgu_w), lambda i: (i, 1)),
            pl.BlockSpec((None, 1, sgu_w), lambda i: (layer, 0, 0)),
            pl.BlockSpec((None, groups, CHUNK, CHUNK), lambda i: (layer, 0, 0, 0)),
            pl.BlockSpec((None, groups, CHUNK, HEAD_DIM), lambda i: (layer, 0, 0, 0)),
        ],
        out_specs=pl.BlockSpec((tr, sgu_w), lambda i: (i, 0)),
        compiler_params=_params("parallel"),
        name="sgu",
    )(p, p, gam, ws, b_full)


def _softmax_pv(scores, values, q_rows):
    m = scores[0].max(axis=-1, keepdims=True)
    for s in scores[1:]:
        m = jnp.maximum(m, s.max(axis=-1, keepdims=True))
    l = jnp.zeros((q_rows, 1), F32)
    o = None
    for s, v in zip(scores, values):
        p = jnp.exp(s - m)
        l = l + p.sum(axis=-1, keepdims=True)
        pv = jnp.dot(p.astype(BF16), v, preferred_element_type=F32)
        o = pv if o is None else o + pv
    return o / l


def _na_kernel(q_ref, k0_ref, k1_ref, k2_ref, v0_ref, v1_ref, v2_ref, kc_ref, vc_ref, bias_ref, o_ref, *, heads):
    qb = q_ref.shape[0]
    scale = HEAD_DIM ** -0.5
    for h in range(heads):
        cs = slice(h * HEAD_DIM, (h + 1) * HEAD_DIM)
        q = (q_ref[:, cs].astype(F32) * scale).astype(BF16)
        scores, values = [], []
        for j, (k_ref, v_ref) in enumerate(((k0_ref, v0_ref), (k1_ref, v1_ref), (k2_ref, v2_ref))):
            s = lax.dot_general(q, k_ref[:, cs], _NT, preferred_element_type=F32)
            scores.append(s + bias_ref[h, :, j * qb:(j + 1) * qb])
            values.append(v_ref[:, cs])
        scores.append(lax.dot_general(q, kc_ref[:, cs], _NT, preferred_element_type=F32))
        values.append(vc_ref[:, cs])
        o_ref[:, cs] = _softmax_pv(scores, values, qb).astype(o_ref.dtype)


def _na_bias(rpb, rows):
    heads = rpb.shape[0]
    kw = NA_KW
    col = jnp.arange(GRID_W)
    col_start = jnp.clip(col - kw // 2, 0, GRID_W - kw)
    col_mask = (col[None, :] >= col_start[:, None]) & (col[None, :] < col_start[:, None] + kw)
    dx = jnp.clip(col[None, :] - col[:, None], -(kw - 1), kw - 1) + (kw - 1)
    rpb_x = jnp.where(col_mask[None, None], rpb.astype(F32)[:, :, dx], -jnp.inf)
    nblk = rows // Q_ROWS
    dy_idx = np.zeros((3, Q_ROWS, K_ROWS), np.int32)
    valid = np.zeros((3, Q_ROWS, K_ROWS), bool)
    for t, kblk in enumerate((0, 1, nblk - 1)):
        ks = min(max(Q_ROWS * kblk - NA_KH // 2, 0), rows - K_ROWS)
        for i in range(Q_ROWS):
            r = Q_ROWS * kblk + i
            rs = min(max(r - NA_KH // 2, 0), rows - NA_KH)
            for j in range(K_ROWS):
                kr = ks + j
                valid[t, i, j] = rs <= kr < rs + NA_KH
                dy_idx[t, i, j] = min(max(kr - r + NA_KH - 1, 0), 2 * NA_KH - 2)
    b = rpb_x[:, dy_idx]
    b = jnp.where(jnp.asarray(valid)[None, :, :, :, None, None], b, -jnp.inf)
    b = b.transpose(1, 0, 2, 4, 3, 5)
    return b.reshape(3, heads, Q_ROWS * GRID_W, K_ROWS * GRID_W)


def _na_lat(p_lat, p_ctx, bias, bsz, seq, ctx_len, na_w, col0):
    heads = na_w // HEAD_DIM
    qb = Q_ROWS * GRID_W
    nblk = seq // qb
    assert nblk >= 4 and K_ROWS == 3 * Q_ROWS

    def kblock(j):
        return lambda b, k: (b * nblk + jnp.clip(k - 1, 0, nblk - 3) + j, col0 + 1)

    def vblock(j):
        return lambda b, k: (b * nblk + jnp.clip(k - 1, 0, nblk - 3) + j, col0 + 2)

    def btype(b, k):
        return (jnp.where(k == 0, 0, jnp.where(k == nblk - 1, 2, 1)), 0, 0, 0)

    return pl.pallas_call(
        functools.partial(_na_kernel, heads=heads),
        out_shape=jax.ShapeDtypeStruct((bsz * seq, na_w), BF16),
        grid=(bsz, nblk),
        in_specs=[
            pl.BlockSpec((qb, na_w), lambda b, k: (b * nblk + k, col0)),
            pl.BlockSpec((qb, na_w), kblock(0)),
            pl.BlockSpec((qb, na_w), kblock(1)),
            pl.BlockSpec((qb, na_w), kblock(2)),
            pl.BlockSpec((qb, na_w), vblock(0)),
            pl.BlockSpec((qb, na_w), vblock(1)),
            pl.BlockSpec((qb, na_w), vblock(2)),
            pl.BlockSpec((ctx_len, na_w), lambda b, k: (b, col0 + 1)),
            pl.BlockSpec((ctx_len, na_w), lambda b, k: (b, col0 + 2)),
            pl.BlockSpec((None, heads, qb, K_ROWS * GRID_W), btype),
        ],
        out_specs=pl.BlockSpec((qb, na_w), lambda b, k: (b * nblk + k, 0)),
        compiler_params=_params("parallel", "arbitrary"),
        name="na_lat",
    )(p_lat, p_lat, p_lat, p_lat, p_lat, p_lat, p_lat, p_ctx, p_ctx, bias)


def _attn_ctx_kernel(q_ref, k_ref, v_ref, o_ref, *, heads):
    scale = HEAD_DIM ** -0.5
    for h in range(heads):
        cs = slice(h * HEAD_DIM, (h + 1) * HEAD_DIM)
        q = (q_ref[:, cs].astype(F32) * scale).astype(BF16)
        s = lax.dot_general(q, k_ref[:, cs], _NT, preferred_element_type=F32)
        o_ref[:, cs] = _softmax_pv([s], [v_ref[:, cs]], q_ref.shape[0]).astype(o_ref.dtype)


def _na_ctx(p_ctx, bsz, ctx_len, na_w, col0):
    heads = na_w // HEAD_DIM
    return pl.pallas_call(
        functools.partial(_attn_ctx_kernel, heads=heads),
        out_shape=jax.ShapeDtypeStruct((bsz * ctx_len, na_w), BF16),
        grid=(bsz,),
        in_specs=[pl.BlockSpec((ctx_len, na_w), lambda b, c=c: (b, col0 + c)) for c in range(3)],
        out_specs=pl.BlockSpec((ctx_len, na_w), lambda b: (b, 0)),
        compiler_params=_params("parallel"),
        name="na_ctx",
    )(p_ctx, p_ctx, p_ctx)


def _swap16(x):
    lane = lax.broadcasted_iota(jnp.int32, x.shape, 1)
    up = pltpu.roll(x, HEAD_DIM - 16, 1)
    down = pltpu.roll(x, 16, 1)
    return jnp.where((lane & 16) == 0, up, down)


def _diff_kernel(*refs, n_lat, rope, lam_init):
    if rope:
        (q_ref, k_ref, v_ref, kc_ref, vc_ref, cosq_ref, sinq_ref, cos_ref, sin_ref, lam_ref, g_ref,
         o_ref, k_scr, vt_scr) = refs
    else:
        q_ref, kc_ref, vc_ref, lam_ref, g_ref, o_ref, k_scr, vt_scr = refs
    n_ctx = kc_ref.shape[0]

    @pl.when(pl.program_id(2) == 0)
    def _():
        step = 512
        for c in range(n_lat // step):
            rs = slice(c * step, (c + 1) * step)
            k = k_ref[rs, :].astype(F32)
            k_scr[rs, :] = (k * cos_ref[rs, :] + _swap16(k) * sin_ref[rs, :]).astype(BF16)
            vt_scr[:, rs] = v_ref[rs, :].astype(F32).T.astype(BF16)
        k_scr[n_lat:n_lat + n_ctx, :] = kc_ref[...]
        vt_scr[:, n_lat:n_lat + n_ctx] = vc_ref[...].astype(F32).T.astype(BF16)

    q = q_ref[...].astype(F32)
    if rope:
        q = q * cosq_ref[...] + _swap16(q) * sinq_ref[...]
    else:
        q = q * _Q_SCALE
    tq = q.shape[0]
    lane = lax.broadcasted_iota(jnp.int32, q.shape, 1)
    half = HEAD_DIM // 2
    qab = jnp.concatenate([jnp.where(lane < half, q, 0.0), jnp.where(lane >= half, q, 0.0)], axis=0).astype(BF16)

    lp = lam_ref[...]
    d1 = jnp.sum(lp[0:1, :] * lp[1:2, :], axis=-1, keepdims=True)
    d2 = jnp.sum(lp[2:3, :] * lp[3:4, :], axis=-1, keepdims=True)
    lam = jnp.exp(d1) - jnp.exp(d2) + lam_init

    tk = DIFF_TK
    m = l = acc = None
    n_chunks = (n_lat + n_ctx) // tk

    def scores(c):
        return lax.dot_general(k_scr[c * tk:(c + 1) * tk, :], qab, _NT, preferred_element_type=F32)

    pending = [scores(c) for c in range(min(DIFF_AHEAD, n_chunks))]
    for c in range(n_chunks):
        ks = slice(c * tk, (c + 1) * tk)
        s = pending.pop(0)
        if c + DIFF_AHEAD < n_chunks:
            pending.append(scores(c + DIFF_AHEAD))
        mc = s.max(axis=0, keepdims=True)
        m_new = mc if c == 0 else jnp.maximum(m, mc)
        p = jnp.exp2(s - m_new)
        ls = p.sum(axis=0, keepdims=True)
        pv = jnp.dot(vt_scr[:, ks], p.astype(BF16), preferred_element_type=F32)
        if c == 0:
            l, acc = ls, pv
        else:
            alpha = jnp.exp2(m - m_new)
            l = l * alpha + ls
            acc = acc * alpha + pv
        m = m_new
    c1 = 1.0 / l[:, :tq]
    c2 = lam / l[:, tq:]
    ot = acc[:, :tq] * c1 - acc[:, tq:] * c2
    ot = ot * lax.rsqrt(jnp.mean(ot * ot, axis=0, keepdims=True) + RMS_EPS)
    o_ref[...] = ((ot.T * g_ref[...]) * (1.0 - lam_init)).astype(o_ref.dtype)


def _diff_lat(p_lat, p_ctx, tabs, lam_p, sub_g, bsz, seq, ctx_len, heads, lam_init):
    tq = DIFF_TQ
    nq = seq // tq
    hd = HEAD_DIM
    nk = seq + ctx_len
    cosq, sinq, cos, sin = tabs
    return pl.pallas_call(
        functools.partial(_diff_kernel, n_lat=seq, rope=True, lam_init=lam_init),
        out_shape=jax.ShapeDtypeStruct((bsz * seq, heads * hd), BF16),
        grid=(bsz, heads, nq),
        in_specs=[
            pl.BlockSpec((tq, hd), lambda b, h, i: (b * nq + i, h)),
            pl.BlockSpec((seq, hd), lambda b, h, i: (b, heads + h)),
            pl.BlockSpec((seq, hd), lambda b, h, i: (b, 2 * heads + h)),
            pl.BlockSpec((ctx_len, hd), lambda b, h, i: (b, heads + h)),
            pl.BlockSpec((ctx_len, hd), lambda b, h, i: (b, 2 * heads + h)),
            pl.BlockSpec((tq, hd), lambda b, h, i: (i, 0)),
            pl.BlockSpec((tq, hd), lambda b, h, i: (i, 0)),
            pl.BlockSpec((seq, hd), lambda b, h, i: (0, 0)),
            pl.BlockSpec((seq, hd), lambda b, h, i: (0, 0)),
            pl.BlockSpec((4, hd // 2), lambda b, h, i: (0, 0)),
            pl.BlockSpec((1, hd), lambda b, h, i: (0, 0)),
        ],
        out_specs=pl.BlockSpec((tq, hd), lambda b, h, i: (b * nq + i, h)),
        scratch_shapes=[pltpu.VMEM((nk, hd), BF16), pltpu.VMEM((hd, nk), BF16)],
        compiler_params=_params("parallel", "parallel", "arbitrary"),
        name="diff_lat",
    )(p_lat, p_lat, p_lat, p_ctx, p_ctx, cosq, sinq, cos, sin, lam_p, sub_g.reshape(1, hd))


def _diff_ctx(p_ctx, lam_p, sub_g, bsz, ctx_len, heads, lam_init):
    hd = HEAD_DIM
    return pl.pallas_call(
        functools.partial(_diff_kernel, n_lat=0, rope=False, lam_init=lam_init),
        out_shape=jax.ShapeDtypeStruct((bsz * ctx_len, heads * hd), BF16),
        grid=(bsz, heads, 1),
        in_specs=[
            pl.BlockSpec((ctx_len, hd), lambda b, h, i: (b, h)),
            pl.BlockSpec((ctx_len, hd), lambda b, h, i: (b, heads + h)),
            pl.BlockSpec((ctx_len, hd), lambda b, h, i: (b, 2 * heads + h)),
            pl.BlockSpec((4, hd // 2), lambda b, h, i: (0, 0)),
            pl.BlockSpec((1, hd), lambda b, h, i: (0, 0)),
        ],
        out_specs=pl.BlockSpec((ctx_len, hd), lambda b, h, i: (b, h)),
        scratch_shapes=[pltpu.VMEM((ctx_len, hd), BF16), pltpu.VMEM((hd, ctx_len), BF16)],
        compiler_params=_params("parallel", "parallel", "arbitrary"),
        name="diff_ctx",
    )(p_ctx, p_ctx, p_ctx, lam_p, sub_g.reshape(1, hd))


def _rope_tables(seq):
    axis = HEAD_DIM // 4
    t = jnp.arange(seq)
    row = (t // GRID_W).astype(F32)
    col = (t % GRID_W).astype(F32)
    inv = ROPE_THETA ** (-jnp.arange(0, axis, 2, dtype=F32) / axis)
    ang_r = row[:, None] * inv
    ang_c = col[:, None] * inv
    cr, sr, cc, sc = jnp.cos(ang_r), jnp.sin(ang_r), jnp.cos(ang_c), jnp.sin(ang_c)
    cos = jnp.concatenate([cr, cr, cc, cc] * 2, axis=-1)
    sin = jnp.concatenate([-sr, sr, -sc, sc] * 2, axis=-1)
    return cos * _Q_SCALE, sin * _Q_SCALE, cos, sin


def _fnet_kernel(f_ref, cn_ref, sn_ref, cc_ref, sc_ref, o_ref, a_scr, b_scr, *, norm):
    @pl.when(pl.program_id(1) == 0)
    def _():
        step = 512 if f_ref.shape[0] % 512 == 0 else f_ref.shape[0]
        for c in range(f_ref.shape[0] // step):
            rs = slice(c * step, (c + 1) * step)
            f = f_ref[rs, :]
            a_scr[rs, :] = jnp.dot(f, cc_ref[...], preferred_element_type=F32).astype(BF16)
            b_scr[rs, :] = jnp.dot(f, sc_ref[...], preferred_element_type=F32).astype(BF16)

    y = jnp.dot(cn_ref[...], a_scr[...], preferred_element_type=F32)
    y = y - jnp.dot(sn_ref[...], b_scr[...], preferred_element_type=F32)
    o_ref[...] = (y * norm).astype(o_ref.dtype)


def _fnet(p, cn, sn, ccb, scb, bsz, n, fw, col):
    tr = _tile(n, 512)
    norm = 1.0 / math.sqrt(n * HEAD_DIM)
    return pl.pallas_call(
        functools.partial(_fnet_kernel, norm=norm),
        out_shape=jax.ShapeDtypeStruct((bsz * n, fw), BF16),
        grid=(bsz, n // tr),
        in_specs=[
            pl.BlockSpec((n, fw), lambda b, i: (b, col)),
            pl.BlockSpec((tr, n), lambda b, i: (i, 0)),
            pl.BlockSpec((tr, n), lambda b, i: (i, 0)),
            pl.BlockSpec((fw, fw), lambda b, i: (0, 0)),
            pl.BlockSpec((fw, fw), lambda b, i: (0, 0)),
        ],
        out_specs=pl.BlockSpec((tr, fw), lambda b, i: (b * (n // tr) + i, 0)),
        scratch_shapes=[pltpu.VMEM((n, fw), BF16), pltpu.VMEM((n, fw), BF16)],
        compiler_params=_params("parallel", "arbitrary"),
        name="fnet",
    )(p, cn, sn, ccb, scb)


def _dft_tables(n):
    if n <= 256:
        idx = np.outer(np.arange(n), np.arange(n)) % n
        ang = 2.0 * np.pi * idx / n
        return jnp.asarray(np.cos(ang), F32).astype(BF16), jnp.asarray(np.sin(ang), F32).astype(BF16)
    r = int(round(math.sqrt(n)))
    assert r * r == n
    k = np.arange(n)
    t = np.arange(r)
    ang_a = 2.0 * np.pi * (np.outer(t, k) % r) / r
    ang_b = 2.0 * np.pi * (np.outer(t, k) % n) / n
    ca, sa = jnp.asarray(np.cos(ang_a), F32)[:, None, :], jnp.asarray(np.sin(ang_a), F32)[:, None, :]
    cb, sb = jnp.asarray(np.cos(ang_b), F32)[None, :, :], jnp.asarray(np.sin(ang_b), F32)[None, :, :]
    cos = (ca * cb - sa * sb).reshape(n, n).astype(BF16)
    sin = (sa * cb + ca * sb).reshape(n, n).astype(BF16)
    return cos, sin


def _channel_dft_tables(fw):
    idx = np.outer(np.arange(HEAD_DIM), np.arange(HEAD_DIM)) % HEAD_DIM
    ang = 2.0 * np.pi * idx / HEAD_DIM
    eye = np.eye(fw // HEAD_DIM)
    return (jnp.asarray(np.kron(eye, np.cos(ang)), F32).astype(BF16),
            jnp.asarray(np.kron(eye, np.sin(ang)), F32).astype(BF16))


def kernel(x, c, ctx, c_ctx, ada_w, ada_b, norm1_g, norm2_g, w_in, w_out, sgu_norm_g, sgu_w, sgu_b, na_rpb,
           diff_lq1, diff_lk1, diff_lq2, diff_lk2, diff_subln_g, mlp_w1, mlp_w2, final_g):
    bsz, seq, d = x.shape
    ctx_len = ctx.shape[1]
    depth = ada_w.shape[0]
    rows = seq // GRID_W
    mix_heads = d // HEAD_DIM
    sgu_w_dim = (mix_heads // 2) * HEAD_DIM
    na_w = d - sgu_w_dim
    diff_heads = (mix_heads * 3) // 4
    fnet_w = d - diff_heads * HEAD_DIM
    assert sgu_w_dim == na_w and bsz + 1 <= MOD_ROWS

    cond = jnp.zeros((MOD_ROWS, d), F32).at[:bsz].set(c).at[bsz].set(c_ctx)
    mods = _ada_table(cond, ada_w, ada_b).reshape(depth, MOD_ROWS, 1, ADA_CHUNKS * d)

    w_in_b, w_out_b = w_in.astype(BF16), w_out.astype(BF16)
    w1_b, w2_b = mlp_w1.astype(BF16), mlp_w2.astype(BF16)
    sgu_w_b = sgu_w.astype(BF16)
    n1g = norm1_g.reshape(depth, 1, d)
    n2g = norm2_g.reshape(depth, 1, d)
    sgu_g = sgu_norm_g.reshape(-1, 1, sgu_w_dim)
    sgu_b_full = jnp.broadcast_to(sgu_b[..., None], sgu_b.shape + (HEAD_DIM,)).astype(F32)

    tm_lat = _tile(seq, 512)
    tm_ctx = _tile(bsz * ctx_len, 512)
    lat_row = lambda i: (i * tm_lat) // seq
    ctx_row = lambda i: bsz

    x_lat = x.reshape(bsz * seq, d)
    x_ctx = ctx.reshape(bsz * ctx_len, d)

    if depth > 1:
        rope_tabs = _rope_tables(seq)
        cn, sn = _dft_tables(seq)
        cn_c, sn_c = _dft_tables(ctx_len)
        ccb, scb = _channel_dft_tables(fnet_w)

    for l in range(depth):
        need_ctx = l < depth - 1
        p_lat = _in_proj(x_lat, mods, n1g, w_in_b, l, lat_row, tm_lat)
        p_ctx = _in_proj(x_ctx, mods, n1g, w_in_b, l, ctx_row, tm_ctx)
        i = l // 2
        ma_ctx = mb_ctx = None
        if l % 2 == 0:
            bias = _na_bias(na_rpb[i], rows)
            ma_lat = _sgu(p_lat, sgu_g, sgu_w_b, sgu_b_full, i, sgu_w_dim)
            mb_lat = _na_lat(p_lat, p_ctx, bias, bsz, seq, ctx_len, na_w, 2)
            if need_ctx:
                ma_ctx = _sgu(p_ctx, sgu_g, sgu_w_b, sgu_b_full, i, sgu_w_dim)
                mb_ctx = _na_ctx(p_ctx, bsz, ctx_len, na_w, 2)
        else:
            lam_init = 0.8 - 0.6 * math.exp(-0.3 * l)
            lam_p = jnp.stack([diff_lq1[i], diff_lk1[i], diff_lq2[i], diff_lk2[i]]).astype(F32)
            fcol = (3 * diff_heads * HEAD_DIM) // fnet_w
            ma_lat = _diff_lat(p_lat, p_ctx, rope_tabs, lam_p, diff_subln_g[i], bsz, seq, ctx_len, diff_heads, lam_init)
            mb_lat = _fnet(p_lat, cn, sn, ccb, scb, bsz, seq, fnet_w, fcol)
            if need_ctx:
                ma_ctx = _diff_ctx(p_ctx, lam_p, diff_subln_g[i], bsz, ctx_len, diff_heads, lam_init)
                mb_ctx = _fnet(p_ctx, cn_c, sn_c, ccb, scb, bsz, ctx_len, fnet_w, fcol)
        x_lat = _out_proj(x_lat, ma_lat, mb_lat, w_out_b, mods, l, lat_row, tm_lat)
        x_lat = _mlp(x_lat, mods, n2g, w1_b, w2_b, final_g, l, lat_row, tm_lat, final_norm=not need_ctx)
        if need_ctx:
            x_ctx = _out_proj(x_ctx, ma_ctx, mb_ctx, w_out_b, mods, l, ctx_row, tm_ctx)
            x_ctx = _mlp(x_ctx, mods, n2g, w1_b, w2_b, final_g, l, ctx_row, tm_ctx, final_norm=False)
    return x_lat.reshape(bsz, seq, d)
```

```python
import functools
import math

import numpy as np
import jax
import jax.numpy as jnp
from jax import lax
from jax.experimental import pallas as pl
from jax.experimental.pallas import tpu as pltpu

F32 = jnp.float32
BF16 = jnp.bfloat16

GRID_W = 64
HEAD_DIM = 128
CHUNK = 128
NA_KH = 8
NA_KW = 16
ROPE_THETA = 10000.0
Q_ROWS = 4
K_ROWS = Q_ROWS + NA_KH
RMS_EPS = 1e-6
LN_EPS = 1e-5
ADA_CHUNKS = 6
MOD_ROWS = 16

VMEM_LIMIT = 56 * 1024 * 1024

_NT = (((1,), (1,)), ((), ()))

_Q_SCALE = (HEAD_DIM // 2) ** -0.5 * math.log2(math.e)
DIFF_TQ = 512
DIFF_TK = 512
DIFF_AHEAD = 2
DIFF_BOUND_MARGIN = 1.01
DIFF_L_MIN = 2.0 ** -100
SUM_ROWS = 16


def _params(*sem):
    return pltpu.CompilerParams(dimension_semantics=sem, vmem_limit_bytes=VMEM_LIMIT)


def _tile(n, pref):
    if n <= pref:
        return n
    t = (pref // 128) * 128
    while n % t:
        t -= 128
    assert t > 0, (n, pref)
    return t


def _ada_kernel(s_ref, w_ref, b_ref, o_ref):
    s = s_ref[...]
    a = (s * jax.nn.sigmoid(s)).astype(BF16)
    o_ref[...] = jnp.dot(a, w_ref[...].astype(BF16), preferred_element_type=F32) + b_ref[...]


def _ada_table(cond, ada_w, ada_b):
    depth, d, n = ada_w.shape
    tn = _tile(n, 1024)
    return pl.pallas_call(
        _ada_kernel,
        out_shape=jax.ShapeDtypeStruct((depth, MOD_ROWS, n), F32),
        grid=(depth, n // tn),
        in_specs=[
            pl.BlockSpec((MOD_ROWS, d), lambda l, j: (0, 0)),
            pl.BlockSpec((None, d, tn), lambda l, j: (l, 0, j)),
            pl.BlockSpec((None, 1, tn), lambda l, j: (l, 0, j)),
        ],
        out_specs=pl.BlockSpec((None, MOD_ROWS, tn), lambda l, j: (l, 0, j)),
        compiler_params=_params("parallel", "parallel"),
        name="ada_table",
    )(cond, ada_w, ada_b.reshape(depth, 1, n))


def _norm_mod_rows(x_ref, g_ref, sh_ref, sc_ref, h_ref):
    rows = 64
    g = g_ref[...]
    sc1 = 1.0 + sc_ref[...]
    sh = sh_ref[...]

    def body(r, carry):
        sl = pl.ds(pl.multiple_of(r * rows, rows), rows)
        xv = x_ref[sl, :]
        ms = jnp.mean(xv * xv, axis=-1, keepdims=True)
        y = xv * lax.rsqrt(ms + RMS_EPS) * g
        h_ref[sl, :] = (y * sc1 + sh).astype(h_ref.dtype)
        return carry

    lax.fori_loop(0, x_ref.shape[0] // rows, body, 0)


def _mod_spec(d, layer, chunk, row_fn):
    return pl.BlockSpec((None, None, 1, d), lambda i, j: (layer, row_fn(i), 0, chunk))


def _layer_vec_spec(d, layer):
    return pl.BlockSpec((None, 1, d), lambda i, j: (layer, 0, 0))


def _in_kernel(x_ref, g_ref, sh_ref, sc_ref, w_ref, o_ref, h_scr):
    @pl.when(pl.program_id(1) == 0)
    def _():
        _norm_mod_rows(x_ref, g_ref, sh_ref, sc_ref, h_scr)

    o_ref[...] = jnp.dot(h_scr[...], w_ref[...], preferred_element_type=F32).astype(o_ref.dtype)


def _in_proj(x, mods, g, w, layer, row_fn, tm):
    m, d = x.shape
    n = w.shape[2]
    tn = _tile(n, 2560)
    return pl.pallas_call(
        _in_kernel,
        out_shape=jax.ShapeDtypeStruct((m, n), BF16),
        grid=(m // tm, n // tn),
        in_specs=[
            pl.BlockSpec((tm, d), lambda i, j: (i, 0)),
            _layer_vec_spec(d, layer),
            _mod_spec(d, layer, 0, row_fn),
            _mod_spec(d, layer, 1, row_fn),
            pl.BlockSpec((None, d, tn), lambda i, j: (layer, 0, j)),
        ],
        out_specs=pl.BlockSpec((tm, tn), lambda i, j: (i, j)),
        scratch_shapes=[pltpu.VMEM((tm, d), BF16)],
        compiler_params=_params("parallel", "arbitrary"),
        name="in_proj",
    )(x, g, mods, mods, w)


def _out_kernel(x_ref, ma_ref, mb_ref, wa_ref, wb_ref, gate_ref, o_ref):
    y = jnp.dot(ma_ref[...], wa_ref[...], preferred_element_type=F32)
    y = y + jnp.dot(mb_ref[...], wb_ref[...], preferred_element_type=F32)
    o_ref[...] = x_ref[...] + gate_ref[...] * y


def _out_proj(x, ma, mb, w, mods, layer, row_fn, tm):
    m, d = x.shape
    ka, kb = ma.shape[1], mb.shape[1]
    assert ka % kb == 0 and w.shape[1:] == (ka + kb, d)
    return pl.pallas_call(
        _out_kernel,
        out_shape=jax.ShapeDtypeStruct((m, d), F32),
        grid=(m // tm, 1),
        in_specs=[
            pl.BlockSpec((tm, d), lambda i, j: (i, 0)),
            pl.BlockSpec((tm, ka), lambda i, j: (i, 0)),
            pl.BlockSpec((tm, kb), lambda i, j: (i, 0)),
            pl.BlockSpec((None, ka, d), lambda i, j: (layer, 0, 0)),
            pl.BlockSpec((None, kb, d), lambda i, j: (layer, ka // kb, 0)),
            _mod_spec(d, layer, 2, row_fn),
        ],
        out_specs=pl.BlockSpec((tm, d), lambda i, j: (i, 0)),
        compiler_params=_params("parallel", "arbitrary"),
        name="out_proj",
    )(x, ma, mb, w, w, mods)


def _mlp_kernel(x_ref, g_ref, sh_ref, sc_ref, gate_ref, w1_ref, w2_ref, fg_ref, o_ref, h_scr, acc_scr, *, final_norm):
    f = pl.program_id(1)

    @pl.when(f == 0)
    def _():
        _norm_mod_rows(x_ref, g_ref, sh_ref, sc_ref, h_scr)
        acc_scr[...] = jnp.zeros_like(acc_scr)

    a = jnp.dot(h_scr[...], w1_ref[...], preferred_element_type=F32)
    a = jnp.square(jnp.maximum(a, 0.0)).astype(BF16)
    acc_scr[...] += jnp.dot(a, w2_ref[...], preferred_element_type=F32)

    @pl.when(f == pl.num_programs(1) - 1)
    def _():
        out = x_ref[...] + gate_ref[...] * acc_scr[...]
        if final_norm:
            ms = jnp.mean(out * out, axis=-1, keepdims=True)
            out = out * lax.rsqrt(ms + RMS_EPS) * fg_ref[...]
        o_ref[...] = out


def _mlp(x, mods, g, w1, w2, final_g, layer, row_fn, tm, final_norm):
    m, d = x.shape
    dff = w1.shape[2]
    tf = _tile(dff, 1024)
    return pl.pallas_call(
        functools.partial(_mlp_kernel, final_norm=final_norm),
        out_shape=jax.ShapeDtypeStruct((m, d), F32),
        grid=(m // tm, dff // tf),
        in_specs=[
            pl.BlockSpec((tm, d), lambda i, j: (i, 0)),
            _layer_vec_spec(d, layer),
            _mod_spec(d, layer, 3, row_fn),
            _mod_spec(d, layer, 4, row_fn),
            _mod_spec(d, layer, 5, row_fn),
            pl.BlockSpec((None, d, tf), lambda i, j: (layer, 0, j)),
            pl.BlockSpec((None, tf, d), lambda i, j: (layer, j, 0)),
            pl.BlockSpec((1, d), lambda i, j: (0, 0)),
        ],
        out_specs=pl.BlockSpec((tm, d), lambda i, j: (i, 0)),
        scratch_shapes=[pltpu.VMEM((tm, d), BF16), pltpu.VMEM((tm, d), F32)],
        compiler_params=_params("parallel", "arbitrary"),
        name="mlp",
    )(x, g, mods, mods, mods, w1, w2, final_g.reshape(1, d))


def _sgu_kernel(u_ref, v_ref, gam_ref, ws_ref, b_ref, o_ref, *, groups):
    for c in range(u_ref.shape[0] // CHUNK):
        rs = slice(c * CHUNK, (c + 1) * CHUNK)
        for g in range(groups):
            cs = slice(g * HEAD_DIM, (g + 1) * HEAD_DIM)
            v = v_ref[rs, cs].astype(F32)
            mu = jnp.mean(v, axis=-1, keepdims=True)
            dv = v - mu
            var = jnp.mean(dv * dv, axis=-1, keepdims=True)
            vn = (dv * lax.rsqrt(var + LN_EPS) * gam_ref[:, cs]).astype(BF16)
            mixed = jnp.dot(ws_ref[g], vn, preferred_element_type=F32) + b_ref[g]
            o_ref[rs, cs] = (u_ref[rs, cs].astype(F32) * mixed).astype(o_ref.dtype)


def _sgu(p, gam, ws, b_full, layer, sgu_w):
    m = p.shape[0]
    groups = sgu_w // HEAD_DIM
    tr = _tile(m, 512)
    return pl.pallas_call(
        functools.partial(_sgu_kernel, groups=groups),
        out_shape=jax.ShapeDtypeStruct((m, sgu_w), BF16),
        grid=(m // tr,),
        in_specs=[
            pl.BlockSpec((tr, sgu_w), lambda i: (i, 0)),
            pl.BlockSpec((tr, sgu_w), lambda i: (i, 1)),
            pl.BlockSpec((None, 1, sgu_w), lambda i: (layer, 0, 0)),
            pl.BlockSpec((None, groups, CHUNK, CHUNK), lambda i: (layer, 0, 0, 0)),
            pl.BlockSpec((None, groups, CHUNK, HEAD_DIM), lambda i: (layer, 0, 0, 0)),
        ],
        out_specs=pl.BlockSpec((tr, sgu_w), lambda i: (i, 0)),
        compiler_params=_params("parallel"),
        name="sgu",
    )(p, p, gam, ws, b_full)


def _softmax_pv(scores, values, q_rows):
    m = scores[0].max(axis=-1, keepdims=True)
    for s in scores[1:]:
        m = jnp.maximum(m, s.max(axis=-1, keepdims=True))
    l = jnp.zeros((q_rows, 1), F32)
    o = None
    for s, v in zip(scores, values):
        p = jnp.exp(s - m)
        l = l + p.sum(axis=-1, keepdims=True)
        pv = jnp.dot(p.astype(BF16), v, preferred_element_type=F32)
        o = pv if o is None else o + pv
    return o / l


def _na_kernel(q_ref, k0_ref, k1_ref, k2_ref, v0_ref, v1_ref, v2_ref, kc_ref, vc_ref, bias_ref, o_ref, *, heads):
    qb = q_ref.shape[0]
    scale = HEAD_DIM ** -0.5
    for h in range(heads):
        cs = slice(h * HEAD_DIM, (h + 1) * HEAD_DIM)
        q = (q_ref[:, cs].astype(F32) * scale).astype(BF16)
        scores, values = [], []
        for j, (k_ref, v_ref) in enumerate(((k0_ref, v0_ref), (k1_ref, v1_ref), (k2_ref, v2_ref))):
            s = lax.dot_general(q, k_ref[:, cs], _NT, preferred_element_type=F32)
            scores.append(s + bias_ref[h, :, j * qb:(j + 1) * qb])
            values.append(v_ref[:, cs])
        scores.append(lax.dot_general(q, kc_ref[:, cs], _NT, preferred_element_type=F32))
        values.append(vc_ref[:, cs])
        o_ref[:, cs] = _softmax_pv(scores, values, qb).astype(o_ref.dtype)


def _na_bias(rpb, rows):
    heads = rpb.shape[0]
    kw = NA_KW
    col = jnp.arange(GRID_W)
    col_start = jnp.clip(col - kw // 2, 0, GRID_W - kw)
    col_mask = (col[None, :] >= col_start[:, None]) & (col[None, :] < col_start[:, None] + kw)
    dx = jnp.clip(col[None, :] - col[:, None], -(kw - 1), kw - 1) + (kw - 1)
    rpb_x = jnp.where(col_mask[None, None], rpb.astype(F32)[:, :, dx], -jnp.inf)
    nblk = rows // Q_ROWS
    dy_idx = np.zeros((3, Q_ROWS, K_ROWS), np.int32)
    valid = np.zeros((3, Q_ROWS, K_ROWS), bool)
    for t, kblk in enumerate((0, 1, nblk - 1)):
        ks = min(max(Q_ROWS * kblk - NA_KH // 2, 0), rows - K_ROWS)
        for i in range(Q_ROWS):
            r = Q_ROWS * kblk + i
            rs = min(max(r - NA_KH // 2, 0), rows - NA_KH)
            for j in range(K_ROWS):
                kr = ks + j
                valid[t, i, j] = rs <= kr < rs + NA_KH
                dy_idx[t, i, j] = min(max(kr - r + NA_KH - 1, 0), 2 * NA_KH - 2)
    b = rpb_x[:, dy_idx]
    b = jnp.where(jnp.asarray(valid)[None, :, :, :, None, None], b, -jnp.inf)
    b = b.transpose(1, 0, 2, 4, 3, 5)
    return b.reshape(3, heads, Q_ROWS * GRID_W, K_ROWS * GRID_W)


def _na_lat(p_lat, p_ctx, bias, bsz, seq, ctx_len, na_w, col0):
    heads = na_w // HEAD_DIM
    qb = Q_ROWS * GRID_W
    nblk = seq // qb
    assert nblk >= 4 and K_ROWS == 3 * Q_ROWS

    def kblock(j):
        return lambda b, k: (b * nblk + jnp.clip(k - 1, 0, nblk - 3) + j, col0 + 1)

    def vblock(j):
        return lambda b, k: (b * nblk + jnp.clip(k - 1, 0, nblk - 3) + j, col0 + 2)

    def btype(b, k):
        return (jnp.where(k == 0, 0, jnp.where(k == nblk - 1, 2, 1)), 0, 0, 0)

    return pl.pallas_call(
        functools.partial(_na_kernel, heads=heads),
        out_shape=jax.ShapeDtypeStruct((bsz * seq, na_w), BF16),
        grid=(bsz, nblk),
        in_specs=[
            pl.BlockSpec((qb, na_w), lambda b, k: (b * nblk + k, col0)),
            pl.BlockSpec((qb, na_w), kblock(0)),
            pl.BlockSpec((qb, na_w), kblock(1)),
            pl.BlockSpec((qb, na_w), kblock(2)),
            pl.BlockSpec((qb, na_w), vblock(0)),
            pl.BlockSpec((qb, na_w), vblock(1)),
            pl.BlockSpec((qb, na_w), vblock(2)),
            pl.BlockSpec((ctx_len, na_w), lambda b, k: (b, col0 + 1)),
            pl.BlockSpec((ctx_len, na_w), lambda b, k: (b, col0 + 2)),
            pl.BlockSpec((None, heads, qb, K_ROWS * GRID_W), btype),
        ],
        out_specs=pl.BlockSpec((qb, na_w), lambda b, k: (b * nblk + k, 0)),
        compiler_params=_params("parallel", "arbitrary"),
        name="na_lat",
    )(p_lat, p_lat, p_lat, p_lat, p_lat, p_lat, p_lat, p_ctx, p_ctx, bias)


def _attn_ctx_kernel(q_ref, k_ref, v_ref, o_ref, *, heads):
    scale = HEAD_DIM ** -0.5
    for h in range(heads):
        cs = slice(h * HEAD_DIM, (h + 1) * HEAD_DIM)
        q = (q_ref[:, cs].astype(F32) * scale).astype(BF16)
        s = lax.dot_general(q, k_ref[:, cs], _NT, preferred_element_type=F32)
        o_ref[:, cs] = _softmax_pv([s], [v_ref[:, cs]], q_ref.shape[0]).astype(o_ref.dtype)


def _na_ctx(p_ctx, bsz, ctx_len, na_w, col0):
    heads = na_w // HEAD_DIM
    return pl.pallas_call(
        functools.partial(_attn_ctx_kernel, heads=heads),
        out_shape=jax.ShapeDtypeStruct((bsz * ctx_len, na_w), BF16),
        grid=(bsz,),
        in_specs=[pl.BlockSpec((ctx_len, na_w), lambda b, c=c: (b, col0 + c)) for c in range(3)],
        out_specs=pl.BlockSpec((ctx_len, na_w), lambda b: (b, 0)),
        compiler_params=_params("parallel"),
        name="na_ctx",
    )(p_ctx, p_ctx, p_ctx)


def _swap16(x):
    lane = lax.broadcasted_iota(jnp.int32, x.shape, 1)
    up = pltpu.roll(x, HEAD_DIM - 16, 1)
    down = pltpu.roll(x, 16, 1)
    return jnp.where((lane & 16) == 0, up, down)


def _diff_kernel(*refs, n_lat, rope, lam_init):
    if rope:
        (q_ref, k_ref, v_ref, kc_ref, vc_ref, cosq_ref, sinq_ref, cos_ref, sin_ref, lam_ref, g_ref,
         o_ref, k_scr, vt_scr, kmax_scr) = refs
    else:
        q_ref, kc_ref, vc_ref, lam_ref, g_ref, o_ref, k_scr, vt_scr, kmax_scr = refs
    n_ctx = kc_ref.shape[0]
    nk = n_lat + n_ctx
    hd = HEAD_DIM
    half = hd // 2

    @pl.when(pl.program_id(2) == 0)
    def _():
        step = 512
        row = lax.broadcasted_iota(jnp.int32, (hd, hd), 0)
        col = lax.broadcasted_iota(jnp.int32, (hd, hd), 1)
        sel = jnp.where(((col == 0) & (row < half)) | ((col == 1) & (row >= half)), 1.0, 0.0).astype(BF16)

        def put_keys(rs, kr, kmax):
            n = kr.shape[0]
            k_scr[rs, :hd] = kr
            lane = lax.broadcasted_iota(jnp.int32, (n, hd), 1)
            k_scr[rs, hd:] = jnp.where(lane == 0, 1.0, 0.0).astype(BF16)
            kf = kr.astype(F32)
            sq = jnp.dot((kf * kf).astype(BF16), sel, preferred_element_type=F32).max(axis=0, keepdims=True)
            return sq if kmax is None else jnp.maximum(kmax, sq)

        kmax = None
        for c in range(n_lat // step):
            rs = slice(c * step, (c + 1) * step)
            k = k_ref[rs, :].astype(F32)
            kmax = put_keys(rs, (k * cos_ref[rs, :] + _swap16(k) * sin_ref[rs, :]).astype(BF16), kmax)
            vt_scr[:hd, rs] = v_ref[rs, :].astype(F32).T.astype(BF16)
        kmax = put_keys(slice(n_lat, nk), kc_ref[...], kmax)
        vt_scr[:hd, n_lat:nk] = vc_ref[...].astype(F32).T.astype(BF16)
        vt_scr[hd:, :] = jnp.ones((SUM_ROWS, nk), BF16)
        kmax_scr[...] = jnp.broadcast_to(kmax, kmax_scr.shape)

    q = q_ref[...].astype(F32)
    if rope:
        q = q * cosq_ref[...] + _swap16(q) * sinq_ref[...]
    else:
        q = q * _Q_SCALE
    tq = q.shape[0]
    q = q.astype(BF16).astype(F32)
    lane = lax.broadcasted_iota(jnp.int32, q.shape, 1)
    q2 = q * q
    kmax = kmax_scr[0:1, :]
    u1 = jnp.sqrt(jnp.sum(jnp.where(lane < half, q2, 0.0), axis=-1, keepdims=True) * kmax[:, 0:1]) * DIFF_BOUND_MARGIN
    u2 = jnp.sqrt(jnp.sum(jnp.where(lane >= half, q2, 0.0), axis=-1, keepdims=True) * kmax[:, 1:2]) * DIFF_BOUND_MARGIN
    q_aug = jnp.concatenate([
        jnp.concatenate([jnp.where(lane < half, q, 0.0), jnp.where(lane == 0, -u1, 0.0)], axis=1),
        jnp.concatenate([jnp.where(lane >= half, q, 0.0), jnp.where(lane == 0, -u2, 0.0)], axis=1),
    ], axis=0).astype(BF16)

    lp = lam_ref[...]
    d1 = jnp.sum(lp[0:1, :] * lp[1:2, :], axis=-1, keepdims=True)
    d2 = jnp.sum(lp[2:3, :] * lp[3:4, :], axis=-1, keepdims=True)
    lam = jnp.exp(d1) - jnp.exp(d2) + lam_init

    chunks = [(k0, min(DIFF_TK, nk - k0)) for k0 in range(0, nk, DIFF_TK)]

    def finish(acc):
        l = acc[hd:hd + 1, :]
        c1 = 1.0 / l[:, :tq]
        c2 = lam / l[:, tq:]
        ot = acc[:hd, :tq] * c1 - acc[:hd, tq:] * c2
        ot = ot * lax.rsqrt(jnp.mean(ot * ot, axis=0, keepdims=True) + RMS_EPS)
        o_ref[...] = ((ot.T * g_ref[...]) * (1.0 - lam_init)).astype(o_ref.dtype)

    def pipelined(scores, consume):
        pending = [scores(c) for c in range(min(DIFF_AHEAD, len(chunks)))]
        for c in range(len(chunks)):
            s = pending.pop(0)
            if c + DIFF_AHEAD < len(chunks):
                pending.append(scores(c + DIFF_AHEAD))
            consume(c, s)

    state = {}

    def fast_scores(c):
        k0, tk = chunks[c]
        return lax.dot_general(k_scr[k0:k0 + tk, :], q_aug, _NT, preferred_element_type=F32)

    def fast_consume(c, s):
        k0, tk = chunks[c]
        pv = jnp.dot(vt_scr[:, k0:k0 + tk], jnp.exp2(s).astype(BF16), preferred_element_type=F32)
        state["acc"] = pv if c == 0 else state["acc"] + pv

    pipelined(fast_scores, fast_consume)
    acc = state["acc"]
    finish(acc)

    @pl.when(jnp.logical_not(jnp.min(acc[hd:hd + 1, :]) > DIFF_L_MIN))
    def _():
        qab = q_aug[:, :hd]
        st = {}

        def safe_scores(c):
            k0, tk = chunks[c]
            return lax.dot_general(k_scr[k0:k0 + tk, :hd], qab, _NT, preferred_element_type=F32)

        def safe_consume(c, s):
            k0, tk = chunks[c]
            mc = s.max(axis=0, keepdims=True)
            m_new = mc if c == 0 else jnp.maximum(st["m"], mc)
            pv = jnp.dot(vt_scr[:, k0:k0 + tk], jnp.exp2(s - m_new).astype(BF16), preferred_element_type=F32)
            st["acc"] = pv if c == 0 else st["acc"] * jnp.exp2(st["m"] - m_new) + pv
            st["m"] = m_new

        pipelined(safe_scores, safe_consume)
        finish(st["acc"])


def _diff_lat(p_lat, p_ctx, tabs, lam_p, sub_g, bsz, seq, ctx_len, heads, lam_init):
    tq = DIFF_TQ
    nq = seq // tq
    hd = HEAD_DIM
    nk = seq + ctx_len
    cosq, sinq, cos, sin = tabs
    return pl.pallas_call(
        functools.partial(_diff_kernel, n_lat=seq, rope=True, lam_init=lam_init),
        out_shape=jax.ShapeDtypeStruct((bsz * seq, heads * hd), BF16),
        grid=(bsz, heads, nq),
        in_specs=[
            pl.BlockSpec((tq, hd), lambda b, h, i: (b * nq + i, h)),
            pl.BlockSpec((seq, hd), lambda b, h, i: (b, heads + h)),
            pl.BlockSpec((seq, hd), lambda b, h, i: (b, 2 * heads + h)),
            pl.BlockSpec((ctx_len, hd), lambda b, h, i: (b, heads + h)),
            pl.BlockSpec((ctx_len, hd), lambda b, h, i: (b, 2 * heads + h)),
            pl.BlockSpec((tq, hd), lambda b, h, i: (i, 0)),
            pl.BlockSpec((tq, hd), lambda b, h, i: (i, 0)),
            pl.BlockSpec((seq, hd), lambda b, h, i: (0, 0)),
            pl.BlockSpec((seq, hd), lambda b, h, i: (0, 0)),
            pl.BlockSpec((4, hd // 2), lambda b, h, i: (0, 0)),
            pl.BlockSpec((1, hd), lambda b, h, i: (0, 0)),
        ],
        out_specs=pl.BlockSpec((tq, hd), lambda b, h, i: (b * nq + i, h)),
        scratch_shapes=[pltpu.VMEM((nk, 2 * hd), BF16), pltpu.VMEM((hd + SUM_ROWS, nk), BF16), pltpu.VMEM((8, hd), F32)],
        compiler_params=_params("parallel", "parallel", "arbitrary"),
        name="diff_lat",
    )(p_lat, p_lat, p_lat, p_ctx, p_ctx, cosq, sinq, cos, sin, lam_p, sub_g.reshape(1, hd))


def _diff_ctx(p_ctx, lam_p, sub_g, bsz, ctx_len, heads, lam_init):
    hd = HEAD_DIM
    return pl.pallas_call(
        functools.partial(_diff_kernel, n_lat=0, rope=False, lam_init=lam_init),
        out_shape=jax.ShapeDtypeStruct((bsz * ctx_len, heads * hd), BF16),
        grid=(bsz, heads, 1),
        in_specs=[
            pl.BlockSpec((ctx_len, hd), lambda b, h, i: (b, h)),
            pl.BlockSpec((ctx_len, hd), lambda b, h, i: (b, heads + h)),
            pl.BlockSpec((ctx_len, hd), lambda b, h, i: (b, 2 * heads + h)),
            pl.BlockSpec((4, hd // 2), lambda b, h, i: (0, 0)),
            pl.BlockSpec((1, hd), lambda b, h, i: (0, 0)),
        ],
        out_specs=pl.BlockSpec((ctx_len, hd), lambda b, h, i: (b, h)),
        scratch_shapes=[pltpu.VMEM((ctx_len, 2 * hd), BF16), pltpu.VMEM((hd + SUM_ROWS, ctx_len), BF16),
                        pltpu.VMEM((8, hd), F32)],
        compiler_params=_params("parallel", "parallel", "arbitrary"),
        name="diff_ctx",
    )(p_ctx, p_ctx, p_ctx, lam_p, sub_g.reshape(1, hd))


def _rope_tables(seq):
    axis = HEAD_DIM // 4
    t = jnp.arange(seq)
    row = (t // GRID_W).astype(F32)
    col = (t % GRID_W).astype(F32)
    inv = ROPE_THETA ** (-jnp.arange(0, axis, 2, dtype=F32) / axis)
    ang_r = row[:, None] * inv
    ang_c = col[:, None] * inv
    cr, sr, cc, sc = jnp.cos(ang_r), jnp.sin(ang_r), jnp.cos(ang_c), jnp.sin(ang_c)
    cos = jnp.concatenate([cr, cr, cc, cc] * 2, axis=-1)
    sin = jnp.concatenate([-sr, sr, -sc, sc] * 2, axis=-1)
    return cos * _Q_SCALE, sin * _Q_SCALE, cos, sin


def _fnet_kernel(f_ref, cn_ref, sn_ref, cc_ref, sc_ref, o_ref, a_scr, b_scr, *, norm):
    @pl.when(pl.program_id(1) == 0)
    def _():
        step = 512 if f_ref.shape[0] % 512 == 0 else f_ref.shape[0]
        for c in range(f_ref.shape[0] // step):
            rs = slice(c * step, (c + 1) * step)
            f = f_ref[rs, :]
            a_scr[rs, :] = jnp.dot(f, cc_ref[...], preferred_element_type=F32).astype(BF16)
            b_scr[rs, :] = jnp.dot(f, sc_ref[...], preferred_element_type=F32).astype(BF16)

    y = jnp.dot(cn_ref[...], a_scr[...], preferred_element_type=F32)
    y = y - jnp.dot(sn_ref[...], b_scr[...], preferred_element_type=F32)
    o_ref[...] = (y * norm).astype(o_ref.dtype)


def _fnet(p, cn, sn, ccb, scb, bsz, n, fw, col):
    tr = _tile(n, 512)
    norm = 1.0 / math.sqrt(n * HEAD_DIM)
    return pl.pallas_call(
        functools.partial(_fnet_kernel, norm=norm),
        out_shape=jax.ShapeDtypeStruct((bsz * n, fw), BF16),
        grid=(bsz, n // tr),
        in_specs=[
            pl.BlockSpec((n, fw), lambda b, i: (b, col)),
            pl.BlockSpec((tr, n), lambda b, i: (i, 0)),
            pl.BlockSpec((tr, n), lambda b, i: (i, 0)),
            pl.BlockSpec((fw, fw), lambda b, i: (0, 0)),
            pl.BlockSpec((fw, fw), lambda b, i: (0, 0)),
        ],
        out_specs=pl.BlockSpec((tr, fw), lambda b, i: (b * (n // tr) + i, 0)),
        scratch_shapes=[pltpu.VMEM((n, fw), BF16), pltpu.VMEM((n, fw), BF16)],
        compiler_params=_params("parallel", "arbitrary"),
        name="fnet",
    )(p, cn, sn, ccb, scb)


def _dft_tables(n):
    if n <= 256:
        idx = np.outer(np.arange(n), np.arange(n)) % n
        ang = 2.0 * np.pi * idx / n
        return jnp.asarray(np.cos(ang), F32).astype(BF16), jnp.asarray(np.sin(ang), F32).astype(BF16)
    r = int(round(math.sqrt(n)))
    assert r * r == n
    k = np.arange(n)
    t = np.arange(r)
    ang_a = 2.0 * np.pi * (np.outer(t, k) % r) / r
    ang_b = 2.0 * np.pi * (np.outer(t, k) % n) / n
    ca, sa = jnp.asarray(np.cos(ang_a), F32)[:, None, :], jnp.asarray(np.sin(ang_a), F32)[:, None, :]
    cb, sb = jnp.asarray(np.cos(ang_b), F32)[None, :, :], jnp.asarray(np.sin(ang_b), F32)[None, :, :]
    cos = (ca * cb - sa * sb).reshape(n, n).astype(BF16)
    sin = (sa * cb + ca * sb).reshape(n, n).astype(BF16)
    return cos, sin


def _channel_dft_tables(fw):
    idx = np.outer(np.arange(HEAD_DIM), np.arange(HEAD_DIM)) % HEAD_DIM
    ang = 2.0 * np.pi * idx / HEAD_DIM
    eye = np.eye(fw // HEAD_DIM)
    return (jnp.asarray(np.kron(eye, np.cos(ang)), F32).astype(BF16),
            jnp.asarray(np.kron(eye, np.sin(ang)), F32).astype(BF16))


def kernel(x, c, ctx, c_ctx, ada_w, ada_b, norm1_g, norm2_g, w_in, w_out, sgu_norm_g, sgu_w, sgu_b, na_rpb,
           diff_lq1, diff_lk1, diff_lq2, diff_lk2, diff_subln_g, mlp_w1, mlp_w2, final_g):
    bsz, seq, d = x.shape
    ctx_len = ctx.shape[1]
    depth = ada_w.shape[0]
    rows = seq // GRID_W
    mix_heads = d // HEAD_DIM
    sgu_w_dim = (mix_heads // 2) * HEAD_DIM
    na_w = d - sgu_w_dim
    diff_heads = (mix_heads * 3) // 4
    fnet_w = d - diff_heads * HEAD_DIM
    assert sgu_w_dim == na_w and bsz + 1 <= MOD_ROWS

    cond = jnp.zeros((MOD_ROWS, d), F32).at[:bsz].set(c).at[bsz].set(c_ctx)
    mods = _ada_table(cond, ada_w, ada_b).reshape(depth, MOD_ROWS, 1, ADA_CHUNKS * d)

    w_in_b, w_out_b = w_in.astype(BF16), w_out.astype(BF16)
    w1_b, w2_b = mlp_w1.astype(BF16), mlp_w2.astype(BF16)
    sgu_w_b = sgu_w.astype(BF16)
    n1g = norm1_g.reshape(depth, 1, d)
    n2g = norm2_g.reshape(depth, 1, d)
    sgu_g = sgu_norm_g.reshape(-1, 1, sgu_w_dim)
    sgu_b_full = jnp.broadcast_to(sgu_b[..., None], sgu_b.shape + (HEAD_DIM,)).astype(F32)

    tm_lat = _tile(seq, 512)
    tm_ctx = _tile(bsz * ctx_len, 512)
    lat_row = lambda i: (i * tm_lat) // seq
    ctx_row = lambda i: bsz

    x_lat = x.reshape(bsz * seq, d)
    x_ctx = ctx.reshape(bsz * ctx_len, d)

    if depth > 1:
        rope_tabs = _rope_tables(seq)
        cn, sn = _dft_tables(seq)
        cn_c, sn_c = _dft_tables(ctx_len)
        ccb, scb = _channel_dft_tables(fnet_w)

    for l in range(depth):
        need_ctx = l < depth - 1
        p_lat = _in_proj(x_lat, mods, n1g, w_in_b, l, lat_row, tm_lat)
        p_ctx = _in_proj(x_ctx, mods, n1g, w_in_b, l, ctx_row, tm_ctx)
        i = l // 2
        ma_ctx = mb_ctx = None
        if l % 2 == 0:
            bias = _na_bias(na_rpb[i], rows)
            ma_lat = _sgu(p_lat, sgu_g, sgu_w_b, sgu_b_full, i, sgu_w_dim)
            mb_lat = _na_lat(p_lat, p_ctx, bias, bsz, seq, ctx_len, na_w, 2)
            if need_ctx:
                ma_ctx = _sgu(p_ctx, sgu_g, sgu_w_b, sgu_b_full, i, sgu_w_dim)
                mb_ctx = _na_ctx(p_ctx, bsz, ctx_len, na_w, 2)
        else:
            lam_init = 0.8 - 0.6 * math.exp(-0.3 * l)
            lam_p = jnp.stack([diff_lq1[i], diff_lk1[i], diff_lq2[i], diff_lk2[i]]).astype(F32)
            fcol = (3 * diff_heads * HEAD_DIM) // fnet_w
            ma_lat = _diff_lat(p_lat, p_ctx, rope_tabs, lam_p, diff_subln_g[i], bsz, seq, ctx_len, diff_heads, lam_init)
            mb_lat = _fnet(p_lat, cn, sn, ccb, scb, bsz, seq, fnet_w, fcol)
            if need_ctx:
                ma_ctx = _diff_ctx(p_ctx, lam_p, diff_subln_g[i], bsz, ctx_len, diff_heads, lam_init)
                mb_ctx = _fnet(p_ctx, cn_c, sn_c, ccb, scb, bsz, ctx_len, fnet_w, fcol)
        x_lat = _out_proj(x_lat, ma_lat, mb_lat, w_out_b, mods, l, lat_row, tm_lat)
        x_lat = _mlp(x_lat, mods, n2g, w1_b, w2_b, final_g, l, lat_row, tm_lat, final_norm=not need_ctx)
        if need_ctx:
            x_ctx = _out_proj(x_ctx, ma_ctx, mb_ctx, w_out_b, mods, l, ctx_row, tm_ctx)
            x_ctx = _mlp(x_ctx, mods, n2g, w1_b, w2_b, final_g, l, ctx_row, tm_ctx, final_norm=False)
    return x_lat.reshape(bsz, seq, d)
```

```python
import functools
import math

import numpy as np
import jax
import jax.numpy as jnp
from jax import lax
from jax.experimental import pallas as pl
from jax.experimental.pallas import tpu as pltpu

F32 = jnp.float32
BF16 = jnp.bfloat16

GRID_W = 64
HEAD_DIM = 128
CHUNK = 128
NA_KH = 8
NA_KW = 16
ROPE_THETA = 10000.0
Q_ROWS = 4
K_ROWS = Q_ROWS + NA_KH
RMS_EPS = 1e-6
LN_EPS = 1e-5
ADA_CHUNKS = 6
MOD_ROWS = 16

VMEM_LIMIT = 56 * 1024 * 1024

_NT = (((1,), (1,)), ((), ()))

_Q_SCALE = (HEAD_DIM // 2) ** -0.5 * math.log2(math.e)
DIFF_TQ = 512
DIFF_TK = 1024
DIFF_AHEAD = 1
DIFF_BOUND_MARGIN = 1.01
DIFF_L_MIN = 2.0 ** -100
SUM_ROWS = 16


def _params(*sem):
    return pltpu.CompilerParams(dimension_semantics=sem, vmem_limit_bytes=VMEM_LIMIT)


def _tile(n, pref):
    if n <= pref:
        return n
    t = (pref // 128) * 128
    while n % t:
        t -= 128
    assert t > 0, (n, pref)
    return t


def _ada_kernel(s_ref, w_ref, b_ref, o_ref):
    s = s_ref[...]
    a = (s * jax.nn.sigmoid(s)).astype(BF16)
    o_ref[...] = jnp.dot(a, w_ref[...].astype(BF16), preferred_element_type=F32) + b_ref[...]


def _ada_table(cond, ada_w, ada_b):
    depth, d, n = ada_w.shape
    tn = _tile(n, 1024)
    return pl.pallas_call(
        _ada_kernel,
        out_shape=jax.ShapeDtypeStruct((depth, MOD_ROWS, n), F32),
        grid=(depth, n // tn),
        in_specs=[
            pl.BlockSpec((MOD_ROWS, d), lambda l, j: (0, 0)),
            pl.BlockSpec((None, d, tn), lambda l, j: (l, 0, j)),
            pl.BlockSpec((None, 1, tn), lambda l, j: (l, 0, j)),
        ],
        out_specs=pl.BlockSpec((None, MOD_ROWS, tn), lambda l, j: (l, 0, j)),
        compiler_params=_params("parallel", "parallel"),
        name="ada_table",
    )(cond, ada_w, ada_b.reshape(depth, 1, n))


def _norm_mod_rows(x_ref, g_ref, sh_ref, sc_ref, h_ref, unrolled=False):
    rows = 64
    g = g_ref[...]
    sc1 = 1.0 + sc_ref[...]
    sh = sh_ref[...]

    def body(r, carry):
        sl = slice(r * rows, (r + 1) * rows) if unrolled else pl.ds(pl.multiple_of(r * rows, rows), rows)
        xv = x_ref[sl, :]
        ms = jnp.mean(xv * xv, axis=-1, keepdims=True)
        y = xv * lax.rsqrt(ms + RMS_EPS) * g
        h_ref[sl, :] = (y * sc1 + sh).astype(h_ref.dtype)
        return carry

    if unrolled:
        for r in range(x_ref.shape[0] // rows):
            body(r, 0)
    else:
        lax.fori_loop(0, x_ref.shape[0] // rows, body, 0)


def _mod_spec(d, layer, chunk, row_fn):
    return pl.BlockSpec((None, None, 1, d), lambda i, j: (layer, row_fn(i), 0, chunk))


def _layer_vec_spec(d, layer):
    return pl.BlockSpec((None, 1, d), lambda i, j: (layer, 0, 0))


def _in_kernel(x0_ref, xn_ref, g_ref, sh0_ref, sc0_ref, shn_ref, scn_ref, w_ref, o_ref, h0_scr, h1_scr):
    i = pl.program_id(0)

    @pl.when(i == 0)
    def _():
        _norm_mod_rows(x0_ref, g_ref, sh0_ref, sc0_ref, h0_scr)

    def step(h_cur, h_next):
        o_ref[...] = jnp.dot(h_cur[...], w_ref[...], preferred_element_type=F32).astype(o_ref.dtype)
        _norm_mod_rows(xn_ref, g_ref, shn_ref, scn_ref, h_next, unrolled=True)

    @pl.when(i % 2 == 0)
    def _():
        step(h0_scr, h1_scr)

    @pl.when(i % 2 == 1)
    def _():
        step(h1_scr, h0_scr)


def _in_proj(x, mods, g, w, layer, row_fn, tm):
    m, d = x.shape
    n = w.shape[2]
    nt = m // tm
    nxt = lambda i: jnp.minimum(i + 1, nt - 1)
    once = pl.Buffered(1)
    return pl.pallas_call(
        _in_kernel,
        out_shape=jax.ShapeDtypeStruct((m, n), BF16),
        grid=(nt, 1),
        in_specs=[
            pl.BlockSpec((tm, d), lambda i, j: (0, 0), pipeline_mode=once),
            pl.BlockSpec((tm, d), lambda i, j: (nxt(i), 0)),
            _layer_vec_spec(d, layer),
            _mod_spec(d, layer, 0, lambda i: row_fn(0)),
            _mod_spec(d, layer, 1, lambda i: row_fn(0)),
            _mod_spec(d, layer, 0, lambda i: row_fn(nxt(i))),
            _mod_spec(d, layer, 1, lambda i: row_fn(nxt(i))),
            pl.BlockSpec((None, d, n), lambda i, j: (layer, 0, 0), pipeline_mode=once),
        ],
        out_specs=pl.BlockSpec((tm, n), lambda i, j: (i, 0)),
        scratch_shapes=[pltpu.VMEM((tm, d), BF16), pltpu.VMEM((tm, d), BF16)],
        compiler_params=_params("arbitrary", "arbitrary"),
        name="in_proj",
    )(x, x, g, mods, mods, mods, mods, w)


def _out_kernel(x_ref, ma_ref, mb_ref, wa_ref, wb_ref, gate_ref, o_ref):
    y = jnp.dot(ma_ref[...], wa_ref[...], preferred_element_type=F32)
    y = y + jnp.dot(mb_ref[...], wb_ref[...], preferred_element_type=F32)
    o_ref[...] = x_ref[...] + gate_ref[...] * y


def _out_proj(x, ma, mb, w, mods, layer, row_fn, tm):
    m, d = x.shape
    ka, kb = ma.shape[1], mb.shape[1]
    assert ka % kb == 0 and w.shape[1:] == (ka + kb, d)
    return pl.pallas_call(
        _out_kernel,
        out_shape=jax.ShapeDtypeStruct((m, d), F32),
        grid=(m // tm, 1),
        in_specs=[
            pl.BlockSpec((tm, d), lambda i, j: (i, 0)),
            pl.BlockSpec((tm, ka), lambda i, j: (i, 0)),
            pl.BlockSpec((tm, kb), lambda i, j: (i, 0)),
            pl.BlockSpec((None, ka, d), lambda i, j: (layer, 0, 0)),
            pl.BlockSpec((None, kb, d), lambda i, j: (layer, ka // kb, 0)),
            _mod_spec(d, layer, 2, row_fn),
        ],
        out_specs=pl.BlockSpec((tm, d), lambda i, j: (i, 0)),
        compiler_params=_params("parallel", "arbitrary"),
        name="out_proj",
    )(x, ma, mb, w, w, mods)


def _mlp_kernel(x_ref, g_ref, sh_ref, sc_ref, gate_ref, w1_ref, w2_ref, fg_ref, o_ref, h_scr, acc_scr, *, final_norm):
    f = pl.program_id(1)

    @pl.when(f == 0)
    def _():
        _norm_mod_rows(x_ref, g_ref, sh_ref, sc_ref, h_scr)
        acc_scr[...] = jnp.zeros_like(acc_scr)

    a = jnp.dot(h_scr[...], w1_ref[...], preferred_element_type=F32)
    a = jnp.square(jnp.maximum(a, 0.0)).astype(BF16)
    acc_scr[...] += jnp.dot(a, w2_ref[...], preferred_element_type=F32)

    @pl.when(f == pl.num_programs(1) - 1)
    def _():
        out = x_ref[...] + gate_ref[...] * acc_scr[...]
        if final_norm:
            ms = jnp.mean(out * out, axis=-1, keepdims=True)
            out = out * lax.rsqrt(ms + RMS_EPS) * fg_ref[...]
        o_ref[...] = out


def _mlp(x, mods, g, w1, w2, final_g, layer, row_fn, tm, final_norm):
    m, d = x.shape
    dff = w1.shape[2]
    tf = _tile(dff, 1024)
    return pl.pallas_call(
        functools.partial(_mlp_kernel, final_norm=final_norm),
        out_shape=jax.ShapeDtypeStruct((m, d), F32),
        grid=(m // tm, dff // tf),
        in_specs=[
            pl.BlockSpec((tm, d), lambda i, j: (i, 0)),
            _layer_vec_spec(d, layer),
            _mod_spec(d, layer, 3, row_fn),
            _mod_spec(d, layer, 4, row_fn),
            _mod_spec(d, layer, 5, row_fn),
            pl.BlockSpec((None, d, tf), lambda i, j: (layer, 0, j)),
            pl.BlockSpec((None, tf, d), lambda i, j: (layer, j, 0)),
            pl.BlockSpec((1, d), lambda i, j: (0, 0)),
        ],
        out_specs=pl.BlockSpec((tm, d), lambda i, j: (i, 0)),
        scratch_shapes=[pltpu.VMEM((tm, d), BF16), pltpu.VMEM((tm, d), F32)],
        compiler_params=_params("parallel", "arbitrary"),
        name="mlp",
    )(x, g, mods, mods, mods, w1, w2, final_g.reshape(1, d))


def _sgu_kernel(u_ref, v_ref, gam_ref, ws_ref, b_ref, o_ref, *, groups):
    for c in range(u_ref.shape[0] // CHUNK):
        rs = slice(c * CHUNK, (c + 1) * CHUNK)
        for g in range(groups):
            cs = slice(g * HEAD_DIM, (g + 1) * HEAD_DIM)
            v = v_ref[rs, cs].astype(F32)
            mu = jnp.mean(v, axis=-1, keepdims=True)
            dv = v - mu
            var = jnp.mean(dv * dv, axis=-1, keepdims=True)
            vn = (dv * lax.rsqrt(var + LN_EPS) * gam_ref[:, cs]).astype(BF16)
            mixed = jnp.dot(ws_ref[g], vn, preferred_element_type=F32) + b_ref[g]
            o_ref[rs, cs] = (u_ref[rs, cs].astype(F32) * mixed).astype(o_ref.dtype)


def _sgu(p, gam, ws, b_full, layer, sgu_w):
    m = p.shape[0]
    groups = sgu_w // HEAD_DIM
    tr = _tile(m, 512)
    return pl.pallas_call(
        functools.partial(_sgu_kernel, groups=groups),
        out_shape=jax.ShapeDtypeStruct((m, sgu_w), BF16),
        grid=(m // tr,),
        in_specs=[
            pl.BlockSpec((tr, sgu_w), lambda i: (i, 0)),
            pl.BlockSpec((tr, sgu_w), lambda i: (i, 1)),
            pl.BlockSpec((None, 1, sgu_w), lambda i: (layer, 0, 0)),
            pl.BlockSpec((None, groups, CHUNK, CHUNK), lambda i: (layer, 0, 0, 0)),
            pl.BlockSpec((None, groups, CHUNK, HEAD_DIM), lambda i: (layer, 0, 0, 0)),
        ],
        out_specs=pl.BlockSpec((tr, sgu_w), lambda i: (i, 0)),
        compiler_params=_params("parallel"),
        name="sgu",
    )(p, p, gam, ws, b_full)


def _softmax_pv(scores, values, q_rows):
    m = scores[0].max(axis=-1, keepdims=True)
    for s in scores[1:]:
        m = jnp.maximum(m, s.max(axis=-1, keepdims=True))
    l = jnp.zeros((q_rows, 1), F32)
    o = None
    for s, v in zip(scores, values):
        p = jnp.exp(s - m)
        l = l + p.sum(axis=-1, keepdims=True)
        pv = jnp.dot(p.astype(BF16), v, preferred_element_type=F32)
        o = pv if o is None else o + pv
    return o / l


def _na_kernel(q_ref, k0_ref, k1_ref, k2_ref, v0_ref, v1_ref, v2_ref, kc_ref, vc_ref, bias_ref, o_ref, *, heads):
    qb = q_ref.shape[0]
    scale = HEAD_DIM ** -0.5
    for h in range(heads):
        cs = slice(h * HEAD_DIM, (h + 1) * HEAD_DIM)
        q = (q_ref[:, cs].astype(F32) * scale).astype(BF16)
        scores, values = [], []
        for j, (k_ref, v_ref) in enumerate(((k0_ref, v0_ref), (k1_ref, v1_ref), (k2_ref, v2_ref))):
            s = lax.dot_general(q, k_ref[:, cs], _NT, preferred_element_type=F32)
            scores.append(s + bias_ref[h, :, j * qb:(j + 1) * qb])
            values.append(v_ref[:, cs])
        scores.append(lax.dot_general(q, kc_ref[:, cs], _NT, preferred_element_type=F32))
        values.append(vc_ref[:, cs])
        o_ref[:, cs] = _softmax_pv(scores, values, qb).astype(o_ref.dtype)


def _na_bias(rpb, rows):
    heads = rpb.shape[0]
    kw = NA_KW
    col = jnp.arange(GRID_W)
    col_start = jnp.clip(col - kw // 2, 0, GRID_W - kw)
    col_mask = (col[None, :] >= col_start[:, None]) & (col[None, :] < col_start[:, None] + kw)
    dx = jnp.clip(col[None, :] - col[:, None], -(kw - 1), kw - 1) + (kw - 1)
    rpb_x = jnp.where(col_mask[None, None], rpb.astype(F32)[:, :, dx], -jnp.inf)
    nblk = rows // Q_ROWS
    dy_idx = np.zeros((3, Q_ROWS, K_ROWS), np.int32)
    valid = np.zeros((3, Q_ROWS, K_ROWS), bool)
    for t, kblk in enumerate((0, 1, nblk - 1)):
        ks = min(max(Q_ROWS * kblk - NA_KH // 2, 0), rows - K_ROWS)
        for i in range(Q_ROWS):
            r = Q_ROWS * kblk + i
            rs = min(max(r - NA_KH // 2, 0), rows - NA_KH)
            for j in range(K_ROWS):
                kr = ks + j
                valid[t, i, j] = rs <= kr < rs + NA_KH
                dy_idx[t, i, j] = min(max(kr - r + NA_KH - 1, 0), 2 * NA_KH - 2)
    b = rpb_x[:, dy_idx]
    b = jnp.where(jnp.asarray(valid)[None, :, :, :, None, None], b, -jnp.inf)
    b = b.transpose(1, 0, 2, 4, 3, 5)
    return b.reshape(3, heads, Q_ROWS * GRID_W, K_ROWS * GRID_W)


def _na_lat(p_lat, p_ctx, bias, bsz, seq, ctx_len, na_w, col0):
    heads = na_w // HEAD_DIM
    qb = Q_ROWS * GRID_W
    nblk = seq // qb
    assert nblk >= 4 and K_ROWS == 3 * Q_ROWS

    def kblock(j):
        return lambda b, k: (b * nblk + jnp.clip(k - 1, 0, nblk - 3) + j, col0 + 1)

    def vblock(j):
        return lambda b, k: (b * nblk + jnp.clip(k - 1, 0, nblk - 3) + j, col0 + 2)

    def btype(b, k):
        return (jnp.where(k == 0, 0, jnp.where(k == nblk - 1, 2, 1)), 0, 0, 0)

    return pl.pallas_call(
        functools.partial(_na_kernel, heads=heads),
        out_shape=jax.ShapeDtypeStruct((bsz * seq, na_w), BF16),
        grid=(bsz, nblk),
        in_specs=[
            pl.BlockSpec((qb, na_w), lambda b, k: (b * nblk + k, col0)),
            pl.BlockSpec((qb, na_w), kblock(0)),
            pl.BlockSpec((qb, na_w), kblock(1)),
            pl.BlockSpec((qb, na_w), kblock(2)),
            pl.BlockSpec((qb, na_w), vblock(0)),
            pl.BlockSpec((qb, na_w), vblock(1)),
            pl.BlockSpec((qb, na_w), vblock(2)),
            pl.BlockSpec((ctx_len, na_w), lambda b, k: (b, col0 + 1)),
            pl.BlockSpec((ctx_len, na_w), lambda b, k: (b, col0 + 2)),
            pl.BlockSpec((None, heads, qb, K_ROWS * GRID_W), btype),
        ],
        out_specs=pl.BlockSpec((qb, na_w), lambda b, k: (b * nblk + k, 0)),
        compiler_params=_params("parallel", "arbitrary"),
        name="na_lat",
    )(p_lat, p_lat, p_lat, p_lat, p_lat, p_lat, p_lat, p_ctx, p_ctx, bias)


def _attn_ctx_kernel(q_ref, k_ref, v_ref, o_ref, *, heads):
    scale = HEAD_DIM ** -0.5
    for h in range(heads):
        cs = slice(h * HEAD_DIM, (h + 1) * HEAD_DIM)
        q = (q_ref[:, cs].astype(F32) * scale).astype(BF16)
        s = lax.dot_general(q, k_ref[:, cs], _NT, preferred_element_type=F32)
        o_ref[:, cs] = _softmax_pv([s], [v_ref[:, cs]], q_ref.shape[0]).astype(o_ref.dtype)


def _na_ctx(p_ctx, bsz, ctx_len, na_w, col0):
    heads = na_w // HEAD_DIM
    return pl.pallas_call(
        functools.partial(_attn_ctx_kernel, heads=heads),
        out_shape=jax.ShapeDtypeStruct((bsz * ctx_len, na_w), BF16),
        grid=(bsz,),
        in_specs=[pl.BlockSpec((ctx_len, na_w), lambda b, c=c: (b, col0 + c)) for c in range(3)],
        out_specs=pl.BlockSpec((ctx_len, na_w), lambda b: (b, 0)),
        compiler_params=_params("parallel"),
        name="na_ctx",
    )(p_ctx, p_ctx, p_ctx)


def _swap16(x):
    lane = lax.broadcasted_iota(jnp.int32, x.shape, 1)
    up = pltpu.roll(x, HEAD_DIM - 16, 1)
    down = pltpu.roll(x, 16, 1)
    return jnp.where((lane & 16) == 0, up, down)


def _diff_kernel(*refs, n_lat, rope, lam_init):
    if rope:
        (q_ref, k_ref, v_ref, kc_ref, vc_ref, cosq_ref, sinq_ref, cos_ref, sin_ref, lam_ref, g_ref,
         o_ref, k_scr, vt_scr, kmax_scr) = refs
    else:
        q_ref, kc_ref, vc_ref, lam_ref, g_ref, o_ref, k_scr, vt_scr, kmax_scr = refs
    n_ctx = kc_ref.shape[0]
    nk = n_lat + n_ctx
    hd = HEAD_DIM
    half = hd // 2

    @pl.when(pl.program_id(2) == 0)
    def _():
        step = 512
        row = lax.broadcasted_iota(jnp.int32, (hd, hd), 0)
        col = lax.broadcasted_iota(jnp.int32, (hd, hd), 1)
        sel = jnp.where(((col == 0) & (row < half)) | ((col == 1) & (row >= half)), 1.0, 0.0).astype(BF16)

        def put_keys(rs, kr, kmax):
            n = kr.shape[0]
            k_scr[rs, :hd] = kr
            lane = lax.broadcasted_iota(jnp.int32, (n, hd), 1)
            k_scr[rs, hd:] = jnp.where(lane == 0, 1.0, 0.0).astype(BF16)
            kf = kr.astype(F32)
            sq = jnp.dot((kf * kf).astype(BF16), sel, preferred_element_type=F32).max(axis=0, keepdims=True)
            return sq if kmax is None else jnp.maximum(kmax, sq)

        kmax = None
        for c in range(n_lat // step):
            rs = slice(c * step, (c + 1) * step)
            k = k_ref[rs, :].astype(F32)
            kmax = put_keys(rs, (k * cos_ref[rs, :] + _swap16(k) * sin_ref[rs, :]).astype(BF16), kmax)
            vt_scr[:hd, rs] = v_ref[rs, :].astype(F32).T.astype(BF16)
        kmax = put_keys(slice(n_lat, nk), kc_ref[...], kmax)
        vt_scr[:hd, n_lat:nk] = vc_ref[...].astype(F32).T.astype(BF16)
        vt_scr[hd:, :] = jnp.ones((SUM_ROWS, nk), BF16)
        lane1 = lax.broadcasted_iota(jnp.int32, kmax.shape, 1)
        for mp in range(2):
            kmax_scr[mp:mp + 1, :] = jnp.broadcast_to(
                jnp.max(jnp.where(lane1 == mp, kmax, 0.0), axis=-1, keepdims=True), kmax.shape)

    qraw = q_ref[...]
    q = qraw.astype(F32)
    tq = q.shape[0]
    row = lax.broadcasted_iota(jnp.int32, (hd, hd), 0)
    col = lax.broadcasted_iota(jnp.int32, (hd, hd), 1)
    sel1 = jnp.where((col == 0) & (row < half), 1.0, 0.0).astype(BF16)
    sel2 = jnp.where((col == 0) & (row >= half), 1.0, 0.0).astype(BF16)
    sq = (q * q).astype(BF16)
    n1 = jnp.dot(sq, sel1, preferred_element_type=F32)
    n2 = jnp.dot(sq, sel2, preferred_element_type=F32)
    if rope:
        perm = jnp.where(row == (col ^ 16), 1.0, 0.0).astype(BF16)
        q = q * cosq_ref[...] + jnp.dot(qraw, perm, preferred_element_type=F32) * sinq_ref[...]
    else:
        q = q * _Q_SCALE
    lane = lax.broadcasted_iota(jnp.int32, q.shape, 1)
    shift1 = jnp.sqrt(n1 * kmax_scr[0:1, :]) * (-_Q_SCALE * DIFF_BOUND_MARGIN)
    shift2 = jnp.sqrt(n2 * kmax_scr[1:2, :]) * (-_Q_SCALE * DIFF_BOUND_MARGIN)
    q_aug = jnp.concatenate([
        jnp.concatenate([jnp.where(lane < half, q, 0.0), shift1], axis=1),
        jnp.concatenate([jnp.where(lane >= half, q, 0.0), shift2], axis=1),
    ], axis=0).astype(BF16)

    lp = lam_ref[...]
    d1 = jnp.sum(lp[0:1, :] * lp[1:2, :], axis=-1, keepdims=True)
    d2 = jnp.sum(lp[2:3, :] * lp[3:4, :], axis=-1, keepdims=True)
    lam = jnp.exp(d1) - jnp.exp(d2) + lam_init

    chunks = [(k0, min(DIFF_TK, nk - k0)) for k0 in range(0, nk, DIFF_TK)]

    def finish(acc):
        l = acc[hd:hd + 1, :]
        c1 = 1.0 / l[:, :tq]
        c2 = lam / l[:, tq:]
        ot = acc[:hd, :tq] * c1 - acc[:hd, tq:] * c2
        ot = ot * lax.rsqrt(jnp.mean(ot * ot, axis=0, keepdims=True) + RMS_EPS)
        o_ref[...] = ((ot.T * g_ref[...]) * (1.0 - lam_init)).astype(o_ref.dtype)

    def pipelined(scores, consume):
        pending = [scores(c) for c in range(min(DIFF_AHEAD, len(chunks)))]
        for c in range(len(chunks)):
            s = pending.pop(0)
            if c + DIFF_AHEAD < len(chunks):
                pending.append(scores(c + DIFF_AHEAD))
            consume(c, s)

    state = {}

    def fast_scores(c):
        k0, tk = chunks[c]
        return lax.dot_general(k_scr[k0:k0 + tk, :], q_aug, _NT, preferred_element_type=F32)

    def fast_consume(c, s):
        k0, tk = chunks[c]
        pv = jnp.dot(vt_scr[:, k0:k0 + tk], jnp.exp2(s).astype(BF16), preferred_element_type=F32)
        state["acc"] = pv if c == 0 else state["acc"] + pv

    pipelined(fast_scores, fast_consume)
    acc = state["acc"]
    finish(acc)

    @pl.when(jnp.logical_not(jnp.min(acc[hd:hd + 1, :]) > DIFF_L_MIN))
    def _():
        qab = q_aug[:, :hd]
        st = {}

        def safe_scores(c):
            k0, tk = chunks[c]
            return lax.dot_general(k_scr[k0:k0 + tk, :hd], qab, _NT, preferred_element_type=F32)

        def safe_consume(c, s):
            k0, tk = chunks[c]
            mc = s.max(axis=0, keepdims=True)
            m_new = mc if c == 0 else jnp.maximum(st["m"], mc)
            pv = jnp.dot(vt_scr[:, k0:k0 + tk], jnp.exp2(s - m_new).astype(BF16), preferred_element_type=F32)
            st["acc"] = pv if c == 0 else st["acc"] * jnp.exp2(st["m"] - m_new) + pv
            st["m"] = m_new

        pipelined(safe_scores, safe_consume)
        finish(st["acc"])


def _diff_lat(p_lat, p_ctx, tabs, lam_p, sub_g, bsz, seq, ctx_len, heads, lam_init):
    tq = DIFF_TQ
    nq = seq // tq
    hd = HEAD_DIM
    nk = seq + ctx_len
    cosq, sinq, cos, sin = tabs
    return pl.pallas_call(
        functools.partial(_diff_kernel, n_lat=seq, rope=True, lam_init=lam_init),
        out_shape=jax.ShapeDtypeStruct((bsz * seq, heads * hd), BF16),
        grid=(bsz, heads, nq),
        in_specs=[
            pl.BlockSpec((tq, hd), lambda b, h, i: (b * nq + i, h)),
            pl.BlockSpec((seq, hd), lambda b, h, i: (b, heads + h)),
            pl.BlockSpec((seq, hd), lambda b, h, i: (b, 2 * heads + h)),
            pl.BlockSpec((ctx_len, hd), lambda b, h, i: (b, heads + h)),
            pl.BlockSpec((ctx_len, hd), lambda b, h, i: (b, 2 * heads + h)),
            pl.BlockSpec((tq, hd), lambda b, h, i: (i, 0)),
            pl.BlockSpec((tq, hd), lambda b, h, i: (i, 0)),
            pl.BlockSpec((seq, hd), lambda b, h, i: (0, 0)),
            pl.BlockSpec((seq, hd), lambda b, h, i: (0, 0)),
            pl.BlockSpec((4, hd // 2), lambda b, h, i: (0, 0)),
            pl.BlockSpec((1, hd), lambda b, h, i: (0, 0)),
        ],
        out_specs=pl.BlockSpec((tq, hd), lambda b, h, i: (b * nq + i, h)),
        scratch_shapes=[pltpu.VMEM((nk, 2 * hd), BF16), pltpu.VMEM((hd + SUM_ROWS, nk), BF16), pltpu.VMEM((8, hd), F32)],
        compiler_params=_params("parallel", "parallel", "arbitrary"),
        name="diff_lat",
    )(p_lat, p_lat, p_lat, p_ctx, p_ctx, cosq, sinq, cos, sin, lam_p, sub_g.reshape(1, hd))


def _diff_ctx(p_ctx, lam_p, sub_g, bsz, ctx_len, heads, lam_init):
    hd = HEAD_DIM
    return pl.pallas_call(
        functools.partial(_diff_kernel, n_lat=0, rope=False, lam_init=lam_init),
        out_shape=jax.ShapeDtypeStruct((bsz * ctx_len, heads * hd), BF16),
        grid=(bsz, heads, 1),
        in_specs=[
            pl.BlockSpec((ctx_len, hd), lambda b, h, i: (b, h)),
            pl.BlockSpec((ctx_len, hd), lambda b, h, i: (b, heads + h)),
            pl.BlockSpec((ctx_len, hd), lambda b, h, i: (b, 2 * heads + h)),
            pl.BlockSpec((4, hd // 2), lambda b, h, i: (0, 0)),
            pl.BlockSpec((1, hd), lambda b, h, i: (0, 0)),
        ],
        out_specs=pl.BlockSpec((ctx_len, hd), lambda b, h, i: (b, h)),
        scratch_shapes=[pltpu.VMEM((ctx_len, 2 * hd), BF16), pltpu.VMEM((hd + SUM_ROWS, ctx_len), BF16),
                        pltpu.VMEM((8, hd), F32)],
        compiler_params=_params("parallel", "parallel", "arbitrary"),
        name="diff_ctx",
    )(p_ctx, p_ctx, p_ctx, lam_p, sub_g.reshape(1, hd))


def _rope_tables(seq):
    axis = HEAD_DIM // 4
    t = jnp.arange(seq)
    row = (t // GRID_W).astype(F32)
    col = (t % GRID_W).astype(F32)
    inv = ROPE_THETA ** (-jnp.arange(0, axis, 2, dtype=F32) / axis)
    ang_r = row[:, None] * inv
    ang_c = col[:, None] * inv
    cr, sr, cc, sc = jnp.cos(ang_r), jnp.sin(ang_r), jnp.cos(ang_c), jnp.sin(ang_c)
    cos = jnp.concatenate([cr, cr, cc, cc] * 2, axis=-1)
    sin = jnp.concatenate([-sr, sr, -sc, sc] * 2, axis=-1)
    return cos * _Q_SCALE, sin * _Q_SCALE, cos, sin


def _fnet_kernel(f_ref, cn_ref, sn_ref, cc_ref, sc_ref, o_ref, a_scr, b_scr, *, norm):
    @pl.when(pl.program_id(1) == 0)
    def _():
        step = 512 if f_ref.shape[0] % 512 == 0 else f_ref.shape[0]
        for c in range(f_ref.shape[0] // step):
            rs = slice(c * step, (c + 1) * step)
            f = f_ref[rs, :]
            a_scr[rs, :] = jnp.dot(f, cc_ref[...], preferred_element_type=F32).astype(BF16)
            b_scr[rs, :] = jnp.dot(f, sc_ref[...], preferred_element_type=F32).astype(BF16)

    y = jnp.dot(cn_ref[...], a_scr[...], preferred_element_type=F32)
    y = y - jnp.dot(sn_ref[...], b_scr[...], preferred_element_type=F32)
    o_ref[...] = (y * norm).astype(o_ref.dtype)


def _fnet(p, cn, sn, ccb, scb, bsz, n, fw, col):
    tr = _tile(n, 512)
    norm = 1.0 / math.sqrt(n * HEAD_DIM)
    return pl.pallas_call(
        functools.partial(_fnet_kernel, norm=norm),
        out_shape=jax.ShapeDtypeStruct((bsz * n, fw), BF16),
        grid=(bsz, n // tr),
        in_specs=[
            pl.BlockSpec((n, fw), lambda b, i: (b, col)),
            pl.BlockSpec((tr, n), lambda b, i: (i, 0)),
            pl.BlockSpec((tr, n), lambda b, i: (i, 0)),
            pl.BlockSpec((fw, fw), lambda b, i: (0, 0)),
            pl.BlockSpec((fw, fw), lambda b, i: (0, 0)),
        ],
        out_specs=pl.BlockSpec((tr, fw), lambda b, i: (b * (n // tr) + i, 0)),
        scratch_shapes=[pltpu.VMEM((n, fw), BF16), pltpu.VMEM((n, fw), BF16)],
        compiler_params=_params("parallel", "arbitrary"),
        name="fnet",
    )(p, cn, sn, ccb, scb)


def _dft_tables(n):
    if n <= 256:
        idx = np.outer(np.arange(n), np.arange(n)) % n
        ang = 2.0 * np.pi * idx / n
        return jnp.asarray(np.cos(ang), F32).astype(BF16), jnp.asarray(np.sin(ang), F32).astype(BF16)
    r = int(round(math.sqrt(n)))
    assert r * r == n
    k = np.arange(n)
    t = np.arange(r)
    ang_a = 2.0 * np.pi * (np.outer(t, k) % r) / r
    ang_b = 2.0 * np.pi * (np.outer(t, k) % n) / n
    ca, sa = jnp.asarray(np.cos(ang_a), F32)[:, None, :], jnp.asarray(np.sin(ang_a), F32)[:, None, :]
    cb, sb = jnp.asarray(np.cos(ang_b), F32)[None, :, :], jnp.asarray(np.sin(ang_b), F32)[None, :, :]
    cos = (ca * cb - sa * sb).reshape(n, n).astype(BF16)
    sin = (sa * cb + ca * sb).reshape(n, n).astype(BF16)
    return cos, sin


def _channel_dft_tables(fw):
    idx = np.outer(np.arange(HEAD_DIM), np.arange(HEAD_DIM)) % HEAD_DIM
    ang = 2.0 * np.pi * idx / HEAD_DIM
    eye = np.eye(fw // HEAD_DIM)
    return (jnp.asarray(np.kron(eye, np.cos(ang)), F32).astype(BF16),
            jnp.asarray(np.kron(eye, np.sin(ang)), F32).astype(BF16))


def kernel(x, c, ctx, c_ctx, ada_w, ada_b, norm1_g, norm2_g, w_in, w_out, sgu_norm_g, sgu_w, sgu_b, na_rpb,
           diff_lq1, diff_lk1, diff_lq2, diff_lk2, diff_subln_g, mlp_w1, mlp_w2, final_g):
    bsz, seq, d = x.shape
    ctx_len = ctx.shape[1]
    depth = ada_w.shape[0]
    rows = seq // GRID_W
    mix_heads = d // HEAD_DIM
    sgu_w_dim = (mix_heads // 2) * HEAD_DIM
    na_w = d - sgu_w_dim
    diff_heads = (mix_heads * 3) // 4
    fnet_w = d - diff_heads * HEAD_DIM
    assert sgu_w_dim == na_w and bsz + 1 <= MOD_ROWS

    cond = jnp.zeros((MOD_ROWS, d), F32).at[:bsz].set(c).at[bsz].set(c_ctx)
    mods = _ada_table(cond, ada_w, ada_b).reshape(depth, MOD_ROWS, 1, ADA_CHUNKS * d)

    w_in_b, w_out_b = w_in.astype(BF16), w_out.astype(BF16)
    w1_b, w2_b = mlp_w1.astype(BF16), mlp_w2.astype(BF16)
    sgu_w_b = sgu_w.astype(BF16)
    n1g = norm1_g.reshape(depth, 1, d)
    n2g = norm2_g.reshape(depth, 1, d)
    sgu_g = sgu_norm_g.reshape(-1, 1, sgu_w_dim)
    sgu_b_full = jnp.broadcast_to(sgu_b[..., None], sgu_b.shape + (HEAD_DIM,)).astype(F32)

    tm_lat = _tile(seq, 512)
    tm_ctx = _tile(bsz * ctx_len, 512)
    lat_row = lambda i: (i * tm_lat) // seq
    ctx_row = lambda i: bsz

    x_lat = x.reshape(bsz * seq, d)
    x_ctx = ctx.reshape(bsz * ctx_len, d)

    if depth > 1:
        rope_tabs = _rope_tables(seq)
        cn, sn = _dft_tables(seq)
        cn_c, sn_c = _dft_tables(ctx_len)
        ccb, scb = _channel_dft_tables(fnet_w)

    for l in range(depth):
        need_ctx = l < depth - 1
        p_lat = _in_proj(x_lat, mods, n1g, w_in_b, l, lat_row, tm_lat)
        p_ctx = _in_proj(x_ctx, mods, n1g, w_in_b, l, ctx_row, tm_ctx)
        i = l // 2
        ma_ctx = mb_ctx = None
        if l % 2 == 0:
            bias = _na_bias(na_rpb[i], rows)
            ma_lat = _sgu(p_lat, sgu_g, sgu_w_b, sgu_b_full, i, sgu_w_dim)
            mb_lat = _na_lat(p_lat, p_ctx, bias, bsz, seq, ctx_len, na_w, 2)
            if need_ctx:
                ma_ctx = _sgu(p_ctx, sgu_g, sgu_w_b, sgu_b_full, i, sgu_w_dim)
                mb_ctx = _na_ctx(p_ctx, bsz, ctx_len, na_w, 2)
        else:
            lam_init = 0.8 - 0.6 * math.exp(-0.3 * l)
            lam_p = jnp.stack([diff_lq1[i], diff_lk1[i], diff_lq2[i], diff_lk2[i]]).astype(F32)
            fcol = (3 * diff_heads * HEAD_DIM) // fnet_w
            ma_lat = _diff_lat(p_lat, p_ctx, rope_tabs, lam_p, diff_subln_g[i], bsz, seq, ctx_len, diff_heads, lam_init)
            mb_lat = _fnet(p_lat, cn, sn, ccb, scb, bsz, seq, fnet_w, fcol)
            if need_ctx:
                ma_ctx = _diff_ctx(p_ctx, lam_p, diff_subln_g[i], bsz, ctx_len, diff_heads, lam_init)
                mb_ctx = _fnet(p_ctx, cn_c, sn_c, ccb, scb, bsz, ctx_len, fnet_w, fcol)
        x_lat = _out_proj(x_lat, ma_lat, mb_lat, w_out_b, mods, l, lat_row, tm_lat)
        x_lat = _mlp(x_lat, mods, n2g, w1_b, w2_b, final_g, l, lat_row, tm_lat, final_norm=not need_ctx)
        if need_ctx:
            x_ctx = _out_proj(x_ctx, ma_ctx, mb_ctx, w_out_b, mods, l, ctx_row, tm_ctx)
            x_ctx = _mlp(x_ctx, mods, n2g, w1_b, w2_b, final_g, l, ctx_row, tm_ctx, final_norm=False)
    return x_lat.reshape(bsz, seq, d)
```

```python
import functools
import math

import numpy as np
import jax
import jax.numpy as jnp
from jax import lax
from jax.experimental import pallas as pl
from jax.experimental.pallas import tpu as pltpu

F32 = jnp.float32
BF16 = jnp.bfloat16

GRID_W = 64
HEAD_DIM = 128
CHUNK = 128
NA_KH = 8
NA_KW = 16
ROPE_THETA = 10000.0
Q_ROWS = 4
K_ROWS = Q_ROWS + NA_KH
RMS_EPS = 1e-6
LN_EPS = 1e-5
ADA_CHUNKS = 6
MOD_ROWS = 16

VMEM_LIMIT = 56 * 1024 * 1024

_NT = (((1,), (1,)), ((), ()))

_Q_SCALE = (HEAD_DIM // 2) ** -0.5 * math.log2(math.e)
DIFF_TQ = 512
DIFF_TK = 1024
DIFF_AHEAD = 1
DIFF_BOUND_MARGIN = 1.01
DIFF_L_MIN = 2.0 ** -100
SUM_ROWS = 16


def _params(*sem):
    return pltpu.CompilerParams(dimension_semantics=sem, vmem_limit_bytes=VMEM_LIMIT)


def _tile(n, pref):
    if n <= pref:
        return n
    t = (pref // 128) * 128
    while n % t:
        t -= 128
    assert t > 0, (n, pref)
    return t


def _ada_kernel(s_ref, w_ref, b_ref, o_ref):
    s = s_ref[...]
    a = (s * jax.nn.sigmoid(s)).astype(BF16)
    o_ref[...] = jnp.dot(a, w_ref[...].astype(BF16), preferred_element_type=F32) + b_ref[...]


def _ada_table(cond, ada_w, ada_b):
    depth, d, n = ada_w.shape
    tn = _tile(n, 1024)
    return pl.pallas_call(
        _ada_kernel,
        out_shape=jax.ShapeDtypeStruct((depth, MOD_ROWS, n), F32),
        grid=(depth, n // tn),
        in_specs=[
            pl.BlockSpec((MOD_ROWS, d), lambda l, j: (0, 0)),
            pl.BlockSpec((None, d, tn), lambda l, j: (l, 0, j)),
            pl.BlockSpec((None, 1, tn), lambda l, j: (l, 0, j)),
        ],
        out_specs=pl.BlockSpec((None, MOD_ROWS, tn), lambda l, j: (l, 0, j)),
        compiler_params=_params("parallel", "parallel"),
        name="ada_table",
    )(cond, ada_w, ada_b.reshape(depth, 1, n))


def _norm_mod_rows(x_ref, g_ref, sh_ref, sc_ref, h_ref, unrolled=False, part=None):
    rows = 64
    g = g_ref[...]
    sc1 = 1.0 + sc_ref[...]
    sh = sh_ref[...]
    total = x_ref.shape[0] // rows
    if part is not None:
        assert unrolled and total % part[1] == 0
        total //= part[1]

    def body(r, carry):
        if part is not None:
            sl = pl.ds(pl.multiple_of((part[0] * total + r) * rows, rows), rows)
        elif unrolled:
            sl = slice(r * rows, (r + 1) * rows)
        else:
            sl = pl.ds(pl.multiple_of(r * rows, rows), rows)
        xv = x_ref[sl, :]
        ms = jnp.mean(xv * xv, axis=-1, keepdims=True)
        y = xv * lax.rsqrt(ms + RMS_EPS) * g
        h_ref[sl, :] = (y * sc1 + sh).astype(h_ref.dtype)
        return carry

    if unrolled:
        for r in range(total):
            body(r, 0)
    else:
        lax.fori_loop(0, total, body, 0)


def _mod_spec(d, layer, chunk, row_fn):
    return pl.BlockSpec((None, None, 1, d), lambda i, j: (layer, row_fn(i), 0, chunk))


def _layer_vec_spec(d, layer):
    return pl.BlockSpec((None, 1, d), lambda i, j: (layer, 0, 0))


def _in_kernel(x0_ref, xn_ref, g_ref, sh0_ref, sc0_ref, shn_ref, scn_ref, w_ref, o_ref, h0_scr, h1_scr):
    i = pl.program_id(0)

    @pl.when(i == 0)
    def _():
        _norm_mod_rows(x0_ref, g_ref, sh0_ref, sc0_ref, h0_scr)

    def step(h_cur, h_next):
        o_ref[...] = jnp.dot(h_cur[...], w_ref[...], preferred_element_type=F32).astype(o_ref.dtype)
        _norm_mod_rows(xn_ref, g_ref, shn_ref, scn_ref, h_next, unrolled=True)

    @pl.when(i % 2 == 0)
    def _():
        step(h0_scr, h1_scr)

    @pl.when(i % 2 == 1)
    def _():
        step(h1_scr, h0_scr)


def _in_proj(x, mods, g, w, layer, row_fn, tm):
    m, d = x.shape
    n = w.shape[2]
    nt = m // tm
    nxt = lambda i: jnp.minimum(i + 1, nt - 1)
    once = pl.Buffered(1)
    return pl.pallas_call(
        _in_kernel,
        out_shape=jax.ShapeDtypeStruct((m, n), BF16),
        grid=(nt, 1),
        in_specs=[
            pl.BlockSpec((tm, d), lambda i, j: (0, 0), pipeline_mode=once),
            pl.BlockSpec((tm, d), lambda i, j: (nxt(i), 0)),
            _layer_vec_spec(d, layer),
            _mod_spec(d, layer, 0, lambda i: row_fn(0)),
            _mod_spec(d, layer, 1, lambda i: row_fn(0)),
            _mod_spec(d, layer, 0, lambda i: row_fn(nxt(i))),
            _mod_spec(d, layer, 1, lambda i: row_fn(nxt(i))),
            pl.BlockSpec((None, d, n), lambda i, j: (layer, 0, 0), pipeline_mode=once),
        ],
        out_specs=pl.BlockSpec((tm, n), lambda i, j: (i, 0)),
        scratch_shapes=[pltpu.VMEM((tm, d), BF16), pltpu.VMEM((tm, d), BF16)],
        compiler_params=_params("arbitrary", "arbitrary"),
        name="in_proj",
    )(x, x, g, mods, mods, mods, mods, w)


def _out_kernel(x_ref, ma_ref, mb_ref, wa_ref, wb_ref, gate_ref, o_ref):
    y = jnp.dot(ma_ref[...], wa_ref[...], preferred_element_type=F32)
    y = y + jnp.dot(mb_ref[...], wb_ref[...], preferred_element_type=F32)
    o_ref[...] = x_ref[...] + gate_ref[...] * y


def _out_proj(x, ma, mb, w, mods, layer, row_fn, tm):
    m, d = x.shape
    ka, kb = ma.shape[1], mb.shape[1]
    assert ka % kb == 0 and w.shape[1:] == (ka + kb, d)
    return pl.pallas_call(
        _out_kernel,
        out_shape=jax.ShapeDtypeStruct((m, d), F32),
        grid=(m // tm, 1),
        in_specs=[
            pl.BlockSpec((tm, d), lambda i, j: (i, 0)),
            pl.BlockSpec((tm, ka), lambda i, j: (i, 0)),
            pl.BlockSpec((tm, kb), lambda i, j: (i, 0)),
            pl.BlockSpec((None, ka, d), lambda i, j: (layer, 0, 0)),
            pl.BlockSpec((None, kb, d), lambda i, j: (layer, ka // kb, 0)),
            _mod_spec(d, layer, 2, row_fn),
        ],
        out_specs=pl.BlockSpec((tm, d), lambda i, j: (i, 0)),
        compiler_params=_params("parallel", "arbitrary"),
        name="out_proj",
    )(x, ma, mb, w, w, mods)


def _mlp_kernel(x_ref, x0_ref, xn_ref, g_ref, sh0_ref, sc0_ref, shn_ref, scn_ref, gate_ref, w1_ref, w2_ref, fg_ref,
                o_ref, h_scr, *, final_norm, n_parts):
    i = pl.program_id(0)
    f = pl.program_id(1)
    nf = n_parts
    slot = i % 2

    @pl.when((i == 0) & (f == 0))
    def _():
        _norm_mod_rows(x0_ref, g_ref, sh0_ref, sc0_ref, h_scr.at[0])

    def step(first, last):
        a = jnp.dot(h_scr[slot], w1_ref[...], preferred_element_type=F32)
        a = jnp.square(jnp.maximum(a, 0.0)).astype(BF16)
        y = jnp.dot(a, w2_ref[...], preferred_element_type=F32)
        if not first:
            y = o_ref[...] + y
        if last:
            y = x_ref[...] + gate_ref[...] * y
            if final_norm:
                y = y * lax.rsqrt(jnp.mean(y * y, axis=-1, keepdims=True) + RMS_EPS) * fg_ref[...]
        o_ref[...] = y
        _norm_mod_rows(xn_ref, g_ref, shn_ref, scn_ref, h_scr.at[1 - slot], unrolled=True, part=(f, n_parts))

    pl.when(f == 0)(lambda: step(True, False))
    pl.when((f > 0) & (f < nf - 1))(lambda: step(False, False))
    pl.when(f == nf - 1)(lambda: step(False, True))


def _mlp(x, mods, g, w1, w2, final_g, layer, row_fn, tm, final_norm):
    m, d = x.shape
    dff = w1.shape[2]
    tf = _tile(dff, 1024)
    nt = m // tm
    assert dff // tf >= 2
    nxt = lambda i: jnp.minimum(i + 1, nt - 1)
    return pl.pallas_call(
        functools.partial(_mlp_kernel, final_norm=final_norm, n_parts=dff // tf),
        out_shape=jax.ShapeDtypeStruct((m, d), F32),
        grid=(nt, dff // tf),
        in_specs=[
            pl.BlockSpec((tm, d), lambda i, j: (i, 0)),
            pl.BlockSpec((tm, d), lambda i, j: (0, 0), pipeline_mode=pl.Buffered(1)),
            pl.BlockSpec((tm, d), lambda i, j: (nxt(i), 0)),
            _layer_vec_spec(d, layer),
            _mod_spec(d, layer, 3, lambda i: row_fn(0)),
            _mod_spec(d, layer, 4, lambda i: row_fn(0)),
            _mod_spec(d, layer, 3, lambda i: row_fn(nxt(i))),
            _mod_spec(d, layer, 4, lambda i: row_fn(nxt(i))),
            _mod_spec(d, layer, 5, row_fn),
            pl.BlockSpec((None, d, tf), lambda i, j: (layer, 0, j)),
            pl.BlockSpec((None, tf, d), lambda i, j: (layer, j, 0)),
            pl.BlockSpec((1, d), lambda i, j: (0, 0)),
        ],
        out_specs=pl.BlockSpec((tm, d), lambda i, j: (i, 0)),
        scratch_shapes=[pltpu.VMEM((2, tm, d), BF16)],
        compiler_params=_params("arbitrary", "arbitrary"),
        name="mlp",
    )(x, x, x, g, mods, mods, mods, mods, mods, w1, w2, final_g.reshape(1, d))


def _sgu_kernel(u_ref, v_ref, gam_ref, ws_ref, b_ref, o_ref, *, groups):
    for c in range(u_ref.shape[0] // CHUNK):
        rs = slice(c * CHUNK, (c + 1) * CHUNK)
        for g in range(groups):
            cs = slice(g * HEAD_DIM, (g + 1) * HEAD_DIM)
            v = v_ref[rs, cs].astype(F32)
            mu = jnp.mean(v, axis=-1, keepdims=True)
            dv = v - mu
            var = jnp.mean(dv * dv, axis=-1, keepdims=True)
            vn = (dv * lax.rsqrt(var + LN_EPS) * gam_ref[:, cs]).astype(BF16)
            mixed = jnp.dot(ws_ref[g], vn, preferred_element_type=F32) + b_ref[g]
            o_ref[rs, cs] = (u_ref[rs, cs].astype(F32) * mixed).astype(o_ref.dtype)


def _sgu(p, gam, ws, b_full, layer, sgu_w):
    m = p.shape[0]
    groups = sgu_w // HEAD_DIM
    tr = _tile(m, 512)
    return pl.pallas_call(
        functools.partial(_sgu_kernel, groups=groups),
        out_shape=jax.ShapeDtypeStruct((m, sgu_w), BF16),
        grid=(m // tr,),
        in_specs=[
            pl.BlockSpec((tr, sgu_w), lambda i: (i, 0)),
            pl.BlockSpec((tr, sgu_w), lambda i: (i, 1)),
            pl.BlockSpec((None, 1, sgu_w), lambda i: (layer, 0, 0)),
            pl.BlockSpec((None, groups, CHUNK, CHUNK), lambda i: (layer, 0, 0, 0)),
            pl.BlockSpec((None, groups, CHUNK, HEAD_DIM), lambda i: (layer, 0, 0, 0)),
        ],
        out_specs=pl.BlockSpec((tr, sgu_w), lambda i: (i, 0)),
        compiler_params=_params("parallel"),
        name="sgu",
    )(p, p, gam, ws, b_full)


def _softmax_pv(scores, values):
    d = values[0].shape[1]
    m = scores[0].max(axis=-1, keepdims=True)
    for s in scores[1:]:
        m = jnp.maximum(m, s.max(axis=-1, keepdims=True))
    o = None
    for s, v in zip(scores, values):
        v1 = jnp.concatenate([v, jnp.ones(v.shape, v.dtype)], axis=1)
        pv = jnp.dot(jnp.exp2(s - m).astype(BF16), v1, preferred_element_type=F32)
        o = pv if o is None else o + pv
    return o[:, :d] / o[:, d:]


_NA_SCALE = HEAD_DIM ** -0.5 * math.log2(math.e)


def _na_kernel(q_ref, k0_ref, k1_ref, k2_ref, v0_ref, v1_ref, v2_ref, kc_ref, vc_ref, bias_ref, o_ref, *, heads):
    qb = q_ref.shape[0]
    scale = _NA_SCALE
    for h in range(heads):
        cs = slice(h * HEAD_DIM, (h + 1) * HEAD_DIM)
        q = (q_ref[:, cs].astype(F32) * scale).astype(BF16)
        scores, values = [], []
        for j, (k_ref, v_ref) in enumerate(((k0_ref, v0_ref), (k1_ref, v1_ref), (k2_ref, v2_ref))):
            s = lax.dot_general(q, k_ref[:, cs], _NT, preferred_element_type=F32)
            scores.append(s + bias_ref[h, :, j * qb:(j + 1) * qb])
            values.append(v_ref[:, cs])
        scores.append(lax.dot_general(q, kc_ref[:, cs], _NT, preferred_element_type=F32))
        values.append(vc_ref[:, cs])
        o_ref[:, cs] = _softmax_pv(scores, values).astype(o_ref.dtype)


def _na_bias(rpb, rows):
    heads = rpb.shape[0]
    kw = NA_KW
    col = jnp.arange(GRID_W)
    col_start = jnp.clip(col - kw // 2, 0, GRID_W - kw)
    col_mask = (col[None, :] >= col_start[:, None]) & (col[None, :] < col_start[:, None] + kw)
    dx = jnp.clip(col[None, :] - col[:, None], -(kw - 1), kw - 1) + (kw - 1)
    rpb_x = jnp.where(col_mask[None, None], rpb.astype(F32)[:, :, dx], -jnp.inf)
    nblk = rows // Q_ROWS
    dy_idx = np.zeros((3, Q_ROWS, K_ROWS), np.int32)
    valid = np.zeros((3, Q_ROWS, K_ROWS), bool)
    for t, kblk in enumerate((0, 1, nblk - 1)):
        ks = min(max(Q_ROWS * kblk - NA_KH // 2, 0), rows - K_ROWS)
        for i in range(Q_ROWS):
            r = Q_ROWS * kblk + i
            rs = min(max(r - NA_KH // 2, 0), rows - NA_KH)
            for j in range(K_ROWS):
                kr = ks + j
                valid[t, i, j] = rs <= kr < rs + NA_KH
                dy_idx[t, i, j] = min(max(kr - r + NA_KH - 1, 0), 2 * NA_KH - 2)
    b = rpb_x[:, dy_idx]
    b = jnp.where(jnp.asarray(valid)[None, :, :, :, None, None], b, -jnp.inf)
    b = b.transpose(1, 0, 2, 4, 3, 5) * math.log2(math.e)
    return b.reshape(3, heads, Q_ROWS * GRID_W, K_ROWS * GRID_W)


def _na_lat(p_lat, p_ctx, bias, bsz, seq, ctx_len, na_w, col0):
    heads = na_w // HEAD_DIM
    qb = Q_ROWS * GRID_W
    nblk = seq // qb
    assert nblk >= 4 and K_ROWS == 3 * Q_ROWS

    def kblock(j):
        return lambda b, k: (b * nblk + jnp.clip(k - 1, 0, nblk - 3) + j, col0 + 1)

    def vblock(j):
        return lambda b, k: (b * nblk + jnp.clip(k - 1, 0, nblk - 3) + j, col0 + 2)

    def btype(b, k):
        return (jnp.where(k == 0, 0, jnp.where(k == nblk - 1, 2, 1)), 0, 0, 0)

    return pl.pallas_call(
        functools.partial(_na_kernel, heads=heads),
        out_shape=jax.ShapeDtypeStruct((bsz * seq, na_w), BF16),
        grid=(bsz, nblk),
        in_specs=[
            pl.BlockSpec((qb, na_w), lambda b, k: (b * nblk + k, col0)),
            pl.BlockSpec((qb, na_w), kblock(0)),
            pl.BlockSpec((qb, na_w), kblock(1)),
            pl.BlockSpec((qb, na_w), kblock(2)),
            pl.BlockSpec((qb, na_w), vblock(0)),
            pl.BlockSpec((qb, na_w), vblock(1)),
            pl.BlockSpec((qb, na_w), vblock(2)),
            pl.BlockSpec((ctx_len, na_w), lambda b, k: (b, col0 + 1)),
            pl.BlockSpec((ctx_len, na_w), lambda b, k: (b, col0 + 2)),
            pl.BlockSpec((None, heads, qb, K_ROWS * GRID_W), btype),
        ],
        out_specs=pl.BlockSpec((qb, na_w), lambda b, k: (b * nblk + k, 0)),
        compiler_params=_params("parallel", "arbitrary"),
        name="na_lat",
    )(p_lat, p_lat, p_lat, p_lat, p_lat, p_lat, p_lat, p_ctx, p_ctx, bias)


def _attn_ctx_kernel(q_ref, k_ref, v_ref, o_ref, *, heads):
    for h in range(heads):
        cs = slice(h * HEAD_DIM, (h + 1) * HEAD_DIM)
        q = (q_ref[:, cs].astype(F32) * _NA_SCALE).astype(BF16)
        s = lax.dot_general(q, k_ref[:, cs], _NT, preferred_element_type=F32)
        o_ref[:, cs] = _softmax_pv([s], [v_ref[:, cs]]).astype(o_ref.dtype)


def _na_ctx(p_ctx, bsz, ctx_len, na_w, col0):
    heads = na_w // HEAD_DIM
    return pl.pallas_call(
        functools.partial(_attn_ctx_kernel, heads=heads),
        out_shape=jax.ShapeDtypeStruct((bsz * ctx_len, na_w), BF16),
        grid=(bsz,),
        in_specs=[pl.BlockSpec((ctx_len, na_w), lambda b, c=c: (b, col0 + c)) for c in range(3)],
        out_specs=pl.BlockSpec((ctx_len, na_w), lambda b: (b, 0)),
        compiler_params=_params("parallel"),
        name="na_ctx",
    )(p_ctx, p_ctx, p_ctx)


def _swap16(x):
    lane = lax.broadcasted_iota(jnp.int32, x.shape, 1)
    up = pltpu.roll(x, HEAD_DIM - 16, 1)
    down = pltpu.roll(x, 16, 1)
    return jnp.where((lane & 16) == 0, up, down)


def _diff_kernel(*refs, n_lat, rope, lam_init):
    if rope:
        (q_ref, k_ref, v_ref, kc_ref, vc_ref, cosq_ref, sinq_ref, cos_ref, sin_ref, lam_ref, g_ref,
         o_ref, k_scr, vt_scr, kmax_scr) = refs
    else:
        q_ref, kc_ref, vc_ref, lam_ref, g_ref, o_ref, k_scr, vt_scr, kmax_scr = refs
    n_ctx = kc_ref.shape[0]
    nk = n_lat + n_ctx
    hd = HEAD_DIM
    half = hd // 2

    @pl.when(pl.program_id(2) == 0)
    def _():
        step = 512
        row = lax.broadcasted_iota(jnp.int32, (hd, hd), 0)
        col = lax.broadcasted_iota(jnp.int32, (hd, hd), 1)
        sel = jnp.where(((col == 0) & (row < half)) | ((col == 1) & (row >= half)), 1.0, 0.0).astype(BF16)

        def put_keys(rs, kr, kmax):
            n = kr.shape[0]
            k_scr[rs, :hd] = kr
            lane = lax.broadcasted_iota(jnp.int32, (n, hd), 1)
            k_scr[rs, hd:] = jnp.where(lane == 0, 1.0, 0.0).astype(BF16)
            kf = kr.astype(F32)
            sq = jnp.dot((kf * kf).astype(BF16), sel, preferred_element_type=F32).max(axis=0, keepdims=True)
            return sq if kmax is None else jnp.maximum(kmax, sq)

        kmax = None
        for c in range(n_lat // step):
            rs = slice(c * step, (c + 1) * step)
            k = k_ref[rs, :].astype(F32)
            kmax = put_keys(rs, (k * cos_ref[rs, :] + _swap16(k) * sin_ref[rs, :]).astype(BF16), kmax)
            vt_scr[:hd, rs] = v_ref[rs, :].astype(F32).T.astype(BF16)
        kmax = put_keys(slice(n_lat, nk), kc_ref[...], kmax)
        vt_scr[:hd, n_lat:nk] = vc_ref[...].astype(F32).T.astype(BF16)
        vt_scr[hd:, :] = jnp.ones((SUM_ROWS, nk), BF16)
        lane1 = lax.broadcasted_iota(jnp.int32, kmax.shape, 1)
        for mp in range(2):
            kmax_scr[mp:mp + 1, :] = jnp.broadcast_to(
                jnp.max(jnp.where(lane1 == mp, kmax, 0.0), axis=-1, keepdims=True), kmax.shape)

    qraw = q_ref[...]
    q = qraw.astype(F32)
    tq = q.shape[0]
    row = lax.broadcasted_iota(jnp.int32, (hd, hd), 0)
    col = lax.broadcasted_iota(jnp.int32, (hd, hd), 1)
    sel1 = jnp.where((col == 0) & (row < half), 1.0, 0.0).astype(BF16)
    sel2 = jnp.where((col == 0) & (row >= half), 1.0, 0.0).astype(BF16)
    sq = (q * q).astype(BF16)
    n1 = jnp.dot(sq, sel1, preferred_element_type=F32)
    n2 = jnp.dot(sq, sel2, preferred_element_type=F32)
    if rope:
        perm = jnp.where(row == (col ^ 16), 1.0, 0.0).astype(BF16)
        q = q * cosq_ref[...] + jnp.dot(qraw, perm, preferred_element_type=F32) * sinq_ref[...]
    else:
        q = q * _Q_SCALE
    lane = lax.broadcasted_iota(jnp.int32, q.shape, 1)
    shift1 = jnp.sqrt(n1 * kmax_scr[0:1, :]) * (-_Q_SCALE * DIFF_BOUND_MARGIN)
    shift2 = jnp.sqrt(n2 * kmax_scr[1:2, :]) * (-_Q_SCALE * DIFF_BOUND_MARGIN)
    q_aug = jnp.concatenate([
        jnp.concatenate([jnp.where(lane < half, q, 0.0), shift1], axis=1),
        jnp.concatenate([jnp.where(lane >= half, q, 0.0), shift2], axis=1),
    ], axis=0).astype(BF16)

    lp = lam_ref[...]
    d1 = jnp.sum(lp[0:1, :] * lp[1:2, :], axis=-1, keepdims=True)
    d2 = jnp.sum(lp[2:3, :] * lp[3:4, :], axis=-1, keepdims=True)
    lam = jnp.exp(d1) - jnp.exp(d2) + lam_init

    chunks = [(k0, min(DIFF_TK, nk - k0)) for k0 in range(0, nk, DIFF_TK)]

    def finish(acc):
        l = acc[hd:hd + 1, :]
        c1 = 1.0 / l[:, :tq]
        c2 = lam / l[:, tq:]
        ot = acc[:hd, :tq] * c1 - acc[:hd, tq:] * c2
        ot = ot * lax.rsqrt(jnp.mean(ot * ot, axis=0, keepdims=True) + RMS_EPS)
        o_ref[...] = ((ot.T * g_ref[...]) * (1.0 - lam_init)).astype(o_ref.dtype)

    def pipelined(scores, consume):
        pending = [scores(c) for c in range(min(DIFF_AHEAD, len(chunks)))]
        for c in range(len(chunks)):
            s = pending.pop(0)
            if c + DIFF_AHEAD < len(chunks):
                pending.append(scores(c + DIFF_AHEAD))
            consume(c, s)

    state = {}

    def fast_scores(c):
        k0, tk = chunks[c]
        return lax.dot_general(k_scr[k0:k0 + tk, :], q_aug, _NT, preferred_element_type=F32)

    def fast_consume(c, s):
        k0, tk = chunks[c]
        pv = jnp.dot(vt_scr[:, k0:k0 + tk], jnp.exp2(s).astype(BF16), preferred_element_type=F32)
        state["acc"] = pv if c == 0 else state["acc"] + pv

    pipelined(fast_scores, fast_consume)
    acc = state["acc"]
    finish(acc)

    @pl.when(jnp.logical_not(jnp.min(acc[hd:hd + 1, :]) > DIFF_L_MIN))
    def _():
        qab = q_aug[:, :hd]
        st = {}

        def safe_scores(c):
            k0, tk = chunks[c]
            return lax.dot_general(k_scr[k0:k0 + tk, :hd], qab, _NT, preferred_element_type=F32)

        def safe_consume(c, s):
            k0, tk = chunks[c]
            mc = s.max(axis=0, keepdims=True)
            m_new = mc if c == 0 else jnp.maximum(st["m"], mc)
            pv = jnp.dot(vt_scr[:, k0:k0 + tk], jnp.exp2(s - m_new).astype(BF16), preferred_element_type=F32)
            st["acc"] = pv if c == 0 else st["acc"] * jnp.exp2(st["m"] - m_new) + pv
            st["m"] = m_new

        pipelined(safe_scores, safe_consume)
        finish(st["acc"])


def _diff_lat(p_lat, p_ctx, tabs, lam_p, sub_g, bsz, seq, ctx_len, heads, lam_init):
    tq = DIFF_TQ
    nq = seq // tq
    hd = HEAD_DIM
    nk = seq + ctx_len
    cosq, sinq, cos, sin = tabs
    return pl.pallas_call(
        functools.partial(_diff_kernel, n_lat=seq, rope=True, lam_init=lam_init),
        out_shape=jax.ShapeDtypeStruct((bsz * seq, heads * hd), BF16),
        grid=(bsz, heads, nq),
        in_specs=[
            pl.BlockSpec((tq, hd), lambda b, h, i: (b * nq + i, h)),
            pl.BlockSpec((seq, hd), lambda b, h, i: (b, heads + h)),
            pl.BlockSpec((seq, hd), lambda b, h, i: (b, 2 * heads + h)),
            pl.BlockSpec((ctx_len, hd), lambda b, h, i: (b, heads + h)),
            pl.BlockSpec((ctx_len, hd), lambda b, h, i: (b, 2 * heads + h)),
            pl.BlockSpec((tq, hd), lambda b, h, i: (i, 0)),
            pl.BlockSpec((tq, hd), lambda b, h, i: (i, 0)),
            pl.BlockSpec((seq, hd), lambda b, h, i: (0, 0)),
            pl.BlockSpec((seq, hd), lambda b, h, i: (0, 0)),
            pl.BlockSpec((4, hd // 2), lambda b, h, i: (0, 0)),
            pl.BlockSpec((1, hd), lambda b, h, i: (0, 0)),
        ],
        out_specs=pl.BlockSpec((tq, hd), lambda b, h, i: (b * nq + i, h)),
        scratch_shapes=[pltpu.VMEM((nk, 2 * hd), BF16), pltpu.VMEM((hd + SUM_ROWS, nk), BF16), pltpu.VMEM((8, hd), F32)],
        compiler_params=_params("parallel", "parallel", "arbitrary"),
        name="diff_lat",
    )(p_lat, p_lat, p_lat, p_ctx, p_ctx, cosq, sinq, cos, sin, lam_p, sub_g.reshape(1, hd))


def _diff_ctx(p_ctx, lam_p, sub_g, bsz, ctx_len, heads, lam_init):
    hd = HEAD_DIM
    return pl.pallas_call(
        functools.partial(_diff_kernel, n_lat=0, rope=False, lam_init=lam_init),
        out_shape=jax.ShapeDtypeStruct((bsz * ctx_len, heads * hd), BF16),
        grid=(bsz, heads, 1),
        in_specs=[
            pl.BlockSpec((ctx_len, hd), lambda b, h, i: (b, h)),
            pl.BlockSpec((ctx_len, hd), lambda b, h, i: (b, heads + h)),
            pl.BlockSpec((ctx_len, hd), lambda b, h, i: (b, 2 * heads + h)),
            pl.BlockSpec((4, hd // 2), lambda b, h, i: (0, 0)),
            pl.BlockSpec((1, hd), lambda b, h, i: (0, 0)),
        ],
        out_specs=pl.BlockSpec((ctx_len, hd), lambda b, h, i: (b, h)),
        scratch_shapes=[pltpu.VMEM((ctx_len, 2 * hd), BF16), pltpu.VMEM((hd + SUM_ROWS, ctx_len), BF16),
                        pltpu.VMEM((8, hd), F32)],
        compiler_params=_params("parallel", "parallel", "arbitrary"),
        name="diff_ctx",
    )(p_ctx, p_ctx, p_ctx, lam_p, sub_g.reshape(1, hd))


def _rope_tables(seq):
    axis = HEAD_DIM // 4
    t = jnp.arange(seq)
    row = (t // GRID_W).astype(F32)
    col = (t % GRID_W).astype(F32)
    inv = ROPE_THETA ** (-jnp.arange(0, axis, 2, dtype=F32) / axis)
    ang_r = row[:, None] * inv
    ang_c = col[:, None] * inv
    cr, sr, cc, sc = jnp.cos(ang_r), jnp.sin(ang_r), jnp.cos(ang_c), jnp.sin(ang_c)
    cos = jnp.concatenate([cr, cr, cc, cc] * 2, axis=-1)
    sin = jnp.concatenate([-sr, sr, -sc, sc] * 2, axis=-1)
    return cos * _Q_SCALE, sin * _Q_SCALE, cos, sin


def _fnet_kernel(f_ref, cn_ref, sn_ref, cc_ref, sc_ref, o_ref, a_scr, b_scr, *, norm):
    @pl.when(pl.program_id(1) == 0)
    def _():
        step = 512 if f_ref.shape[0] % 512 == 0 else f_ref.shape[0]
        for c in range(f_ref.shape[0] // step):
            rs = slice(c * step, (c + 1) * step)
            f = f_ref[rs, :]
            a_scr[rs, :] = jnp.dot(f, cc_ref[...], preferred_element_type=F32).astype(BF16)
            b_scr[rs, :] = jnp.dot(f, sc_ref[...], preferred_element_type=F32).astype(BF16)

    y = jnp.dot(cn_ref[...], a_scr[...], preferred_element_type=F32)
    y = y - jnp.dot(sn_ref[...], b_scr[...], preferred_element_type=F32)
    o_ref[...] = (y * norm).astype(o_ref.dtype)


def _fnet(p, cn, sn, ccb, scb, bsz, n, fw, col):
    tr = _tile(n, 512)
    norm = 1.0 / math.sqrt(n * HEAD_DIM)
    return pl.pallas_call(
        functools.partial(_fnet_kernel, norm=norm),
        out_shape=jax.ShapeDtypeStruct((bsz * n, fw), BF16),
        grid=(bsz, n // tr),
        in_specs=[
            pl.BlockSpec((n, fw), lambda b, i: (b, col)),
            pl.BlockSpec((tr, n), lambda b, i: (i, 0)),
            pl.BlockSpec((tr, n), lambda b, i: (i, 0)),
            pl.BlockSpec((fw, fw), lambda b, i: (0, 0)),
            pl.BlockSpec((fw, fw), lambda b, i: (0, 0)),
        ],
        out_specs=pl.BlockSpec((tr, fw), lambda b, i: (b * (n // tr) + i, 0)),
        scratch_shapes=[pltpu.VMEM((n, fw), BF16), pltpu.VMEM((n, fw), BF16)],
        compiler_params=_params("parallel", "arbitrary"),
        name="fnet",
    )(p, cn, sn, ccb, scb)


def _dft_tables(n):
    if n <= 256:
        idx = np.outer(np.arange(n), np.arange(n)) % n
        ang = 2.0 * np.pi * idx / n
        return jnp.asarray(np.cos(ang), F32).astype(BF16), jnp.asarray(np.sin(ang), F32).astype(BF16)
    r = int(round(math.sqrt(n)))
    assert r * r == n
    k = np.arange(n)
    t = np.arange(r)
    ang_a = 2.0 * np.pi * (np.outer(t, k) % r) / r
    ang_b = 2.0 * np.pi * (np.outer(t, k) % n) / n
    ca, sa = jnp.asarray(np.cos(ang_a), F32)[:, None, :], jnp.asarray(np.sin(ang_a), F32)[:, None, :]
    cb, sb = jnp.asarray(np.cos(ang_b), F32)[None, :, :], jnp.asarray(np.sin(ang_b), F32)[None, :, :]
    cos = (ca * cb - sa * sb).reshape(n, n).astype(BF16)
    sin = (sa * cb + ca * sb).reshape(n, n).astype(BF16)
    return cos, sin


def _channel_dft_tables(fw):
    idx = np.outer(np.arange(HEAD_DIM), np.arange(HEAD_DIM)) % HEAD_DIM
    ang = 2.0 * np.pi * idx / HEAD_DIM
    eye = np.eye(fw // HEAD_DIM)
    return (jnp.asarray(np.kron(eye, np.cos(ang)), F32).astype(BF16),
            jnp.asarray(np.kron(eye, np.sin(ang)), F32).astype(BF16))


def kernel(x, c, ctx, c_ctx, ada_w, ada_b, norm1_g, norm2_g, w_in, w_out, sgu_norm_g, sgu_w, sgu_b, na_rpb,
           diff_lq1, diff_lk1, diff_lq2, diff_lk2, diff_subln_g, mlp_w1, mlp_w2, final_g):
    bsz, seq, d = x.shape
    ctx_len = ctx.shape[1]
    depth = ada_w.shape[0]
    rows = seq // GRID_W
    mix_heads = d // HEAD_DIM
    sgu_w_dim = (mix_heads // 2) * HEAD_DIM
    na_w = d - sgu_w_dim
    diff_heads = (mix_heads * 3) // 4
    fnet_w = d - diff_heads * HEAD_DIM
    assert sgu_w_dim == na_w and bsz + 1 <= MOD_ROWS

    cond = jnp.zeros((MOD_ROWS, d), F32).at[:bsz].set(c).at[bsz].set(c_ctx)
    mods = _ada_table(cond, ada_w, ada_b).reshape(depth, MOD_ROWS, 1, ADA_CHUNKS * d)

    w_in_b, w_out_b = w_in.astype(BF16), w_out.astype(BF16)
    w1_b, w2_b = mlp_w1.astype(BF16), mlp_w2.astype(BF16)
    sgu_w_b = sgu_w.astype(BF16)
    n1g = norm1_g.reshape(depth, 1, d)
    n2g = norm2_g.reshape(depth, 1, d)
    sgu_g = sgu_norm_g.reshape(-1, 1, sgu_w_dim)
    sgu_b_full = jnp.broadcast_to(sgu_b[..., None], sgu_b.shape + (HEAD_DIM,)).astype(F32)

    tm_lat = _tile(seq, 512)
    tm_ctx = _tile(bsz * ctx_len, 512)
    lat_row = lambda i: (i * tm_lat) // seq
    ctx_row = lambda i: bsz

    x_lat = x.reshape(bsz * seq, d)
    x_ctx = ctx.reshape(bsz * ctx_len, d)

    if depth > 1:
        rope_tabs = _rope_tables(seq)
        cn, sn = _dft_tables(seq)
        cn_c, sn_c = _dft_tables(ctx_len)
        ccb, scb = _channel_dft_tables(fnet_w)

    for l in range(depth):
        need_ctx = l < depth - 1
        p_lat = _in_proj(x_lat, mods, n1g, w_in_b, l, lat_row, tm_lat)
        p_ctx = _in_proj(x_ctx, mods, n1g, w_in_b, l, ctx_row, tm_ctx)
        i = l // 2
        ma_ctx = mb_ctx = None
        if l % 2 == 0:
            bias = _na_bias(na_rpb[i], rows)
            ma_lat = _sgu(p_lat, sgu_g, sgu_w_b, sgu_b_full, i, sgu_w_dim)
            mb_lat = _na_lat(p_lat, p_ctx, bias, bsz, seq, ctx_len, na_w, 2)
            if need_ctx:
                ma_ctx = _sgu(p_ctx, sgu_g, sgu_w_b, sgu_b_full, i, sgu_w_dim)
                mb_ctx = _na_ctx(p_ctx, bsz, ctx_len, na_w, 2)
        else:
            lam_init = 0.8 - 0.6 * math.exp(-0.3 * l)
            lam_p = jnp.stack([diff_lq1[i], diff_lk1[i], diff_lq2[i], diff_lk2[i]]).astype(F32)
            fcol = (3 * diff_heads * HEAD_DIM) // fnet_w
            ma_lat = _diff_lat(p_lat, p_ctx, rope_tabs, lam_p, diff_subln_g[i], bsz, seq, ctx_len, diff_heads, lam_init)
            mb_lat = _fnet(p_lat, cn, sn, ccb, scb, bsz, seq, fnet_w, fcol)
            if need_ctx:
                ma_ctx = _diff_ctx(p_ctx, lam_p, diff_subln_g[i], bsz, ctx_len, diff_heads, lam_init)
                mb_ctx = _fnet(p_ctx, cn_c, sn_c, ccb, scb, bsz, ctx_len, fnet_w, fcol)
        x_lat = _out_proj(x_lat, ma_lat, mb_lat, w_out_b, mods, l, lat_row, tm_lat)
        x_lat = _mlp(x_lat, mods, n2g, w1_b, w2_b, final_g, l, lat_row, tm_lat, final_norm=not need_ctx)
        if need_ctx:
            x_ctx = _out_proj(x_ctx, ma_ctx, mb_ctx, w_out_b, mods, l, ctx_row, tm_ctx)
            x_ctx = _mlp(x_ctx, mods, n2g, w1_b, w2_b, final_g, l, ctx_row, tm_ctx, final_norm=False)
    return x_lat.reshape(bsz, seq, d)
```

```python
import functools
import math

import numpy as np
import jax
import jax.numpy as jnp
from jax import lax
from jax.experimental import pallas as pl
from jax.experimental.pallas import tpu as pltpu

F32 = jnp.float32
BF16 = jnp.bfloat16

GRID_W = 64
HEAD_DIM = 128
CHUNK = 128
NA_KH = 8
NA_KW = 16
ROPE_THETA = 10000.0
Q_ROWS = 4
K_ROWS = Q_ROWS + NA_KH
RMS_EPS = 1e-6
LN_EPS = 1e-5
ADA_CHUNKS = 6
MOD_ROWS = 16

VMEM_LIMIT = 56 * 1024 * 1024

_NT = (((1,), (1,)), ((), ()))

_Q_SCALE = (HEAD_DIM // 2) ** -0.5 * math.log2(math.e)
DIFF_TQ = 1024
DIFF_TK = 1024
DIFF_AHEAD = 1
DIFF_BOUND_MARGIN = 1.01
DIFF_L_MIN = 2.0 ** -100
SUM_ROWS = 16


def _params(*sem):
    return pltpu.CompilerParams(dimension_semantics=sem, vmem_limit_bytes=VMEM_LIMIT)


def _tile(n, pref):
    if n <= pref:
        return n
    t = (pref // 128) * 128
    while n % t:
        t -= 128
    assert t > 0, (n, pref)
    return t


def _ada_kernel(s_ref, w_ref, b_ref, o_ref):
    s = s_ref[...]
    a = (s * jax.nn.sigmoid(s)).astype(BF16)
    o_ref[...] = jnp.dot(a, w_ref[...].astype(BF16), preferred_element_type=F32) + b_ref[...]


def _ada_table(cond, ada_w, ada_b):
    depth, d, n = ada_w.shape
    tn = _tile(n, 1024)
    return pl.pallas_call(
        _ada_kernel,
        out_shape=jax.ShapeDtypeStruct((depth, MOD_ROWS, n), F32),
        grid=(depth, n // tn),
        in_specs=[
            pl.BlockSpec((MOD_ROWS, d), lambda l, j: (0, 0)),
            pl.BlockSpec((None, d, tn), lambda l, j: (l, 0, j)),
            pl.BlockSpec((None, 1, tn), lambda l, j: (l, 0, j)),
        ],
        out_specs=pl.BlockSpec((None, MOD_ROWS, tn), lambda l, j: (l, 0, j)),
        compiler_params=_params("parallel", "parallel"),
        name="ada_table",
    )(cond, ada_w, ada_b.reshape(depth, 1, n))


def _norm_mod_rows(x_ref, g_ref, sh_ref, sc_ref, h_ref, unrolled=False, part=None):
    rows = 64
    g = g_ref[...]
    sc1 = 1.0 + sc_ref[...]
    sh = sh_ref[...]
    total = x_ref.shape[0] // rows
    if part is not None:
        assert unrolled and total % part[1] == 0
        total //= part[1]

    def body(r, carry):
        if part is not None:
            sl = pl.ds(pl.multiple_of((part[0] * total + r) * rows, rows), rows)
        elif unrolled:
            sl = slice(r * rows, (r + 1) * rows)
        else:
            sl = pl.ds(pl.multiple_of(r * rows, rows), rows)
        xv = x_ref[sl, :]
        ms = jnp.mean(xv * xv, axis=-1, keepdims=True)
        y = xv * lax.rsqrt(ms + RMS_EPS) * g
        h_ref[sl, :] = (y * sc1 + sh).astype(h_ref.dtype)
        return carry

    if unrolled:
        for r in range(total):
            body(r, 0)
    else:
        lax.fori_loop(0, total, body, 0)


def _mod_spec(d, layer, chunk, row_fn):
    return pl.BlockSpec((None, None, 1, d), lambda i, j: (layer, row_fn(i), 0, chunk))


def _layer_vec_spec(d, layer):
    return pl.BlockSpec((None, 1, d), lambda i, j: (layer, 0, 0))


def _in_kernel(x0_ref, xn_ref, g_ref, sh0_ref, sc0_ref, shn_ref, scn_ref, w_ref, o_ref, h0_scr, h1_scr):
    i = pl.program_id(0)

    @pl.when(i == 0)
    def _():
        _norm_mod_rows(x0_ref, g_ref, sh0_ref, sc0_ref, h0_scr)

    def step(h_cur, h_next):
        o_ref[...] = jnp.dot(h_cur[...], w_ref[...], preferred_element_type=F32).astype(o_ref.dtype)
        _norm_mod_rows(xn_ref, g_ref, shn_ref, scn_ref, h_next, unrolled=True)

    @pl.when(i % 2 == 0)
    def _():
        step(h0_scr, h1_scr)

    @pl.when(i % 2 == 1)
    def _():
        step(h1_scr, h0_scr)


def _in_proj(x, mods, g, w, layer, row_fn, tm):
    m, d = x.shape
    n = w.shape[2]
    nt = m // tm
    nxt = lambda i: jnp.minimum(i + 1, nt - 1)
    once = pl.Buffered(1)
    return pl.pallas_call(
        _in_kernel,
        out_shape=jax.ShapeDtypeStruct((m, n), BF16),
        grid=(nt, 1),
        in_specs=[
            pl.BlockSpec((tm, d), lambda i, j: (0, 0), pipeline_mode=once),
            pl.BlockSpec((tm, d), lambda i, j: (nxt(i), 0)),
            _layer_vec_spec(d, layer),
            _mod_spec(d, layer, 0, lambda i: row_fn(0)),
            _mod_spec(d, layer, 1, lambda i: row_fn(0)),
            _mod_spec(d, layer, 0, lambda i: row_fn(nxt(i))),
            _mod_spec(d, layer, 1, lambda i: row_fn(nxt(i))),
            pl.BlockSpec((None, d, n), lambda i, j: (layer, 0, 0), pipeline_mode=once),
        ],
        out_specs=pl.BlockSpec((tm, n), lambda i, j: (i, 0)),
        scratch_shapes=[pltpu.VMEM((tm, d), BF16), pltpu.VMEM((tm, d), BF16)],
        compiler_params=_params("arbitrary", "arbitrary"),
        name="in_proj",
    )(x, x, g, mods, mods, mods, mods, w)


def _out_kernel(x_ref, ma_ref, mb_ref, wa_ref, wb_ref, gate_ref, o_ref):
    y = jnp.dot(ma_ref[...], wa_ref[...], preferred_element_type=F32)
    y = y + jnp.dot(mb_ref[...], wb_ref[...], preferred_element_type=F32)
    o_ref[...] = x_ref[...] + gate_ref[...] * y


def _out_proj(x, ma, mb, w, mods, layer, row_fn, tm):
    m, d = x.shape
    ka, kb = ma.shape[1], mb.shape[1]
    assert ka % kb == 0 and w.shape[1:] == (ka + kb, d)
    return pl.pallas_call(
        _out_kernel,
        out_shape=jax.ShapeDtypeStruct((m, d), F32),
        grid=(m // tm, 1),
        in_specs=[
            pl.BlockSpec((tm, d), lambda i, j: (i, 0)),
            pl.BlockSpec((tm, ka), lambda i, j: (i, 0)),
            pl.BlockSpec((tm, kb), lambda i, j: (i, 0)),
            pl.BlockSpec((None, ka, d), lambda i, j: (layer, 0, 0)),
            pl.BlockSpec((None, kb, d), lambda i, j: (layer, ka // kb, 0)),
            _mod_spec(d, layer, 2, row_fn),
        ],
        out_specs=pl.BlockSpec((tm, d), lambda i, j: (i, 0)),
        compiler_params=_params("parallel", "arbitrary"),
        name="out_proj",
    )(x, ma, mb, w, w, mods)


def _mlp_kernel(x_ref, x0_ref, xn_ref, g_ref, sh0_ref, sc0_ref, shn_ref, scn_ref, gate_ref, w1_ref, w2_ref, fg_ref,
                o_ref, h_scr, *, final_norm, n_parts):
    i = pl.program_id(0)
    f = pl.program_id(1)
    nf = n_parts
    slot = i % 2

    @pl.when((i == 0) & (f == 0))
    def _():
        _norm_mod_rows(x0_ref, g_ref, sh0_ref, sc0_ref, h_scr.at[0])

    def step(first, last):
        a = jnp.dot(h_scr[slot], w1_ref[...], preferred_element_type=F32)
        a = jnp.square(jnp.maximum(a, 0.0)).astype(BF16)
        y = jnp.dot(a, w2_ref[...], preferred_element_type=F32)
        if not first:
            y = o_ref[...] + y
        if last:
            y = x_ref[...] + gate_ref[...] * y
            if final_norm:
                y = y * lax.rsqrt(jnp.mean(y * y, axis=-1, keepdims=True) + RMS_EPS) * fg_ref[...]
        o_ref[...] = y
        _norm_mod_rows(xn_ref, g_ref, shn_ref, scn_ref, h_scr.at[1 - slot], unrolled=True, part=(f, n_parts))

    pl.when(f == 0)(lambda: step(True, False))
    pl.when((f > 0) & (f < nf - 1))(lambda: step(False, False))
    pl.when(f == nf - 1)(lambda: step(False, True))


def _mlp(x, mods, g, w1, w2, final_g, layer, row_fn, tm, final_norm):
    m, d = x.shape
    dff = w1.shape[2]
    tf = _tile(dff, 1024)
    nt = m // tm
    assert dff // tf >= 2
    nxt = lambda i: jnp.minimum(i + 1, nt - 1)
    return pl.pallas_call(
        functools.partial(_mlp_kernel, final_norm=final_norm, n_parts=dff // tf),
        out_shape=jax.ShapeDtypeStruct((m, d), F32),
        grid=(nt, dff // tf),
        in_specs=[
            pl.BlockSpec((tm, d), lambda i, j: (i, 0)),
            pl.BlockSpec((tm, d), lambda i, j: (0, 0), pipeline_mode=pl.Buffered(1)),
            pl.BlockSpec((tm, d), lambda i, j: (nxt(i), 0)),
            _layer_vec_spec(d, layer),
            _mod_spec(d, layer, 3, lambda i: row_fn(0)),
            _mod_spec(d, layer, 4, lambda i: row_fn(0)),
            _mod_spec(d, layer, 3, lambda i: row_fn(nxt(i))),
            _mod_spec(d, layer, 4, lambda i: row_fn(nxt(i))),
            _mod_spec(d, layer, 5, row_fn),
            pl.BlockSpec((None, d, tf), lambda i, j: (layer, 0, j)),
            pl.BlockSpec((None, tf, d), lambda i, j: (layer, j, 0)),
            pl.BlockSpec((1, d), lambda i, j: (0, 0)),
        ],
        out_specs=pl.BlockSpec((tm, d), lambda i, j: (i, 0)),
        scratch_shapes=[pltpu.VMEM((2, tm, d), BF16)],
        compiler_params=_params("arbitrary", "arbitrary"),
        name="mlp",
    )(x, x, x, g, mods, mods, mods, mods, mods, w1, w2, final_g.reshape(1, d))


def _sgu_kernel(u_ref, v_ref, gam_ref, ws_ref, b_ref, o_ref, *, groups):
    for c in range(u_ref.shape[0] // CHUNK):
        rs = slice(c * CHUNK, (c + 1) * CHUNK)
        for g in range(groups):
            cs = slice(g * HEAD_DIM, (g + 1) * HEAD_DIM)
            v = v_ref[rs, cs].astype(F32)
            mu = jnp.mean(v, axis=-1, keepdims=True)
            dv = v - mu
            var = jnp.mean(dv * dv, axis=-1, keepdims=True)
            vn = (dv * lax.rsqrt(var + LN_EPS) * gam_ref[:, cs]).astype(BF16)
            mixed = jnp.dot(ws_ref[g], vn, preferred_element_type=F32) + b_ref[g]
            o_ref[rs, cs] = (u_ref[rs, cs].astype(F32) * mixed).astype(o_ref.dtype)


def _sgu(p, gam, ws, b_full, layer, sgu_w):
    m = p.shape[0]
    groups = sgu_w // HEAD_DIM
    tr = _tile(m, 512)
    return pl.pallas_call(
        functools.partial(_sgu_kernel, groups=groups),
        out_shape=jax.ShapeDtypeStruct((m, sgu_w), BF16),
        grid=(m // tr,),
        in_specs=[
            pl.BlockSpec((tr, sgu_w), lambda i: (i, 0)),
            pl.BlockSpec((tr, sgu_w), lambda i: (i, 1)),
            pl.BlockSpec((None, 1, sgu_w), lambda i: (layer, 0, 0)),
            pl.BlockSpec((None, groups, CHUNK, CHUNK), lambda i: (layer, 0, 0, 0)),
            pl.BlockSpec((None, groups, CHUNK, HEAD_DIM), lambda i: (layer, 0, 0, 0)),
        ],
        out_specs=pl.BlockSpec((tr, sgu_w), lambda i: (i, 0)),
        compiler_params=_params("parallel"),
        name="sgu",
    )(p, p, gam, ws, b_full)


def _softmax_pv(scores, values):
    d = values[0].shape[1]
    m = scores[0].max(axis=-1, keepdims=True)
    for s in scores[1:]:
        m = jnp.maximum(m, s.max(axis=-1, keepdims=True))
    o = None
    for s, v in zip(scores, values):
        v1 = jnp.concatenate([v, jnp.ones(v.shape, v.dtype)], axis=1)
        pv = jnp.dot(jnp.exp2(s - m).astype(BF16), v1, preferred_element_type=F32)
        o = pv if o is None else o + pv
    return o[:, :d] / o[:, d:]


_NA_SCALE = HEAD_DIM ** -0.5 * math.log2(math.e)


def _na_kernel(q_ref, k0_ref, k1_ref, k2_ref, v0_ref, v1_ref, v2_ref, kc_ref, vc_ref, bias_ref, o_ref, *, heads):
    qb = q_ref.shape[0]
    scale = _NA_SCALE
    for h in range(heads):
        cs = slice(h * HEAD_DIM, (h + 1) * HEAD_DIM)
        q = (q_ref[:, cs].astype(F32) * scale).astype(BF16)
        scores, values = [], []
        for j, (k_ref, v_ref) in enumerate(((k0_ref, v0_ref), (k1_ref, v1_ref), (k2_ref, v2_ref))):
            s = lax.dot_general(q, k_ref[:, cs], _NT, preferred_element_type=F32)
            scores.append(s + bias_ref[h, :, j * qb:(j + 1) * qb])
            values.append(v_ref[:, cs])
        scores.append(lax.dot_general(q, kc_ref[:, cs], _NT, preferred_element_type=F32))
        values.append(vc_ref[:, cs])
        o_ref[:, cs] = _softmax_pv(scores, values).astype(o_ref.dtype)


def _na_bias(rpb, rows):
    heads = rpb.shape[0]
    kw = NA_KW
    col = jnp.arange(GRID_W)
    col_start = jnp.clip(col - kw // 2, 0, GRID_W - kw)
    col_mask = (col[None, :] >= col_start[:, None]) & (col[None, :] < col_start[:, None] + kw)
    dx = jnp.clip(col[None, :] - col[:, None], -(kw - 1), kw - 1) + (kw - 1)
    rpb_x = jnp.where(col_mask[None, None], rpb.astype(F32)[:, :, dx] * math.log2(math.e), -jnp.inf)
    outside = jnp.full((heads, GRID_W, GRID_W), -jnp.inf, F32)
    nblk = rows // Q_ROWS
    blocks = []
    for kblk in (0, 1, nblk - 1):
        ks = min(max(Q_ROWS * kblk - NA_KH // 2, 0), rows - K_ROWS)
        qrows = []
        for i in range(Q_ROWS):
            r = Q_ROWS * kblk + i
            rs = min(max(r - NA_KH // 2, 0), rows - NA_KH)
            tiles = [rpb_x[:, ks + j - r + NA_KH - 1] if rs <= ks + j < rs + NA_KH else outside for j in range(K_ROWS)]
            qrows.append(jnp.concatenate(tiles, axis=-1))
        blocks.append(jnp.concatenate(qrows, axis=1))
    return jnp.stack(blocks)


def _na_lat(p_lat, p_ctx, bias, bsz, seq, ctx_len, na_w, col0):
    heads = na_w // HEAD_DIM
    qb = Q_ROWS * GRID_W
    nblk = seq // qb
    assert nblk >= 4 and K_ROWS == 3 * Q_ROWS

    def kblock(j):
        return lambda b, k: (b * nblk + jnp.clip(k - 1, 0, nblk - 3) + j, col0 + 1)

    def vblock(j):
        return lambda b, k: (b * nblk + jnp.clip(k - 1, 0, nblk - 3) + j, col0 + 2)

    def btype(b, k):
        return (jnp.where(k == 0, 0, jnp.where(k == nblk - 1, 2, 1)), 0, 0, 0)

    return pl.pallas_call(
        functools.partial(_na_kernel, heads=heads),
        out_shape=jax.ShapeDtypeStruct((bsz * seq, na_w), BF16),
        grid=(bsz, nblk),
        in_specs=[
            pl.BlockSpec((qb, na_w), lambda b, k: (b * nblk + k, col0)),
            pl.BlockSpec((qb, na_w), kblock(0)),
            pl.BlockSpec((qb, na_w), kblock(1)),
            pl.BlockSpec((qb, na_w), kblock(2)),
            pl.BlockSpec((qb, na_w), vblock(0)),
            pl.BlockSpec((qb, na_w), vblock(1)),
            pl.BlockSpec((qb, na_w), vblock(2)),
            pl.BlockSpec((ctx_len, na_w), lambda b, k: (b, col0 + 1)),
            pl.BlockSpec((ctx_len, na_w), lambda b, k: (b, col0 + 2)),
            pl.BlockSpec((None, heads, qb, K_ROWS * GRID_W), btype),
        ],
        out_specs=pl.BlockSpec((qb, na_w), lambda b, k: (b * nblk + k, 0)),
        compiler_params=_params("parallel", "arbitrary"),
        name="na_lat",
    )(p_lat, p_lat, p_lat, p_lat, p_lat, p_lat, p_lat, p_ctx, p_ctx, bias)


def _attn_ctx_kernel(q_ref, k_ref, v_ref, o_ref, *, heads):
    for h in range(heads):
        cs = slice(h * HEAD_DIM, (h + 1) * HEAD_DIM)
        q = (q_ref[:, cs].astype(F32) * _NA_SCALE).astype(BF16)
        s = lax.dot_general(q, k_ref[:, cs], _NT, preferred_element_type=F32)
        o_ref[:, cs] = _softmax_pv([s], [v_ref[:, cs]]).astype(o_ref.dtype)


def _na_ctx(p_ctx, bsz, ctx_len, na_w, col0):
    heads = na_w // HEAD_DIM
    return pl.pallas_call(
        functools.partial(_attn_ctx_kernel, heads=heads),
        out_shape=jax.ShapeDtypeStruct((bsz * ctx_len, na_w), BF16),
        grid=(bsz,),
        in_specs=[pl.BlockSpec((ctx_len, na_w), lambda b, c=c: (b, col0 + c)) for c in range(3)],
        out_specs=pl.BlockSpec((ctx_len, na_w), lambda b: (b, 0)),
        compiler_params=_params("parallel"),
        name="na_ctx",
    )(p_ctx, p_ctx, p_ctx)


def _swap16(x):
    lane = lax.broadcasted_iota(jnp.int32, x.shape, 1)
    up = pltpu.roll(x, HEAD_DIM - 16, 1)
    down = pltpu.roll(x, 16, 1)
    return jnp.where((lane & 16) == 0, up, down)


def _diff_kernel(*refs, n_lat, rope, lam_init):
    if rope:
        (q_ref, k_ref, v_ref, kc_ref, vc_ref, cosq_ref, sinq_ref, cos_ref, sin_ref, lam_ref, g_ref,
         o_ref, k_scr, vt_scr, kmax_scr) = refs
    else:
        q_ref, kc_ref, vc_ref, lam_ref, g_ref, o_ref, k_scr, vt_scr, kmax_scr = refs
    n_ctx = kc_ref.shape[0]
    nk = n_lat + n_ctx
    hd = HEAD_DIM
    half = hd // 2

    @pl.when(pl.program_id(2) == 0)
    def _():
        step = 512
        row = lax.broadcasted_iota(jnp.int32, (hd, hd), 0)
        col = lax.broadcasted_iota(jnp.int32, (hd, hd), 1)
        sel = jnp.where(((col == 0) & (row < half)) | ((col == 1) & (row >= half)), 1.0, 0.0).astype(BF16)

        def put_keys(rs, kr, kmax):
            n = kr.shape[0]
            k_scr[rs, :hd] = kr
            lane = lax.broadcasted_iota(jnp.int32, (n, hd), 1)
            k_scr[rs, hd:] = jnp.where(lane == 0, 1.0, 0.0).astype(BF16)
            kf = kr.astype(F32)
            sq = jnp.dot((kf * kf).astype(BF16), sel, preferred_element_type=F32).max(axis=0, keepdims=True)
            return sq if kmax is None else jnp.maximum(kmax, sq)

        kmax = None
        for c in range(n_lat // step):
            rs = slice(c * step, (c + 1) * step)
            k = k_ref[rs, :].astype(F32)
            kmax = put_keys(rs, (k * cos_ref[rs, :] + _swap16(k) * sin_ref[rs, :]).astype(BF16), kmax)
            vt_scr[:hd, rs] = v_ref[rs, :].astype(F32).T.astype(BF16)
        kmax = put_keys(slice(n_lat, nk), kc_ref[...], kmax)
        vt_scr[:hd, n_lat:nk] = vc_ref[...].astype(F32).T.astype(BF16)
        vt_scr[hd:, :] = jnp.ones((SUM_ROWS, nk), BF16)
        lane1 = lax.broadcasted_iota(jnp.int32, kmax.shape, 1)
        for mp in range(2):
            kmax_scr[mp:mp + 1, :] = jnp.broadcast_to(
                jnp.max(jnp.where(lane1 == mp, kmax, 0.0), axis=-1, keepdims=True), kmax.shape)

    qraw = q_ref[...]
    q = qraw.astype(F32)
    tq = q.shape[0]
    row = lax.broadcasted_iota(jnp.int32, (hd, hd), 0)
    col = lax.broadcasted_iota(jnp.int32, (hd, hd), 1)
    sel1 = jnp.where((col == 0) & (row < half), 1.0, 0.0).astype(BF16)
    sel2 = jnp.where((col == 0) & (row >= half), 1.0, 0.0).astype(BF16)
    sq = (q * q).astype(BF16)
    n1 = jnp.dot(sq, sel1, preferred_element_type=F32)
    n2 = jnp.dot(sq, sel2, preferred_element_type=F32)
    if rope:
        perm = jnp.where(row == (col ^ 16), 1.0, 0.0).astype(BF16)
        q = q * cosq_ref[...] + jnp.dot(qraw, perm, preferred_element_type=F32) * sinq_ref[...]
    else:
        q = q * _Q_SCALE
    lane = lax.broadcasted_iota(jnp.int32, q.shape, 1)
    shift1 = jnp.sqrt(n1 * kmax_scr[0:1, :]) * (-_Q_SCALE * DIFF_BOUND_MARGIN)
    shift2 = jnp.sqrt(n2 * kmax_scr[1:2, :]) * (-_Q_SCALE * DIFF_BOUND_MARGIN)
    q_aug = jnp.concatenate([
        jnp.concatenate([jnp.where(lane < half, q, 0.0), shift1], axis=1),
        jnp.concatenate([jnp.where(lane >= half, q, 0.0), shift2], axis=1),
    ], axis=0).astype(BF16)

    lp = lam_ref[...]
    d1 = jnp.sum(lp[0:1, :] * lp[1:2, :], axis=-1, keepdims=True)
    d2 = jnp.sum(lp[2:3, :] * lp[3:4, :], axis=-1, keepdims=True)
    lam = jnp.exp(d1) - jnp.exp(d2) + lam_init

    chunks = [(k0, min(DIFF_TK, nk - k0)) for k0 in range(0, nk, DIFF_TK)]

    def finish(acc):
        l = acc[hd:hd + 1, :]
        c1 = 1.0 / l[:, :tq]
        c2 = lam / l[:, tq:]
        ot = acc[:hd, :tq] * c1 - acc[:hd, tq:] * c2
        ot = ot * lax.rsqrt(jnp.mean(ot * ot, axis=0, keepdims=True) + RMS_EPS)
        o_ref[...] = ((ot.T * g_ref[...]) * (1.0 - lam_init)).astype(o_ref.dtype)

    def pipelined(scores, consume):
        pending = [scores(c) for c in range(min(DIFF_AHEAD, len(chunks)))]
        for c in range(len(chunks)):
            s = pending.pop(0)
            if c + DIFF_AHEAD < len(chunks):
                pending.append(scores(c + DIFF_AHEAD))
            consume(c, s)

    state = {}

    def fast_scores(c):
        k0, tk = chunks[c]
        return lax.dot_general(k_scr[k0:k0 + tk, :], q_aug, _NT, preferred_element_type=F32)

    def fast_consume(c, s):
        k0, tk = chunks[c]
        pv = jnp.dot(vt_scr[:, k0:k0 + tk], jnp.exp2(s).astype(BF16), preferred_element_type=F32)
        state["acc"] = pv if c == 0 else state["acc"] + pv

    pipelined(fast_scores, fast_consume)
    acc = state["acc"]
    finish(acc)

    @pl.when(jnp.logical_not(jnp.min(acc[hd:hd + 1, :]) > DIFF_L_MIN))
    def _():
        qab = q_aug[:, :hd]
        st = {}

        def safe_scores(c):
            k0, tk = chunks[c]
            return lax.dot_general(k_scr[k0:k0 + tk, :hd], qab, _NT, preferred_element_type=F32)

        def safe_consume(c, s):
            k0, tk = chunks[c]
            mc = s.max(axis=0, keepdims=True)
            m_new = mc if c == 0 else jnp.maximum(st["m"], mc)
            pv = jnp.dot(vt_scr[:, k0:k0 + tk], jnp.exp2(s - m_new).astype(BF16), preferred_element_type=F32)
            st["acc"] = pv if c == 0 else st["acc"] * jnp.exp2(st["m"] - m_new) + pv
            st["m"] = m_new

        pipelined(safe_scores, safe_consume)
        finish(st["acc"])


def _diff_lat(p_lat, p_ctx, tabs, lam_p, sub_g, bsz, seq, ctx_len, heads, lam_init):
    tq = DIFF_TQ
    nq = seq // tq
    hd = HEAD_DIM
    nk = seq + ctx_len
    cosq, sinq, cos, sin = tabs
    return pl.pallas_call(
        functools.partial(_diff_kernel, n_lat=seq, rope=True, lam_init=lam_init),
        out_shape=jax.ShapeDtypeStruct((bsz * seq, heads * hd), BF16),
        grid=(bsz, heads, nq),
        in_specs=[
            pl.BlockSpec((tq, hd), lambda b, h, i: (b * nq + i, h)),
            pl.BlockSpec((seq, hd), lambda b, h, i: (b, heads + h)),
            pl.BlockSpec((seq, hd), lambda b, h, i: (b, 2 * heads + h)),
            pl.BlockSpec((ctx_len, hd), lambda b, h, i: (b, heads + h)),
            pl.BlockSpec((ctx_len, hd), lambda b, h, i: (b, 2 * heads + h)),
            pl.BlockSpec((tq, hd), lambda b, h, i: (i, 0)),
            pl.BlockSpec((tq, hd), lambda b, h, i: (i, 0)),
            pl.BlockSpec((seq, hd), lambda b, h, i: (0, 0)),
            pl.BlockSpec((seq, hd), lambda b, h, i: (0, 0)),
            pl.BlockSpec((4, hd // 2), lambda b, h, i: (0, 0)),
            pl.BlockSpec((1, hd), lambda b, h, i: (0, 0)),
        ],
        out_specs=pl.BlockSpec((tq, hd), lambda b, h, i: (b * nq + i, h)),
        scratch_shapes=[pltpu.VMEM((nk, 2 * hd), BF16), pltpu.VMEM((hd + SUM_ROWS, nk), BF16), pltpu.VMEM((8, hd), F32)],
        compiler_params=_params("parallel", "parallel", "arbitrary"),
        name="diff_lat",
    )(p_lat, p_lat, p_lat, p_ctx, p_ctx, cosq, sinq, cos, sin, lam_p, sub_g.reshape(1, hd))


def _diff_ctx(p_ctx, lam_p, sub_g, bsz, ctx_len, heads, lam_init):
    hd = HEAD_DIM
    return pl.pallas_call(
        functools.partial(_diff_kernel, n_lat=0, rope=False, lam_init=lam_init),
        out_shape=jax.ShapeDtypeStruct((bsz * ctx_len, heads * hd), BF16),
        grid=(bsz, heads, 1),
        in_specs=[
            pl.BlockSpec((ctx_len, hd), lambda b, h, i: (b, h)),
            pl.BlockSpec((ctx_len, hd), lambda b, h, i: (b, heads + h)),
            pl.BlockSpec((ctx_len, hd), lambda b, h, i: (b, 2 * heads + h)),
            pl.BlockSpec((4, hd // 2), lambda b, h, i: (0, 0)),
            pl.BlockSpec((1, hd), lambda b, h, i: (0, 0)),
        ],
        out_specs=pl.BlockSpec((ctx_len, hd), lambda b, h, i: (b, h)),
        scratch_shapes=[pltpu.VMEM((ctx_len, 2 * hd), BF16), pltpu.VMEM((hd + SUM_ROWS, ctx_len), BF16),
                        pltpu.VMEM((8, hd), F32)],
        compiler_params=_params("parallel", "parallel", "arbitrary"),
        name="diff_ctx",
    )(p_ctx, p_ctx, p_ctx, lam_p, sub_g.reshape(1, hd))


def _rope_tables(seq):
    axis = HEAD_DIM // 4
    t = jnp.arange(seq)
    row = (t // GRID_W).astype(F32)
    col = (t % GRID_W).astype(F32)
    inv = ROPE_THETA ** (-jnp.arange(0, axis, 2, dtype=F32) / axis)
    ang_r = row[:, None] * inv
    ang_c = col[:, None] * inv
    cr, sr, cc, sc = jnp.cos(ang_r), jnp.sin(ang_r), jnp.cos(ang_c), jnp.sin(ang_c)
    cos = jnp.concatenate([cr, cr, cc, cc] * 2, axis=-1)
    sin = jnp.concatenate([-sr, sr, -sc, sc] * 2, axis=-1)
    return cos * _Q_SCALE, sin * _Q_SCALE, cos, sin


def _fnet_kernel(f_ref, cn_ref, sn_ref, cc_ref, sc_ref, o_ref, a_scr, b_scr, *, norm):
    @pl.when(pl.program_id(1) == 0)
    def _():
        step = 512 if f_ref.shape[0] % 512 == 0 else f_ref.shape[0]
        for c in range(f_ref.shape[0] // step):
            rs = slice(c * step, (c + 1) * step)
            f = f_ref[rs, :]
            a_scr[rs, :] = jnp.dot(f, cc_ref[...], preferred_element_type=F32).astype(BF16)
            b_scr[rs, :] = jnp.dot(f, sc_ref[...], preferred_element_type=F32).astype(BF16)

    y = jnp.dot(cn_ref[...], a_scr[...], preferred_element_type=F32)
    y = y - jnp.dot(sn_ref[...], b_scr[...], preferred_element_type=F32)
    o_ref[...] = (y * norm).astype(o_ref.dtype)


def _fnet(p, cn, sn, ccb, scb, bsz, n, fw, col):
    tr = _tile(n, 512)
    norm = 1.0 / math.sqrt(n * HEAD_DIM)
    return pl.pallas_call(
        functools.partial(_fnet_kernel, norm=norm),
        out_shape=jax.ShapeDtypeStruct((bsz * n, fw), BF16),
        grid=(bsz, n // tr),
        in_specs=[
            pl.BlockSpec((n, fw), lambda b, i: (b, col)),
            pl.BlockSpec((tr, n), lambda b, i: (i, 0)),
            pl.BlockSpec((tr, n), lambda b, i: (i, 0)),
            pl.BlockSpec((fw, fw), lambda b, i: (0, 0)),
            pl.BlockSpec((fw, fw), lambda b, i: (0, 0)),
        ],
        out_specs=pl.BlockSpec((tr, fw), lambda b, i: (b * (n // tr) + i, 0)),
        scratch_shapes=[pltpu.VMEM((n, fw), BF16), pltpu.VMEM((n, fw), BF16)],
        compiler_params=_params("parallel", "arbitrary"),
        name="fnet",
    )(p, cn, sn, ccb, scb)


def _dft_tables(n):
    if n <= 256:
        idx = np.outer(np.arange(n), np.arange(n)) % n
        ang = 2.0 * np.pi * idx / n
        return jnp.asarray(np.cos(ang), F32).astype(BF16), jnp.asarray(np.sin(ang), F32).astype(BF16)
    r = int(round(math.sqrt(n)))
    assert r * r == n
    k = np.arange(n)
    t = np.arange(r)
    ang_a = 2.0 * np.pi * (np.outer(t, k) % r) / r
    ang_b = 2.0 * np.pi * (np.outer(t, k) % n) / n
    ca, sa = jnp.asarray(np.cos(ang_a), F32)[:, None, :], jnp.asarray(np.sin(ang_a), F32)[:, None, :]
    cb, sb = jnp.asarray(np.cos(ang_b), F32)[None, :, :], jnp.asarray(np.sin(ang_b), F32)[None, :, :]
    cos = (ca * cb - sa * sb).reshape(n, n).astype(BF16)
    sin = (sa * cb + ca * sb).reshape(n, n).astype(BF16)
    return cos, sin


def _channel_dft_tables(fw):
    idx = np.outer(np.arange(HEAD_DIM), np.arange(HEAD_DIM)) % HEAD_DIM
    ang = 2.0 * np.pi * idx / HEAD_DIM
    eye = np.eye(fw // HEAD_DIM)
    return (jnp.asarray(np.kron(eye, np.cos(ang)), F32).astype(BF16),
            jnp.asarray(np.kron(eye, np.sin(ang)), F32).astype(BF16))


def kernel(x, c, ctx, c_ctx, ada_w, ada_b, norm1_g, norm2_g, w_in, w_out, sgu_norm_g, sgu_w, sgu_b, na_rpb,
           diff_lq1, diff_lk1, diff_lq2, diff_lk2, diff_subln_g, mlp_w1, mlp_w2, final_g):
    bsz, seq, d = x.shape
    ctx_len = ctx.shape[1]
    depth = ada_w.shape[0]
    rows = seq // GRID_W
    mix_heads = d // HEAD_DIM
    sgu_w_dim = (mix_heads // 2) * HEAD_DIM
    na_w = d - sgu_w_dim
    diff_heads = (mix_heads * 3) // 4
    fnet_w = d - diff_heads * HEAD_DIM
    assert sgu_w_dim == na_w and bsz + 1 <= MOD_ROWS

    cond = jnp.zeros((MOD_ROWS, d), F32).at[:bsz].set(c).at[bsz].set(c_ctx)
    mods = _ada_table(cond, ada_w, ada_b).reshape(depth, MOD_ROWS, 1, ADA_CHUNKS * d)

    w_in_b, w_out_b = w_in.astype(BF16), w_out.astype(BF16)
    w1_b, w2_b = mlp_w1.astype(BF16), mlp_w2.astype(BF16)
    sgu_w_b = sgu_w.astype(BF16)
    n1g = norm1_g.reshape(depth, 1, d)
    n2g = norm2_g.reshape(depth, 1, d)
    sgu_g = sgu_norm_g.reshape(-1, 1, sgu_w_dim)
    sgu_b_full = jnp.broadcast_to(sgu_b[..., None], sgu_b.shape + (HEAD_DIM,)).astype(F32)

    tm_lat = _tile(seq, 512)
    tm_ctx = _tile(bsz * ctx_len, 512)
    lat_row = lambda i: (i * tm_lat) // seq
    ctx_row = lambda i: bsz

    x_lat = x.reshape(bsz * seq, d)
    x_ctx = ctx.reshape(bsz * ctx_len, d)

    if depth > 1:
        rope_tabs = _rope_tables(seq)
        cn, sn = _dft_tables(seq)
        cn_c, sn_c = _dft_tables(ctx_len)
        ccb, scb = _channel_dft_tables(fnet_w)

    for l in range(depth):
        need_ctx = l < depth - 1
        p_lat = _in_proj(x_lat, mods, n1g, w_in_b, l, lat_row, tm_lat)
        p_ctx = _in_proj(x_ctx, mods, n1g, w_in_b, l, ctx_row, tm_ctx)
        i = l // 2
        ma_ctx = mb_ctx = None
        if l % 2 == 0:
            bias = _na_bias(na_rpb[i], rows)
            ma_lat = _sgu(p_lat, sgu_g, sgu_w_b, sgu_b_full, i, sgu_w_dim)
            mb_lat = _na_lat(p_lat, p_ctx, bias, bsz, seq, ctx_len, na_w, 2)
            if need_ctx:
                ma_ctx = _sgu(p_ctx, sgu_g, sgu_w_b, sgu_b_full, i, sgu_w_dim)
                mb_ctx = _na_ctx(p_ctx, bsz, ctx_len, na_w, 2)
        else:
            lam_init = 0.8 - 0.6 * math.exp(-0.3 * l)
            lam_p = jnp.stack([diff_lq1[i], diff_lk1[i], diff_lq2[i], diff_lk2[i]]).astype(F32)
            fcol = (3 * diff_heads * HEAD_DIM) // fnet_w
            ma_lat = _diff_lat(p_lat, p_ctx, rope_tabs, lam_p, diff_subln_g[i], bsz, seq, ctx_len, diff_heads, lam_init)
            mb_lat = _fnet(p_lat, cn, sn, ccb, scb, bsz, seq, fnet_w, fcol)
            if need_ctx:
                ma_ctx = _diff_ctx(p_ctx, lam_p, diff_subln_g[i], bsz, ctx_len, diff_heads, lam_init)
                mb_ctx = _fnet(p_ctx, cn_c, sn_c, ccb, scb, bsz, ctx_len, fnet_w, fcol)
        x_lat = _out_proj(x_lat, ma_lat, mb_lat, w_out_b, mods, l, lat_row, tm_lat)
        x_lat = _mlp(x_lat, mods, n2g, w1_b, w2_b, final_g, l, lat_row, tm_lat, final_norm=not need_ctx)
        if need_ctx:
            x_ctx = _out_proj(x_ctx, ma_ctx, mb_ctx, w_out_b, mods, l, ctx_row, tm_ctx)
            x_ctx = _mlp(x_ctx, mods, n2g, w1_b, w2_b, final_g, l, ctx_row, tm_ctx, final_norm=False)
    return x_lat.reshape(bsz, seq, d)
```

```python
import functools
import math

import numpy as np
import jax
import jax.numpy as jnp
from jax import lax
from jax.experimental import pallas as pl
from jax.experimental.pallas import tpu as pltpu

F32 = jnp.float32
BF16 = jnp.bfloat16

GRID_W = 64
HEAD_DIM = 128
CHUNK = 128
NA_KH = 8
NA_KW = 16
ROPE_THETA = 10000.0
Q_ROWS = 4
K_ROWS = Q_ROWS + NA_KH
RMS_EPS = 1e-6
LN_EPS = 1e-5
ADA_CHUNKS = 6
MOD_ROWS = 16

VMEM_LIMIT = 56 * 1024 * 1024

_NT = (((1,), (1,)), ((), ()))

_Q_SCALE = (HEAD_DIM // 2) ** -0.5 * math.log2(math.e)
DIFF_TQ = 1024
DIFF_TK = 1024
DIFF_AHEAD = 1
DIFF_BOUND_MARGIN = 1.01
DIFF_L_MIN = 2.0 ** -100
SUM_ROWS = 16


def _params(*sem):
    return pltpu.CompilerParams(dimension_semantics=sem, vmem_limit_bytes=VMEM_LIMIT)


def _tile(n, pref):
    if n <= pref:
        return n
    t = (pref // 128) * 128
    while n % t:
        t -= 128
    assert t > 0, (n, pref)
    return t


def _ada_kernel(s_ref, w_ref, b_ref, o_ref):
    s = s_ref[...]
    a = (s * jax.nn.sigmoid(s)).astype(BF16)
    o_ref[...] = jnp.dot(a, w_ref[...].astype(BF16), preferred_element_type=F32) + b_ref[...]


def _ada_table(cond, ada_w, ada_b):
    depth, d, n = ada_w.shape
    tn = _tile(n, 1024)
    return pl.pallas_call(
        _ada_kernel,
        out_shape=jax.ShapeDtypeStruct((depth, MOD_ROWS, n), F32),
        grid=(depth, n // tn),
        in_specs=[
            pl.BlockSpec((MOD_ROWS, d), lambda l, j: (0, 0)),
            pl.BlockSpec((None, d, tn), lambda l, j: (l, 0, j)),
            pl.BlockSpec((None, 1, tn), lambda l, j: (l, 0, j)),
        ],
        out_specs=pl.BlockSpec((None, MOD_ROWS, tn), lambda l, j: (l, 0, j)),
        compiler_params=_params("parallel", "parallel"),
        name="ada_table",
    )(cond, ada_w, ada_b.reshape(depth, 1, n))


def _norm_mod_rows(x_ref, g_ref, sh_ref, sc_ref, h_ref, unrolled=False, part=None):
    rows = 64
    g = g_ref[...]
    sc1 = 1.0 + sc_ref[...]
    sh = sh_ref[...]
    total = x_ref.shape[0] // rows
    if part is not None:
        assert unrolled and total % part[1] == 0
        total //= part[1]

    def body(r, carry):
        if part is not None:
            sl = pl.ds(pl.multiple_of((part[0] * total + r) * rows, rows), rows)
        elif unrolled:
            sl = slice(r * rows, (r + 1) * rows)
        else:
            sl = pl.ds(pl.multiple_of(r * rows, rows), rows)
        xv = x_ref[sl, :]
        ms = jnp.mean(xv * xv, axis=-1, keepdims=True)
        y = xv * lax.rsqrt(ms + RMS_EPS) * g
        h_ref[sl, :] = (y * sc1 + sh).astype(h_ref.dtype)
        return carry

    if unrolled:
        for r in range(total):
            body(r, 0)
    else:
        lax.fori_loop(0, total, body, 0)


def _mod_spec(d, layer, chunk, row_fn):
    return pl.BlockSpec((None, None, 1, d), lambda i, j: (layer, row_fn(i), 0, chunk))


def _layer_vec_spec(d, layer):
    return pl.BlockSpec((None, 1, d), lambda i, j: (layer, 0, 0))


def _in_kernel(x0_ref, xn_ref, g_ref, sh0_ref, sc0_ref, shn_ref, scn_ref, w_ref, o_ref, h0_scr, h1_scr):
    i = pl.program_id(0)

    @pl.when(i == 0)
    def _():
        _norm_mod_rows(x0_ref, g_ref, sh0_ref, sc0_ref, h0_scr)

    def step(h_cur, h_next):
        o_ref[...] = jnp.dot(h_cur[...], w_ref[...], preferred_element_type=F32).astype(o_ref.dtype)
        _norm_mod_rows(xn_ref, g_ref, shn_ref, scn_ref, h_next, unrolled=True)

    @pl.when(i % 2 == 0)
    def _():
        step(h0_scr, h1_scr)

    @pl.when(i % 2 == 1)
    def _():
        step(h1_scr, h0_scr)


def _in_proj(x, mods, g, w, layer, row_fn, tm):
    m, d = x.shape
    n = w.shape[1]
    nt = m // tm
    nxt = lambda i: jnp.minimum(i + 1, nt - 1)
    once = pl.Buffered(1)
    return pl.pallas_call(
        _in_kernel,
        out_shape=jax.ShapeDtypeStruct((m, n), BF16),
        grid=(nt, 1),
        in_specs=[
            pl.BlockSpec((tm, d), lambda i, j: (0, 0), pipeline_mode=once),
            pl.BlockSpec((tm, d), lambda i, j: (nxt(i), 0)),
            _layer_vec_spec(d, layer),
            _mod_spec(d, layer, 0, lambda i: row_fn(0)),
            _mod_spec(d, layer, 1, lambda i: row_fn(0)),
            _mod_spec(d, layer, 0, lambda i: row_fn(nxt(i))),
            _mod_spec(d, layer, 1, lambda i: row_fn(nxt(i))),
            pl.BlockSpec((d, n), lambda i, j: (0, 0), pipeline_mode=once),
        ],
        out_specs=pl.BlockSpec((tm, n), lambda i, j: (i, 0)),
        scratch_shapes=[pltpu.VMEM((tm, d), BF16), pltpu.VMEM((tm, d), BF16)],
        compiler_params=_params("arbitrary", "arbitrary"),
        name="in_proj",
    )(x, x, g, mods, mods, mods, mods, w)


def _out_kernel(x_ref, ma_ref, mb_ref, wa_ref, wb_ref, gate_ref, o_ref):
    y = jnp.dot(ma_ref[...], wa_ref[...], preferred_element_type=F32)
    y = y + jnp.dot(mb_ref[...], wb_ref[...], preferred_element_type=F32)
    o_ref[...] = x_ref[...] + gate_ref[...] * y


def _out_proj(x, ma, mb, w, mods, layer, row_fn, tm):
    m, d = x.shape
    ka, kb = ma.shape[1], mb.shape[1]
    assert ka % kb == 0 and w.shape[1:] == (ka + kb, d)
    return pl.pallas_call(
        _out_kernel,
        out_shape=jax.ShapeDtypeStruct((m, d), F32),
        grid=(m // tm, 1),
        in_specs=[
            pl.BlockSpec((tm, d), lambda i, j: (i, 0)),
            pl.BlockSpec((tm, ka), lambda i, j: (i, 0)),
            pl.BlockSpec((tm, kb), lambda i, j: (i, 0)),
            pl.BlockSpec((None, ka, d), lambda i, j: (layer, 0, 0)),
            pl.BlockSpec((None, kb, d), lambda i, j: (layer, ka // kb, 0)),
            _mod_spec(d, layer, 2, row_fn),
        ],
        out_specs=pl.BlockSpec((tm, d), lambda i, j: (i, 0)),
        compiler_params=_params("parallel", "arbitrary"),
        name="out_proj",
    )(x, ma, mb, w, w, mods)


def _mlp_kernel(x_ref, x0_ref, xn_ref, g_ref, sh0_ref, sc0_ref, shn_ref, scn_ref, gate_ref, w1_ref, w2_ref, fg_ref,
                o_ref, h_scr, *, final_norm, n_parts):
    i = pl.program_id(0)
    f = pl.program_id(1)
    nf = n_parts
    slot = i % 2

    @pl.when((i == 0) & (f == 0))
    def _():
        _norm_mod_rows(x0_ref, g_ref, sh0_ref, sc0_ref, h_scr.at[0])

    def step(first, last):
        a = jnp.dot(h_scr[slot], w1_ref[...], preferred_element_type=F32)
        a = jnp.square(jnp.maximum(a, 0.0)).astype(BF16)
        y = jnp.dot(a, w2_ref[...], preferred_element_type=F32)
        if not first:
            y = o_ref[...] + y
        if last:
            y = x_ref[...] + gate_ref[...] * y
            if final_norm:
                y = y * lax.rsqrt(jnp.mean(y * y, axis=-1, keepdims=True) + RMS_EPS) * fg_ref[...]
        o_ref[...] = y
        _norm_mod_rows(xn_ref, g_ref, shn_ref, scn_ref, h_scr.at[1 - slot], unrolled=True, part=(f, n_parts))

    pl.when(f == 0)(lambda: step(True, False))
    pl.when((f > 0) & (f < nf - 1))(lambda: step(False, False))
    pl.when(f == nf - 1)(lambda: step(False, True))


def _mlp(x, mods, g, w1, w2, final_g, layer, row_fn, tm, final_norm):
    m, d = x.shape
    dff = w1.shape[1]
    tf = _tile(dff, 1024)
    nt = m // tm
    assert dff // tf >= 2
    nxt = lambda i: jnp.minimum(i + 1, nt - 1)
    return pl.pallas_call(
        functools.partial(_mlp_kernel, final_norm=final_norm, n_parts=dff // tf),
        out_shape=jax.ShapeDtypeStruct((m, d), F32),
        grid=(nt, dff // tf),
        in_specs=[
            pl.BlockSpec((tm, d), lambda i, j: (i, 0)),
            pl.BlockSpec((tm, d), lambda i, j: (0, 0), pipeline_mode=pl.Buffered(1)),
            pl.BlockSpec((tm, d), lambda i, j: (nxt(i), 0)),
            _layer_vec_spec(d, layer),
            _mod_spec(d, layer, 3, lambda i: row_fn(0)),
            _mod_spec(d, layer, 4, lambda i: row_fn(0)),
            _mod_spec(d, layer, 3, lambda i: row_fn(nxt(i))),
            _mod_spec(d, layer, 4, lambda i: row_fn(nxt(i))),
            _mod_spec(d, layer, 5, row_fn),
            pl.BlockSpec((d, tf), lambda i, j: (0, j)),
            pl.BlockSpec((tf, d), lambda i, j: (j, 0)),
            pl.BlockSpec((1, d), lambda i, j: (0, 0)),
        ],
        out_specs=pl.BlockSpec((tm, d), lambda i, j: (i, 0)),
        scratch_shapes=[pltpu.VMEM((2, tm, d), BF16)],
        compiler_params=_params("arbitrary", "arbitrary"),
        name="mlp",
    )(x, x, x, g, mods, mods, mods, mods, mods, w1, w2, final_g.reshape(1, d))


def _in_ctx_kernel(x_ref, g_ref, sh_ref, sc_ref, w_ref, o_ref, wb_ref, h_scr):
    @pl.when(pl.program_id(0) == 0)
    def _():
        _norm_mod_rows(x_ref, g_ref, sh_ref, sc_ref, h_scr)

    wb = w_ref[...].astype(BF16)
    wb_ref[...] = wb
    o_ref[...] = jnp.dot(h_scr[...], wb, preferred_element_type=F32).astype(o_ref.dtype)


def _ctx_vec_specs(d, layer, row, chunks):
    specs = [pl.BlockSpec((None, 1, d), lambda j: (layer, 0, 0))]
    for chunk in chunks:
        specs.append(pl.BlockSpec((None, None, 1, d), lambda j, chunk=chunk: (layer, row, 0, chunk)))
    return specs


def _in_proj_ctx(x, mods, g, w, layer, row):
    m, d = x.shape
    n = w.shape[2]
    tn = _tile(n, 1024)
    return pl.pallas_call(
        _in_ctx_kernel,
        out_shape=(jax.ShapeDtypeStruct((m, n), BF16), jax.ShapeDtypeStruct((d, n), BF16)),
        grid=(n // tn,),
        in_specs=[pl.BlockSpec((m, d), lambda j: (0, 0), pipeline_mode=pl.Buffered(1))]
        + _ctx_vec_specs(d, layer, row, (0, 1))
        + [pl.BlockSpec((None, d, tn), lambda j: (layer, 0, j))],
        out_specs=(pl.BlockSpec((m, tn), lambda j: (0, j)), pl.BlockSpec((d, tn), lambda j: (0, j))),
        scratch_shapes=[pltpu.VMEM((m, d), BF16)],
        compiler_params=_params("arbitrary"),
        name="in_proj_ctx",
    )(x, g, mods, mods, w)


def _mlp_ctx_kernel(x_ref, g_ref, sh_ref, sc_ref, gate_ref, w1_ref, w2_ref, o_ref, w1b_ref, w2b_ref, h_scr, *, n_parts):
    f = pl.program_id(0)

    @pl.when(f == 0)
    def _():
        _norm_mod_rows(x_ref, g_ref, sh_ref, sc_ref, h_scr)

    w1b = w1_ref[...].astype(BF16)
    w2b = w2_ref[...].astype(BF16)
    w1b_ref[...] = w1b
    w2b_ref[...] = w2b
    a = jnp.dot(h_scr[...], w1b, preferred_element_type=F32)
    a = jnp.square(jnp.maximum(a, 0.0)).astype(BF16)
    y = jnp.dot(a, w2b, preferred_element_type=F32)

    @pl.when(f == 0)
    def _():
        o_ref[...] = y

    @pl.when((f > 0) & (f < n_parts - 1))
    def _():
        o_ref[...] += y

    @pl.when(f == n_parts - 1)
    def _():
        o_ref[...] = x_ref[...] + gate_ref[...] * (o_ref[...] + y)


def _mlp_ctx(x, mods, g, w1, w2, layer, row):
    m, d = x.shape
    dff = w1.shape[2]
    tf = _tile(dff, 256)
    assert dff // tf >= 2
    return pl.pallas_call(
        functools.partial(_mlp_ctx_kernel, n_parts=dff // tf),
        out_shape=(jax.ShapeDtypeStruct((m, d), F32), jax.ShapeDtypeStruct((d, dff), BF16),
                   jax.ShapeDtypeStruct((dff, d), BF16)),
        grid=(dff // tf,),
        in_specs=[pl.BlockSpec((m, d), lambda j: (0, 0), pipeline_mode=pl.Buffered(1))]
        + _ctx_vec_specs(d, layer, row, (3, 4, 5))
        + [pl.BlockSpec((None, d, tf), lambda j: (layer, 0, j)), pl.BlockSpec((None, tf, d), lambda j: (layer, j, 0))],
        out_specs=(pl.BlockSpec((m, d), lambda j: (0, 0)), pl.BlockSpec((d, tf), lambda j: (0, j)),
                   pl.BlockSpec((tf, d), lambda j: (j, 0))),
        scratch_shapes=[pltpu.VMEM((m, d), BF16)],
        compiler_params=_params("arbitrary"),
        name="mlp_ctx",
    )(x, g, mods, mods, mods, w1, w2)


def _sgu_kernel(u_ref, v_ref, gam_ref, ws_ref, b_ref, o_ref, *, groups):
    for c in range(u_ref.shape[0] // CHUNK):
        rs = slice(c * CHUNK, (c + 1) * CHUNK)
        for g in range(groups):
            cs = slice(g * HEAD_DIM, (g + 1) * HEAD_DIM)
            v = v_ref[rs, cs].astype(F32)
            mu = jnp.mean(v, axis=-1, keepdims=True)
            dv = v - mu
            var = jnp.mean(dv * dv, axis=-1, keepdims=True)
            vn = (dv * lax.rsqrt(var + LN_EPS) * gam_ref[:, cs]).astype(BF16)
            mixed = jnp.dot(ws_ref[g], vn, preferred_element_type=F32) + b_ref[g]
            o_ref[rs, cs] = (u_ref[rs, cs].astype(F32) * mixed).astype(o_ref.dtype)


def _sgu(p, gam, ws, b_full, layer, sgu_w):
    m = p.shape[0]
    groups = sgu_w // HEAD_DIM
    tr = _tile(m, 512)
    return pl.pallas_call(
        functools.partial(_sgu_kernel, groups=groups),
        out_shape=jax.ShapeDtypeStruct((m, sgu_w), BF16),
        grid=(m // tr,),
        in_specs=[
            pl.BlockSpec((tr, sgu_w), lambda i: (i, 0)),
            pl.BlockSpec((tr, sgu_w), lambda i: (i, 1)),
            pl.BlockSpec((None, 1, sgu_w), lambda i: (layer, 0, 0)),
            pl.BlockSpec((None, groups, CHUNK, CHUNK), lambda i: (layer, 0, 0, 0)),
            pl.BlockSpec((None, groups, CHUNK, HEAD_DIM), lambda i: (layer, 0, 0, 0)),
        ],
        out_specs=pl.BlockSpec((tr, sgu_w), lambda i: (i, 0)),
        compiler_params=_params("parallel"),
        name="sgu",
    )(p, p, gam, ws, b_full)


def _softmax_pv(scores, values):
    d = values[0].shape[1]
    m = scores[0].max(axis=-1, keepdims=True)
    for s in scores[1:]:
        m = jnp.maximum(m, s.max(axis=-1, keepdims=True))
    o = None
    for s, v in zip(scores, values):
        v1 = jnp.concatenate([v, jnp.ones(v.shape, v.dtype)], axis=1)
        pv = jnp.dot(jnp.exp2(s - m).astype(BF16), v1, preferred_element_type=F32)
        o = pv if o is None else o + pv
    return o[:, :d] / o[:, d:]


_NA_SCALE = HEAD_DIM ** -0.5 * math.log2(math.e)


def _na_kernel(q_ref, k0_ref, k1_ref, k2_ref, v0_ref, v1_ref, v2_ref, kc_ref, vc_ref, bias_ref, o_ref, *, heads):
    qb = q_ref.shape[0]
    scale = _NA_SCALE
    for h in range(heads):
        cs = slice(h * HEAD_DIM, (h + 1) * HEAD_DIM)
        q = (q_ref[:, cs].astype(F32) * scale).astype(BF16)
        scores, values = [], []
        for j, (k_ref, v_ref) in enumerate(((k0_ref, v0_ref), (k1_ref, v1_ref), (k2_ref, v2_ref))):
            s = lax.dot_general(q, k_ref[:, cs], _NT, preferred_element_type=F32)
            scores.append(s + bias_ref[h, :, j * qb:(j + 1) * qb])
            values.append(v_ref[:, cs])
        scores.append(lax.dot_general(q, kc_ref[:, cs], _NT, preferred_element_type=F32))
        values.append(vc_ref[:, cs])
        o_ref[:, cs] = _softmax_pv(scores, values).astype(o_ref.dtype)


def _na_bias(rpb, rows):
    heads = rpb.shape[0]
    kw = NA_KW
    col = jnp.arange(GRID_W)
    col_start = jnp.clip(col - kw // 2, 0, GRID_W - kw)
    col_mask = (col[None, :] >= col_start[:, None]) & (col[None, :] < col_start[:, None] + kw)
    dx = jnp.clip(col[None, :] - col[:, None], -(kw - 1), kw - 1) + (kw - 1)
    rpb_x = jnp.where(col_mask[None, None], rpb.astype(F32)[:, :, dx] * math.log2(math.e), -jnp.inf)
    outside = jnp.full((heads, GRID_W, GRID_W), -jnp.inf, F32)
    nblk = rows // Q_ROWS
    blocks = []
    for kblk in (0, 1, nblk - 1):
        ks = min(max(Q_ROWS * kblk - NA_KH // 2, 0), rows - K_ROWS)
        qrows = []
        for i in range(Q_ROWS):
            r = Q_ROWS * kblk + i
            rs = min(max(r - NA_KH // 2, 0), rows - NA_KH)
            tiles = [rpb_x[:, ks + j - r + NA_KH - 1] if rs <= ks + j < rs + NA_KH else outside for j in range(K_ROWS)]
            qrows.append(jnp.concatenate(tiles, axis=-1))
        blocks.append(jnp.concatenate(qrows, axis=1))
    return jnp.stack(blocks)


def _na_lat(p_lat, p_ctx, bias, bsz, seq, ctx_len, na_w, col0):
    heads = na_w // HEAD_DIM
    qb = Q_ROWS * GRID_W
    nblk = seq // qb
    assert nblk >= 4 and K_ROWS == 3 * Q_ROWS

    def kblock(j):
        return lambda b, k: (b * nblk + jnp.clip(k - 1, 0, nblk - 3) + j, col0 + 1)

    def vblock(j):
        return lambda b, k: (b * nblk + jnp.clip(k - 1, 0, nblk - 3) + j, col0 + 2)

    def btype(b, k):
        return (jnp.where(k == 0, 0, jnp.where(k == nblk - 1, 2, 1)), 0, 0, 0)

    return pl.pallas_call(
        functools.partial(_na_kernel, heads=heads),
        out_shape=jax.ShapeDtypeStruct((bsz * seq, na_w), BF16),
        grid=(bsz, nblk),
        in_specs=[
            pl.BlockSpec((qb, na_w), lambda b, k: (b * nblk + k, col0)),
            pl.BlockSpec((qb, na_w), kblock(0)),
            pl.BlockSpec((qb, na_w), kblock(1)),
            pl.BlockSpec((qb, na_w), kblock(2)),
            pl.BlockSpec((qb, na_w), vblock(0)),
            pl.BlockSpec((qb, na_w), vblock(1)),
            pl.BlockSpec((qb, na_w), vblock(2)),
            pl.BlockSpec((ctx_len, na_w), lambda b, k: (b, col0 + 1)),
            pl.BlockSpec((ctx_len, na_w), lambda b, k: (b, col0 + 2)),
            pl.BlockSpec((None, heads, qb, K_ROWS * GRID_W), btype),
        ],
        out_specs=pl.BlockSpec((qb, na_w), lambda b, k: (b * nblk + k, 0)),
        compiler_params=_params("parallel", "arbitrary"),
        name="na_lat",
    )(p_lat, p_lat, p_lat, p_lat, p_lat, p_lat, p_lat, p_ctx, p_ctx, bias)


def _attn_ctx_kernel(q_ref, k_ref, v_ref, o_ref, *, heads):
    for h in range(heads):
        cs = slice(h * HEAD_DIM, (h + 1) * HEAD_DIM)
        q = (q_ref[:, cs].astype(F32) * _NA_SCALE).astype(BF16)
        s = lax.dot_general(q, k_ref[:, cs], _NT, preferred_element_type=F32)
        o_ref[:, cs] = _softmax_pv([s], [v_ref[:, cs]]).astype(o_ref.dtype)


def _na_ctx(p_ctx, bsz, ctx_len, na_w, col0):
    heads = na_w // HEAD_DIM
    return pl.pallas_call(
        functools.partial(_attn_ctx_kernel, heads=heads),
        out_shape=jax.ShapeDtypeStruct((bsz * ctx_len, na_w), BF16),
        grid=(bsz,),
        in_specs=[pl.BlockSpec((ctx_len, na_w), lambda b, c=c: (b, col0 + c)) for c in range(3)],
        out_specs=pl.BlockSpec((ctx_len, na_w), lambda b: (b, 0)),
        compiler_params=_params("parallel"),
        name="na_ctx",
    )(p_ctx, p_ctx, p_ctx)


def _swap16(x):
    lane = lax.broadcasted_iota(jnp.int32, x.shape, 1)
    up = pltpu.roll(x, HEAD_DIM - 16, 1)
    down = pltpu.roll(x, 16, 1)
    return jnp.where((lane & 16) == 0, up, down)


def _diff_kernel(*refs, n_lat, rope, lam_init):
    if rope:
        (q_ref, k_ref, v_ref, kc_ref, vc_ref, cosq_ref, sinq_ref, cos_ref, sin_ref, lam_ref, g_ref,
         o_ref, k_scr, vt_scr, kmax_scr) = refs
    else:
        q_ref, kc_ref, vc_ref, lam_ref, g_ref, o_ref, k_scr, vt_scr, kmax_scr = refs
    n_ctx = kc_ref.shape[0]
    nk = n_lat + n_ctx
    hd = HEAD_DIM
    half = hd // 2

    @pl.when(pl.program_id(2) == 0)
    def _():
        step = 512
        row = lax.broadcasted_iota(jnp.int32, (hd, hd), 0)
        col = lax.broadcasted_iota(jnp.int32, (hd, hd), 1)
        sel = jnp.where(((col == 0) & (row < half)) | ((col == 1) & (row >= half)), 1.0, 0.0).astype(BF16)

        def put_keys(rs, kr, kmax):
            n = kr.shape[0]
            k_scr[rs, :hd] = kr
            lane = lax.broadcasted_iota(jnp.int32, (n, hd), 1)
            k_scr[rs, hd:] = jnp.where(lane == 0, 1.0, 0.0).astype(BF16)
            kf = kr.astype(F32)
            sq = jnp.dot((kf * kf).astype(BF16), sel, preferred_element_type=F32).max(axis=0, keepdims=True)
            return sq if kmax is None else jnp.maximum(kmax, sq)

        kmax = None
        for c in range(n_lat // step):
            rs = slice(c * step, (c + 1) * step)
            k = k_ref[rs, :].astype(F32)
            kmax = put_keys(rs, (k * cos_ref[rs, :] + _swap16(k) * sin_ref[rs, :]).astype(BF16), kmax)
            vt_scr[:hd, rs] = v_ref[rs, :].astype(F32).T.astype(BF16)
        kmax = put_keys(slice(n_lat, nk), kc_ref[...], kmax)
        vt_scr[:hd, n_lat:nk] = vc_ref[...].astype(F32).T.astype(BF16)
        vt_scr[hd:, :] = jnp.ones((SUM_ROWS, nk), BF16)
        lane1 = lax.broadcasted_iota(jnp.int32, kmax.shape, 1)
        for mp in range(2):
            kmax_scr[mp:mp + 1, :] = jnp.broadcast_to(
                jnp.max(jnp.where(lane1 == mp, kmax, 0.0), axis=-1, keepdims=True), kmax.shape)

    qraw = q_ref[...]
    q = qraw.astype(F32)
    tq = q.shape[0]
    row = lax.broadcasted_iota(jnp.int32, (hd, hd), 0)
    col = lax.broadcasted_iota(jnp.int32, (hd, hd), 1)
    sel1 = jnp.where((col == 0) & (row < half), 1.0, 0.0).astype(BF16)
    sel2 = jnp.where((col == 0) & (row >= half), 1.0, 0.0).astype(BF16)
    sq = (q * q).astype(BF16)
    n1 = jnp.dot(sq, sel1, preferred_element_type=F32)
    n2 = jnp.dot(sq, sel2, preferred_element_type=F32)
    if rope:
        perm = jnp.where(row == (col ^ 16), 1.0, 0.0).astype(BF16)
        q = q * cosq_ref[...] + jnp.dot(qraw, perm, preferred_element_type=F32) * sinq_ref[...]
    else:
        q = q * _Q_SCALE
    lane = lax.broadcasted_iota(jnp.int32, q.shape, 1)
    shift1 = jnp.sqrt(n1 * kmax_scr[0:1, :]) * (-_Q_SCALE * DIFF_BOUND_MARGIN)
    shift2 = jnp.sqrt(n2 * kmax_scr[1:2, :]) * (-_Q_SCALE * DIFF_BOUND_MARGIN)
    q_aug = jnp.concatenate([
        jnp.concatenate([jnp.where(lane < half, q, 0.0), shift1], axis=1),
        jnp.concatenate([jnp.where(lane >= half, q, 0.0), shift2], axis=1),
    ], axis=0).astype(BF16)

    lp = lam_ref[...]
    d1 = jnp.sum(lp[0:1, :] * lp[1:2, :], axis=-1, keepdims=True)
    d2 = jnp.sum(lp[2:3, :] * lp[3:4, :], axis=-1, keepdims=True)
    lam = jnp.exp(d1) - jnp.exp(d2) + lam_init

    chunks = [(k0, min(DIFF_TK, nk - k0)) for k0 in range(0, nk, DIFF_TK)]

    def finish(acc):
        l = acc[hd:hd + 1, :]
        c1 = 1.0 / l[:, :tq]
        c2 = lam / l[:, tq:]
        ot = acc[:hd, :tq] * c1 - acc[:hd, tq:] * c2
        ot = ot * lax.rsqrt(jnp.mean(ot * ot, axis=0, keepdims=True) + RMS_EPS)
        o_ref[...] = ((ot.T * g_ref[...]) * (1.0 - lam_init)).astype(o_ref.dtype)

    def pipelined(scores, consume):
        pending = [scores(c) for c in range(min(DIFF_AHEAD, len(chunks)))]
        for c in range(len(chunks)):
            s = pending.pop(0)
            if c + DIFF_AHEAD < len(chunks):
                pending.append(scores(c + DIFF_AHEAD))
            consume(c, s)

    state = {}

    def fast_scores(c):
        k0, tk = chunks[c]
        return lax.dot_general(k_scr[k0:k0 + tk, :], q_aug, _NT, preferred_element_type=F32)

    def fast_consume(c, s):
        k0, tk = chunks[c]
        pv = jnp.dot(vt_scr[:, k0:k0 + tk], jnp.exp2(s).astype(BF16), preferred_element_type=F32)
        state["acc"] = pv if c == 0 else state["acc"] + pv

    pipelined(fast_scores, fast_consume)
    acc = state["acc"]
    finish(acc)

    @pl.when(jnp.logical_not(jnp.min(acc[hd:hd + 1, :]) > DIFF_L_MIN))
    def _():
        qab = q_aug[:, :hd]
        st = {}

        def safe_scores(c):
            k0, tk = chunks[c]
            return lax.dot_general(k_scr[k0:k0 + tk, :hd], qab, _NT, preferred_element_type=F32)

        def safe_consume(c, s):
            k0, tk = chunks[c]
            mc = s.max(axis=0, keepdims=True)
            m_new = mc if c == 0 else jnp.maximum(st["m"], mc)
            pv = jnp.dot(vt_scr[:, k0:k0 + tk], jnp.exp2(s - m_new).astype(BF16), preferred_element_type=F32)
            st["acc"] = pv if c == 0 else st["acc"] * jnp.exp2(st["m"] - m_new) + pv
            st["m"] = m_new

        pipelined(safe_scores, safe_consume)
        finish(st["acc"])


def _diff_lat(p_lat, p_ctx, tabs, lam_p, sub_g, bsz, seq, ctx_len, heads, lam_init):
    tq = DIFF_TQ
    nq = seq // tq
    hd = HEAD_DIM
    nk = seq + ctx_len
    cosq, sinq, cos, sin = tabs
    return pl.pallas_call(
        functools.partial(_diff_kernel, n_lat=seq, rope=True, lam_init=lam_init),
        out_shape=jax.ShapeDtypeStruct((bsz * seq, heads * hd), BF16),
        grid=(bsz, heads, nq),
        in_specs=[
            pl.BlockSpec((tq, hd), lambda b, h, i: (b * nq + i, h)),
            pl.BlockSpec((seq, hd), lambda b, h, i: (b, heads + h)),
            pl.BlockSpec((seq, hd), lambda b, h, i: (b, 2 * heads + h)),
            pl.BlockSpec((ctx_len, hd), lambda b, h, i: (b, heads + h)),
            pl.BlockSpec((ctx_len, hd), lambda b, h, i: (b, 2 * heads + h)),
            pl.BlockSpec((tq, hd), lambda b, h, i: (i, 0)),
            pl.BlockSpec((tq, hd), lambda b, h, i: (i, 0)),
            pl.BlockSpec((seq, hd), lambda b, h, i: (0, 0)),
            pl.BlockSpec((seq, hd), lambda b, h, i: (0, 0)),
            pl.BlockSpec((4, hd // 2), lambda b, h, i: (0, 0)),
            pl.BlockSpec((1, hd), lambda b, h, i: (0, 0)),
        ],
        out_specs=pl.BlockSpec((tq, hd), lambda b, h, i: (b * nq + i, h)),
        scratch_shapes=[pltpu.VMEM((nk, 2 * hd), BF16), pltpu.VMEM((hd + SUM_ROWS, nk), BF16), pltpu.VMEM((8, hd), F32)],
        compiler_params=_params("parallel", "parallel", "arbitrary"),
        name="diff_lat",
    )(p_lat, p_lat, p_lat, p_ctx, p_ctx, cosq, sinq, cos, sin, lam_p, sub_g.reshape(1, hd))


def _diff_ctx(p_ctx, lam_p, sub_g, bsz, ctx_len, heads, lam_init):
    hd = HEAD_DIM
    return pl.pallas_call(
        functools.partial(_diff_kernel, n_lat=0, rope=False, lam_init=lam_init),
        out_shape=jax.ShapeDtypeStruct((bsz * ctx_len, heads * hd), BF16),
        grid=(bsz, heads, 1),
        in_specs=[
            pl.BlockSpec((ctx_len, hd), lambda b, h, i: (b, h)),
            pl.BlockSpec((ctx_len, hd), lambda b, h, i: (b, heads + h)),
            pl.BlockSpec((ctx_len, hd), lambda b, h, i: (b, 2 * heads + h)),
            pl.BlockSpec((4, hd // 2), lambda b, h, i: (0, 0)),
            pl.BlockSpec((1, hd), lambda b, h, i: (0, 0)),
        ],
        out_specs=pl.BlockSpec((ctx_len, hd), lambda b, h, i: (b, h)),
        scratch_shapes=[pltpu.VMEM((ctx_len, 2 * hd), BF16), pltpu.VMEM((hd + SUM_ROWS, ctx_len), BF16),
                        pltpu.VMEM((8, hd), F32)],
        compiler_params=_params("parallel", "parallel", "arbitrary"),
        name="diff_ctx",
    )(p_ctx, p_ctx, p_ctx, lam_p, sub_g.reshape(1, hd))


def _rope_tables(seq):
    axis = HEAD_DIM // 4
    t = jnp.arange(seq)
    row = (t // GRID_W).astype(F32)
    col = (t % GRID_W).astype(F32)
    inv = ROPE_THETA ** (-jnp.arange(0, axis, 2, dtype=F32) / axis)
    ang_r = row[:, None] * inv
    ang_c = col[:, None] * inv
    cr, sr, cc, sc = jnp.cos(ang_r), jnp.sin(ang_r), jnp.cos(ang_c), jnp.sin(ang_c)
    cos = jnp.concatenate([cr, cr, cc, cc] * 2, axis=-1)
    sin = jnp.concatenate([-sr, sr, -sc, sc] * 2, axis=-1)
    return cos * _Q_SCALE, sin * _Q_SCALE, cos, sin


def _fnet_kernel(f_ref, cn_ref, sn_ref, cc_ref, sc_ref, o_ref, a_scr, b_scr, *, norm):
    @pl.when(pl.program_id(1) == 0)
    def _():
        step = 512 if f_ref.shape[0] % 512 == 0 else f_ref.shape[0]
        for c in range(f_ref.shape[0] // step):
            rs = slice(c * step, (c + 1) * step)
            f = f_ref[rs, :]
            a_scr[rs, :] = jnp.dot(f, cc_ref[...], preferred_element_type=F32).astype(BF16)
            b_scr[rs, :] = jnp.dot(f, sc_ref[...], preferred_element_type=F32).astype(BF16)

    y = jnp.dot(cn_ref[...], a_scr[...], preferred_element_type=F32)
    y = y - jnp.dot(sn_ref[...], b_scr[...], preferred_element_type=F32)
    o_ref[...] = (y * norm).astype(o_ref.dtype)


def _fnet(p, cn, sn, ccb, scb, bsz, n, fw, col):
    tr = _tile(n, 512)
    norm = 1.0 / math.sqrt(n * HEAD_DIM)
    return pl.pallas_call(
        functools.partial(_fnet_kernel, norm=norm),
        out_shape=jax.ShapeDtypeStruct((bsz * n, fw), BF16),
        grid=(bsz, n // tr),
        in_specs=[
            pl.BlockSpec((n, fw), lambda b, i: (b, col)),
            pl.BlockSpec((tr, n), lambda b, i: (i, 0)),
            pl.BlockSpec((tr, n), lambda b, i: (i, 0)),
            pl.BlockSpec((fw, fw), lambda b, i: (0, 0)),
            pl.BlockSpec((fw, fw), lambda b, i: (0, 0)),
        ],
        out_specs=pl.BlockSpec((tr, fw), lambda b, i: (b * (n // tr) + i, 0)),
        scratch_shapes=[pltpu.VMEM((n, fw), BF16), pltpu.VMEM((n, fw), BF16)],
        compiler_params=_params("parallel", "arbitrary"),
        name="fnet",
    )(p, cn, sn, ccb, scb)


def _dft_tables(n):
    if n <= 256:
        idx = np.outer(np.arange(n), np.arange(n)) % n
        ang = 2.0 * np.pi * idx / n
        return jnp.asarray(np.cos(ang), F32).astype(BF16), jnp.asarray(np.sin(ang), F32).astype(BF16)
    r = int(round(math.sqrt(n)))
    assert r * r == n
    k = np.arange(n)
    t = np.arange(r)
    ang_a = 2.0 * np.pi * (np.outer(t, k) % r) / r
    ang_b = 2.0 * np.pi * (np.outer(t, k) % n) / n
    ca, sa = jnp.asarray(np.cos(ang_a), F32)[:, None, :], jnp.asarray(np.sin(ang_a), F32)[:, None, :]
    cb, sb = jnp.asarray(np.cos(ang_b), F32)[None, :, :], jnp.asarray(np.sin(ang_b), F32)[None, :, :]
    cos = (ca * cb - sa * sb).reshape(n, n).astype(BF16)
    sin = (sa * cb + ca * sb).reshape(n, n).astype(BF16)
    return cos, sin


def _channel_dft_tables(fw):
    idx = np.outer(np.arange(HEAD_DIM), np.arange(HEAD_DIM)) % HEAD_DIM
    ang = 2.0 * np.pi * idx / HEAD_DIM
    eye = np.eye(fw // HEAD_DIM)
    return (jnp.asarray(np.kron(eye, np.cos(ang)), F32).astype(BF16),
            jnp.asarray(np.kron(eye, np.sin(ang)), F32).astype(BF16))


def kernel(x, c, ctx, c_ctx, ada_w, ada_b, norm1_g, norm2_g, w_in, w_out, sgu_norm_g, sgu_w, sgu_b, na_rpb,
           diff_lq1, diff_lk1, diff_lq2, diff_lk2, diff_subln_g, mlp_w1, mlp_w2, final_g):
    bsz, seq, d = x.shape
    ctx_len = ctx.shape[1]
    depth = ada_w.shape[0]
    rows = seq // GRID_W
    mix_heads = d // HEAD_DIM
    sgu_w_dim = (mix_heads // 2) * HEAD_DIM
    na_w = d - sgu_w_dim
    diff_heads = (mix_heads * 3) // 4
    fnet_w = d - diff_heads * HEAD_DIM
    assert sgu_w_dim == na_w and bsz + 1 <= MOD_ROWS

    cond = jnp.zeros((MOD_ROWS, d), F32).at[:bsz].set(c).at[bsz].set(c_ctx)
    mods = _ada_table(cond, ada_w, ada_b).reshape(depth, MOD_ROWS, 1, ADA_CHUNKS * d)

    w_out_b = w_out.astype(BF16)
    sgu_w_b = sgu_w.astype(BF16)
    n1g = norm1_g.reshape(depth, 1, d)
    n2g = norm2_g.reshape(depth, 1, d)
    sgu_g = sgu_norm_g.reshape(-1, 1, sgu_w_dim)
    sgu_b_full = jnp.broadcast_to(sgu_b[..., None], sgu_b.shape + (HEAD_DIM,)).astype(F32)

    tm_lat = _tile(seq, 512)
    tm_ctx = _tile(bsz * ctx_len, 512)
    lat_row = lambda i: (i * tm_lat) // seq
    ctx_row = lambda i: bsz

    x_lat = x.reshape(bsz * seq, d)
    x_ctx = ctx.reshape(bsz * ctx_len, d)

    if depth > 1:
        rope_tabs = _rope_tables(seq)
        cn, sn = _dft_tables(seq)
        cn_c, sn_c = _dft_tables(ctx_len)
        ccb, scb = _channel_dft_tables(fnet_w)

    for l in range(depth):
        need_ctx = l < depth - 1
        p_ctx, w_in_l = _in_proj_ctx(x_ctx, mods, n1g, w_in, l, bsz)
        p_lat = _in_proj(x_lat, mods, n1g, w_in_l, l, lat_row, tm_lat)
        i = l // 2
        ma_ctx = mb_ctx = None
        if l % 2 == 0:
            bias = _na_bias(na_rpb[i], rows)
            ma_lat = _sgu(p_lat, sgu_g, sgu_w_b, sgu_b_full, i, sgu_w_dim)
            mb_lat = _na_lat(p_lat, p_ctx, bias, bsz, seq, ctx_len, na_w, 2)
            if need_ctx:
                ma_ctx = _sgu(p_ctx, sgu_g, sgu_w_b, sgu_b_full, i, sgu_w_dim)
                mb_ctx = _na_ctx(p_ctx, bsz, ctx_len, na_w, 2)
        else:
            lam_init = 0.8 - 0.6 * math.exp(-0.3 * l)
            lam_p = jnp.stack([diff_lq1[i], diff_lk1[i], diff_lq2[i], diff_lk2[i]]).astype(F32)
            fcol = (3 * diff_heads * HEAD_DIM) // fnet_w
            ma_lat = _diff_lat(p_lat, p_ctx, rope_tabs, lam_p, diff_subln_g[i], bsz, seq, ctx_len, diff_heads, lam_init)
            mb_lat = _fnet(p_lat, cn, sn, ccb, scb, bsz, seq, fnet_w, fcol)
            if need_ctx:
                ma_ctx = _diff_ctx(p_ctx, lam_p, diff_subln_g[i], bsz, ctx_len, diff_heads, lam_init)
                mb_ctx = _fnet(p_ctx, cn_c, sn_c, ccb, scb, bsz, ctx_len, fnet_w, fcol)
        if need_ctx:
            x_ctx = _out_proj(x_ctx, ma_ctx, mb_ctx, w_out_b, mods, l, ctx_row, tm_ctx)
            x_ctx, w1_l, w2_l = _mlp_ctx(x_ctx, mods, n2g, mlp_w1, mlp_w2, l, bsz)
        else:
            w1_l, w2_l = mlp_w1[l].astype(BF16), mlp_w2[l].astype(BF16)
        x_lat = _out_proj(x_lat, ma_lat, mb_lat, w_out_b, mods, l, lat_row, tm_lat)
        x_lat = _mlp(x_lat, mods, n2g, w1_l, w2_l, final_g, l, lat_row, tm_lat, final_norm=not need_ctx)
    return x_lat.reshape(bsz, seq, d)
```

```python
import functools
import math

import numpy as np
import jax
import jax.numpy as jnp
from jax import lax
from jax.experimental import pallas as pl
from jax.experimental.pallas import tpu as pltpu

F32 = jnp.float32
BF16 = jnp.bfloat16

GRID_W = 64
HEAD_DIM = 128
CHUNK = 128
NA_KH = 8
NA_KW = 16
ROPE_THETA = 10000.0
Q_ROWS = 4
K_ROWS = Q_ROWS + NA_KH
RMS_EPS = 1e-6
LN_EPS = 1e-5
ADA_CHUNKS = 6
MOD_ROWS = 16

VMEM_LIMIT = 56 * 1024 * 1024

_NT = (((1,), (1,)), ((), ()))

_Q_SCALE = (HEAD_DIM // 2) ** -0.5 * math.log2(math.e)
DIFF_TQ = 1024
DIFF_TK = 1024
DIFF_AHEAD = 1
DIFF_BOUND_MARGIN = 1.01
DIFF_L_MIN = 2.0 ** -100
SUM_ROWS = 16


def _params(*sem):
    return pltpu.CompilerParams(dimension_semantics=sem, vmem_limit_bytes=VMEM_LIMIT)


def _tile(n, pref):
    if n <= pref:
        return n
    t = (pref // 128) * 128
    while n % t:
        t -= 128
    assert t > 0, (n, pref)
    return t


def _ada_kernel(s_ref, w_ref, b_ref, o_ref):
    s = s_ref[...]
    a = (s * jax.nn.sigmoid(s)).astype(BF16)
    o_ref[...] = jnp.dot(a, w_ref[...].astype(BF16), preferred_element_type=F32) + b_ref[...]


def _ada_table(cond, ada_w, ada_b):
    depth, d, n = ada_w.shape
    tn = _tile(n, 1024)
    return pl.pallas_call(
        _ada_kernel,
        out_shape=jax.ShapeDtypeStruct((depth, MOD_ROWS, n), F32),
        grid=(depth, n // tn),
        in_specs=[
            pl.BlockSpec((MOD_ROWS, d), lambda l, j: (0, 0)),
            pl.BlockSpec((None, d, tn), lambda l, j: (l, 0, j)),
            pl.BlockSpec((None, 1, tn), lambda l, j: (l, 0, j)),
        ],
        out_specs=pl.BlockSpec((None, MOD_ROWS, tn), lambda l, j: (l, 0, j)),
        compiler_params=_params("parallel", "parallel"),
        name="ada_table",
    )(cond, ada_w, ada_b.reshape(depth, 1, n))


def _norm_mod_rows(x_ref, g_ref, sh_ref, sc_ref, h_ref, unrolled=False, part=None):
    rows = 64
    g = g_ref[...]
    sc1 = 1.0 + sc_ref[...]
    sh = sh_ref[...]
    total = x_ref.shape[0] // rows
    if part is not None:
        assert unrolled and total % part[1] == 0
        total //= part[1]

    def body(r, carry):
        if part is not None:
            sl = pl.ds(pl.multiple_of((part[0] * total + r) * rows, rows), rows)
        elif unrolled:
            sl = slice(r * rows, (r + 1) * rows)
        else:
            sl = pl.ds(pl.multiple_of(r * rows, rows), rows)
        xv = x_ref[sl, :]
        ms = jnp.mean(xv * xv, axis=-1, keepdims=True)
        y = xv * lax.rsqrt(ms + RMS_EPS) * g
        h_ref[sl, :] = (y * sc1 + sh).astype(h_ref.dtype)
        return carry

    if unrolled:
        for r in range(total):
            body(r, 0)
    else:
        lax.fori_loop(0, total, body, 0)


def _mod_spec(d, layer, chunk, row_fn):
    return pl.BlockSpec((None, None, 1, d), lambda i, j: (layer, row_fn(i), 0, chunk))


def _layer_vec_spec(d, layer):
    return pl.BlockSpec((None, 1, d), lambda i, j: (layer, 0, 0))


def _in_kernel(x0_ref, xn_ref, g_ref, sh0_ref, sc0_ref, shn_ref, scn_ref, w_ref, o_ref, h0_scr, h1_scr):
    i = pl.program_id(0)

    @pl.when(i == 0)
    def _():
        _norm_mod_rows(x0_ref, g_ref, sh0_ref, sc0_ref, h0_scr)

    def step(h_cur, h_next):
        o_ref[...] = jnp.dot(h_cur[...], w_ref[...], preferred_element_type=F32).astype(o_ref.dtype)
        _norm_mod_rows(xn_ref, g_ref, shn_ref, scn_ref, h_next, unrolled=True)

    @pl.when(i % 2 == 0)
    def _():
        step(h0_scr, h1_scr)

    @pl.when(i % 2 == 1)
    def _():
        step(h1_scr, h0_scr)


def _in_proj(x, mods, g, w, layer, row_fn, tm):
    m, d = x.shape
    n = w.shape[1]
    nt = m // tm
    nxt = lambda i: jnp.minimum(i + 1, nt - 1)
    once = pl.Buffered(1)
    return pl.pallas_call(
        _in_kernel,
        out_shape=jax.ShapeDtypeStruct((m, n), BF16),
        grid=(nt, 1),
        in_specs=[
            pl.BlockSpec((tm, d), lambda i, j: (0, 0), pipeline_mode=once),
            pl.BlockSpec((tm, d), lambda i, j: (nxt(i), 0)),
            _layer_vec_spec(d, layer),
            _mod_spec(d, layer, 0, lambda i: row_fn(0)),
            _mod_spec(d, layer, 1, lambda i: row_fn(0)),
            _mod_spec(d, layer, 0, lambda i: row_fn(nxt(i))),
            _mod_spec(d, layer, 1, lambda i: row_fn(nxt(i))),
            pl.BlockSpec((d, n), lambda i, j: (0, 0), pipeline_mode=once),
        ],
        out_specs=pl.BlockSpec((tm, n), lambda i, j: (i, 0)),
        scratch_shapes=[pltpu.VMEM((tm, d), BF16), pltpu.VMEM((tm, d), BF16)],
        compiler_params=_params("arbitrary", "arbitrary"),
        name="in_proj",
    )(x, x, g, mods, mods, mods, mods, w)


def _out_kernel(x_ref, ma_ref, mb_ref, wa_ref, wb_ref, gate_ref, o_ref):
    y = jnp.dot(ma_ref[...], wa_ref[...], preferred_element_type=F32)
    y = y + jnp.dot(mb_ref[...], wb_ref[...], preferred_element_type=F32)
    o_ref[...] = x_ref[...] + gate_ref[...] * y


def _out_proj(x, ma, mb, w, mods, layer, row_fn, tm):
    m, d = x.shape
    ka, kb = ma.shape[1], mb.shape[1]
    assert ka % kb == 0 and w.shape[1:] == (ka + kb, d)
    return pl.pallas_call(
        _out_kernel,
        out_shape=jax.ShapeDtypeStruct((m, d), F32),
        grid=(m // tm, 1),
        in_specs=[
            pl.BlockSpec((tm, d), lambda i, j: (i, 0)),
            pl.BlockSpec((tm, ka), lambda i, j: (i, 0)),
            pl.BlockSpec((tm, kb), lambda i, j: (i, 0)),
            pl.BlockSpec((None, ka, d), lambda i, j: (layer, 0, 0)),
            pl.BlockSpec((None, kb, d), lambda i, j: (layer, ka // kb, 0)),
            _mod_spec(d, layer, 2, row_fn),
        ],
        out_specs=pl.BlockSpec((tm, d), lambda i, j: (i, 0)),
        compiler_params=_params("parallel", "arbitrary"),
        name="out_proj",
    )(x, ma, mb, w, w, mods)


def _mlp_kernel(x_ref, x0_ref, xn_ref, g_ref, sh0_ref, sc0_ref, shn_ref, scn_ref, gate_ref, w1_ref, w2_ref, fg_ref,
                o_ref, h_scr, *, final_norm, n_parts):
    i = pl.program_id(0)
    f = pl.program_id(1)
    nf = n_parts
    slot = i % 2

    @pl.when((i == 0) & (f == 0))
    def _():
        _norm_mod_rows(x0_ref, g_ref, sh0_ref, sc0_ref, h_scr.at[0])

    def step(first, last):
        a = jnp.dot(h_scr[slot], w1_ref[...], preferred_element_type=F32)
        a = jnp.square(jnp.maximum(a, 0.0)).astype(BF16)
        y = jnp.dot(a, w2_ref[...], preferred_element_type=F32)
        if not first:
            y = o_ref[...] + y
        if last:
            y = x_ref[...] + gate_ref[...] * y
            if final_norm:
                y = y * lax.rsqrt(jnp.mean(y * y, axis=-1, keepdims=True) + RMS_EPS) * fg_ref[...]
        o_ref[...] = y
        _norm_mod_rows(xn_ref, g_ref, shn_ref, scn_ref, h_scr.at[1 - slot], unrolled=True, part=(f, n_parts))

    pl.when(f == 0)(lambda: step(True, False))
    pl.when((f > 0) & (f < nf - 1))(lambda: step(False, False))
    pl.when(f == nf - 1)(lambda: step(False, True))


def _mlp(x, mods, g, w1, w2, final_g, layer, row_fn, tm, final_norm):
    m, d = x.shape
    dff = w1.shape[1]
    tf = _tile(dff, 1024)
    nt = m // tm
    assert dff // tf >= 2
    nxt = lambda i: jnp.minimum(i + 1, nt - 1)
    return pl.pallas_call(
        functools.partial(_mlp_kernel, final_norm=final_norm, n_parts=dff // tf),
        out_shape=jax.ShapeDtypeStruct((m, d), F32),
        grid=(nt, dff // tf),
        in_specs=[
            pl.BlockSpec((tm, d), lambda i, j: (i, 0)),
            pl.BlockSpec((tm, d), lambda i, j: (0, 0), pipeline_mode=pl.Buffered(1)),
            pl.BlockSpec((tm, d), lambda i, j: (nxt(i), 0)),
            _layer_vec_spec(d, layer),
            _mod_spec(d, layer, 3, lambda i: row_fn(0)),
            _mod_spec(d, layer, 4, lambda i: row_fn(0)),
            _mod_spec(d, layer, 3, lambda i: row_fn(nxt(i))),
            _mod_spec(d, layer, 4, lambda i: row_fn(nxt(i))),
            _mod_spec(d, layer, 5, row_fn),
            pl.BlockSpec((d, tf), lambda i, j: (0, j)),
            pl.BlockSpec((tf, d), lambda i, j: (j, 0)),
            pl.BlockSpec((1, d), lambda i, j: (0, 0)),
        ],
        out_specs=pl.BlockSpec((tm, d), lambda i, j: (i, 0)),
        scratch_shapes=[pltpu.VMEM((2, tm, d), BF16)],
        compiler_params=_params("arbitrary", "arbitrary"),
        name="mlp",
    )(x, x, x, g, mods, mods, mods, mods, mods, w1, w2, final_g.reshape(1, d))


def _in_ctx_kernel(x_ref, g_ref, sh_ref, sc_ref, w_ref, o_ref, wb_ref, h_scr):
    @pl.when(pl.program_id(0) == 0)
    def _():
        _norm_mod_rows(x_ref, g_ref, sh_ref, sc_ref, h_scr)

    wb = w_ref[...].astype(BF16)
    wb_ref[...] = wb
    o_ref[...] = jnp.dot(h_scr[...], wb, preferred_element_type=F32).astype(o_ref.dtype)


def _ctx_vec_specs(d, layer, row, chunks):
    specs = [pl.BlockSpec((None, 1, d), lambda j: (layer, 0, 0))]
    for chunk in chunks:
        specs.append(pl.BlockSpec((None, None, 1, d), lambda j, chunk=chunk: (layer, row, 0, chunk)))
    return specs


def _in_proj_ctx(x, mods, g, w, layer, row):
    m, d = x.shape
    n = w.shape[2]
    tn = _tile(n, 1024)
    return pl.pallas_call(
        _in_ctx_kernel,
        out_shape=(jax.ShapeDtypeStruct((m, n), BF16), jax.ShapeDtypeStruct((d, n), BF16)),
        grid=(n // tn,),
        in_specs=[pl.BlockSpec((m, d), lambda j: (0, 0), pipeline_mode=pl.Buffered(1))]
        + _ctx_vec_specs(d, layer, row, (0, 1))
        + [pl.BlockSpec((None, d, tn), lambda j: (layer, 0, j))],
        out_specs=(pl.BlockSpec((m, tn), lambda j: (0, j)), pl.BlockSpec((d, tn), lambda j: (0, j))),
        scratch_shapes=[pltpu.VMEM((m, d), BF16)],
        compiler_params=_params("arbitrary"),
        name="in_proj_ctx",
    )(x, g, mods, mods, w)


def _mlp_ctx_kernel(x_ref, g_ref, sh_ref, sc_ref, gate_ref, w1_ref, w2_ref, o_ref, w1b_ref, w2b_ref, h_scr, *, n_parts):
    f = pl.program_id(0)

    @pl.when(f == 0)
    def _():
        _norm_mod_rows(x_ref, g_ref, sh_ref, sc_ref, h_scr)

    def step(first, last):
        w1b = w1_ref[...].astype(BF16)
        w2b = w2_ref[...].astype(BF16)
        w1b_ref[...] = w1b
        w2b_ref[...] = w2b
        a = jnp.dot(h_scr[...], w1b, preferred_element_type=F32)
        a = jnp.square(jnp.maximum(a, 0.0)).astype(BF16)
        y = jnp.dot(a, w2b, preferred_element_type=F32)
        if not first:
            y = o_ref[...] + y
        if last:
            y = x_ref[...] + gate_ref[...] * y
        o_ref[...] = y

    pl.when(f == 0)(lambda: step(True, False))
    pl.when((f > 0) & (f < n_parts - 1))(lambda: step(False, False))
    pl.when(f == n_parts - 1)(lambda: step(False, True))


def _mlp_ctx(x, mods, g, w1, w2, layer, row):
    m, d = x.shape
    dff = w1.shape[2]
    tf = _tile(dff, 512)
    assert dff // tf >= 2
    return pl.pallas_call(
        functools.partial(_mlp_ctx_kernel, n_parts=dff // tf),
        out_shape=(jax.ShapeDtypeStruct((m, d), F32), jax.ShapeDtypeStruct((d, dff), BF16),
                   jax.ShapeDtypeStruct((dff, d), BF16)),
        grid=(dff // tf,),
        in_specs=[pl.BlockSpec((m, d), lambda j: (0, 0), pipeline_mode=pl.Buffered(1))]
        + _ctx_vec_specs(d, layer, row, (3, 4, 5))
        + [pl.BlockSpec((None, d, tf), lambda j: (layer, 0, j)), pl.BlockSpec((None, tf, d), lambda j: (layer, j, 0))],
        out_specs=(pl.BlockSpec((m, d), lambda j: (0, 0), pipeline_mode=pl.Buffered(1)), pl.BlockSpec((d, tf), lambda j: (0, j)),
                   pl.BlockSpec((tf, d), lambda j: (j, 0))),
        scratch_shapes=[pltpu.VMEM((m, d), BF16)],
        compiler_params=_params("arbitrary"),
        name="mlp_ctx",
    )(x, g, mods, mods, mods, w1, w2)


def _sgu_kernel(u_ref, v_ref, gam_ref, ws_ref, b_ref, o_ref, *, groups):
    for c in range(u_ref.shape[0] // CHUNK):
        rs = slice(c * CHUNK, (c + 1) * CHUNK)
        for g in range(groups):
            cs = slice(g * HEAD_DIM, (g + 1) * HEAD_DIM)
            v = v_ref[rs, cs].astype(F32)
            mu = jnp.mean(v, axis=-1, keepdims=True)
            dv = v - mu
            var = jnp.mean(dv * dv, axis=-1, keepdims=True)
            vn = (dv * lax.rsqrt(var + LN_EPS) * gam_ref[:, cs]).astype(BF16)
            mixed = jnp.dot(ws_ref[g], vn, preferred_element_type=F32) + b_ref[g]
            o_ref[rs, cs] = (u_ref[rs, cs].astype(F32) * mixed).astype(o_ref.dtype)


def _sgu(p, gam, ws, b_full, layer, sgu_w):
    m = p.shape[0]
    groups = sgu_w // HEAD_DIM
    tr = _tile(m, 512)
    return pl.pallas_call(
        functools.partial(_sgu_kernel, groups=groups),
        out_shape=jax.ShapeDtypeStruct((m, sgu_w), BF16),
        grid=(m // tr,),
        in_specs=[
            pl.BlockSpec((tr, sgu_w), lambda i: (i, 0)),
            pl.BlockSpec((tr, sgu_w), lambda i: (i, 1)),
            pl.BlockSpec((None, 1, sgu_w), lambda i: (layer, 0, 0)),
            pl.BlockSpec((None, groups, CHUNK, CHUNK), lambda i: (layer, 0, 0, 0)),
            pl.BlockSpec((None, groups, CHUNK, HEAD_DIM), lambda i: (layer, 0, 0, 0)),
        ],
        out_specs=pl.BlockSpec((tr, sgu_w), lambda i: (i, 0)),
        compiler_params=_params("parallel"),
        name="sgu",
    )(p, p, gam, ws, b_full)


def _softmax_pv(scores, values):
    d = values[0].shape[1]
    m = scores[0].max(axis=-1, keepdims=True)
    for s in scores[1:]:
        m = jnp.maximum(m, s.max(axis=-1, keepdims=True))
    o = None
    for s, v in zip(scores, values):
        v1 = jnp.concatenate([v, jnp.ones(v.shape, v.dtype)], axis=1)
        pv = jnp.dot(jnp.exp2(s - m).astype(BF16), v1, preferred_element_type=F32)
        o = pv if o is None else o + pv
    return o[:, :d] / o[:, d:]


_NA_SCALE = HEAD_DIM ** -0.5 * math.log2(math.e)


def _na_kernel(q_ref, k0_ref, k1_ref, k2_ref, v0_ref, v1_ref, v2_ref, kc_ref, vc_ref, bias_ref, o_ref, *, heads):
    qb = q_ref.shape[0]
    scale = _NA_SCALE
    for h in range(heads):
        cs = slice(h * HEAD_DIM, (h + 1) * HEAD_DIM)
        q = (q_ref[:, cs].astype(F32) * scale).astype(BF16)
        scores, values = [], []
        for j, (k_ref, v_ref) in enumerate(((k0_ref, v0_ref), (k1_ref, v1_ref), (k2_ref, v2_ref))):
            s = lax.dot_general(q, k_ref[:, cs], _NT, preferred_element_type=F32)
            scores.append(s + bias_ref[h, :, j * qb:(j + 1) * qb])
            values.append(v_ref[:, cs])
        scores.append(lax.dot_general(q, kc_ref[:, cs], _NT, preferred_element_type=F32))
        values.append(vc_ref[:, cs])
        o_ref[:, cs] = _softmax_pv(scores, values).astype(o_ref.dtype)


def _na_bias(rpb, rows):
    heads = rpb.shape[0]
    kw = NA_KW
    col = jnp.arange(GRID_W)
    col_start = jnp.clip(col - kw // 2, 0, GRID_W - kw)
    col_mask = (col[None, :] >= col_start[:, None]) & (col[None, :] < col_start[:, None] + kw)
    dx = jnp.clip(col[None, :] - col[:, None], -(kw - 1), kw - 1) + (kw - 1)
    rpb_x = jnp.where(col_mask[None, None], rpb.astype(F32)[:, :, dx] * math.log2(math.e), -jnp.inf)
    outside = jnp.full((heads, GRID_W, GRID_W), -jnp.inf, F32)
    nblk = rows // Q_ROWS
    blocks = []
    for kblk in (0, 1, nblk - 1):
        ks = min(max(Q_ROWS * kblk - NA_KH // 2, 0), rows - K_ROWS)
        qrows = []
        for i in range(Q_ROWS):
            r = Q_ROWS * kblk + i
            rs = min(max(r - NA_KH // 2, 0), rows - NA_KH)
            tiles = [rpb_x[:, ks + j - r + NA_KH - 1] if rs <= ks + j < rs + NA_KH else outside for j in range(K_ROWS)]
            qrows.append(jnp.concatenate(tiles, axis=-1))
        blocks.append(jnp.concatenate(qrows, axis=1))
    return jnp.stack(blocks)


def _na_lat(p_lat, p_ctx, bias, bsz, seq, ctx_len, na_w, col0):
    heads = na_w // HEAD_DIM
    qb = Q_ROWS * GRID_W
    nblk = seq // qb
    assert nblk >= 4 and K_ROWS == 3 * Q_ROWS

    def kblock(j):
        return lambda b, k: (b * nblk + jnp.clip(k - 1, 0, nblk - 3) + j, col0 + 1)

    def vblock(j):
        return lambda b, k: (b * nblk + jnp.clip(k - 1, 0, nblk - 3) + j, col0 + 2)

    def btype(b, k):
        return (jnp.where(k == 0, 0, jnp.where(k == nblk - 1, 2, 1)), 0, 0, 0)

    return pl.pallas_call(
        functools.partial(_na_kernel, heads=heads),
        out_shape=jax.ShapeDtypeStruct((bsz * seq, na_w), BF16),
        grid=(bsz, nblk),
        in_specs=[
            pl.BlockSpec((qb, na_w), lambda b, k: (b * nblk + k, col0)),
            pl.BlockSpec((qb, na_w), kblock(0)),
            pl.BlockSpec((qb, na_w), kblock(1)),
            pl.BlockSpec((qb, na_w), kblock(2)),
            pl.BlockSpec((qb, na_w), vblock(0)),
            pl.BlockSpec((qb, na_w), vblock(1)),
            pl.BlockSpec((qb, na_w), vblock(2)),
            pl.BlockSpec((ctx_len, na_w), lambda b, k: (b, col0 + 1)),
            pl.BlockSpec((ctx_len, na_w), lambda b, k: (b, col0 + 2)),
            pl.BlockSpec((None, heads, qb, K_ROWS * GRID_W), btype),
        ],
        out_specs=pl.BlockSpec((qb, na_w), lambda b, k: (b * nblk + k, 0)),
        compiler_params=_params("parallel", "arbitrary"),
        name="na_lat",
    )(p_lat, p_lat, p_lat, p_lat, p_lat, p_lat, p_lat, p_ctx, p_ctx, bias)


def _attn_ctx_kernel(q_ref, k_ref, v_ref, o_ref, *, heads):
    for h in range(heads):
        cs = slice(h * HEAD_DIM, (h + 1) * HEAD_DIM)
        q = (q_ref[:, cs].astype(F32) * _NA_SCALE).astype(BF16)
        s = lax.dot_general(q, k_ref[:, cs], _NT, preferred_element_type=F32)
        o_ref[:, cs] = _softmax_pv([s], [v_ref[:, cs]]).astype(o_ref.dtype)


def _na_ctx(p_ctx, bsz, ctx_len, na_w, col0):
    heads = na_w // HEAD_DIM
    return pl.pallas_call(
        functools.partial(_attn_ctx_kernel, heads=heads),
        out_shape=jax.ShapeDtypeStruct((bsz * ctx_len, na_w), BF16),
        grid=(bsz,),
        in_specs=[pl.BlockSpec((ctx_len, na_w), lambda b, c=c: (b, col0 + c)) for c in range(3)],
        out_specs=pl.BlockSpec((ctx_len, na_w), lambda b: (b, 0)),
        compiler_params=_params("parallel"),
        name="na_ctx",
    )(p_ctx, p_ctx, p_ctx)


def _swap16(x):
    lane = lax.broadcasted_iota(jnp.int32, x.shape, 1)
    up = pltpu.roll(x, HEAD_DIM - 16, 1)
    down = pltpu.roll(x, 16, 1)
    return jnp.where((lane & 16) == 0, up, down)


def _diff_kernel(*refs, n_lat, rope, lam_init):
    if rope:
        (q_ref, k_ref, v_ref, kc_ref, vc_ref, cosq_ref, sinq_ref, cos_ref, sin_ref, lam_ref, g_ref,
         o_ref, k_scr, vt_scr, kmax_scr) = refs
    else:
        q_ref, kc_ref, vc_ref, lam_ref, g_ref, o_ref, k_scr, vt_scr, kmax_scr = refs
    n_ctx = kc_ref.shape[0]
    nk = n_lat + n_ctx
    hd = HEAD_DIM
    half = hd // 2

    @pl.when(pl.program_id(2) == 0)
    def _():
        step = 512
        row = lax.broadcasted_iota(jnp.int32, (hd, hd), 0)
        col = lax.broadcasted_iota(jnp.int32, (hd, hd), 1)
        sel = jnp.where(((col == 0) & (row < half)) | ((col == 1) & (row >= half)), 1.0, 0.0).astype(BF16)

        def put_keys(rs, kr, kmax):
            n = kr.shape[0]
            k_scr[rs, :hd] = kr
            lane = lax.broadcasted_iota(jnp.int32, (n, hd), 1)
            k_scr[rs, hd:] = jnp.where(lane == 0, 1.0, 0.0).astype(BF16)
            kf = kr.astype(F32)
            sq = jnp.dot((kf * kf).astype(BF16), sel, preferred_element_type=F32).max(axis=0, keepdims=True)
            return sq if kmax is None else jnp.maximum(kmax, sq)

        kmax = None
        for c in range(n_lat // step):
            rs = slice(c * step, (c + 1) * step)
            k = k_ref[rs, :].astype(F32)
            kmax = put_keys(rs, (k * cos_ref[rs, :] + _swap16(k) * sin_ref[rs, :]).astype(BF16), kmax)
            vt_scr[:hd, rs] = v_ref[rs, :].astype(F32).T.astype(BF16)
        kmax = put_keys(slice(n_lat, nk), kc_ref[...], kmax)
        vt_scr[:hd, n_lat:nk] = vc_ref[...].astype(F32).T.astype(BF16)
        vt_scr[hd:, :] = jnp.ones((SUM_ROWS, nk), BF16)
        lane1 = lax.broadcasted_iota(jnp.int32, kmax.shape, 1)
        for mp in range(2):
            kmax_scr[mp:mp + 1, :] = jnp.broadcast_to(
                jnp.max(jnp.where(lane1 == mp, kmax, 0.0), axis=-1, keepdims=True), kmax.shape)

    qraw = q_ref[...]
    q = qraw.astype(F32)
    tq = q.shape[0]
    row = lax.broadcasted_iota(jnp.int32, (hd, hd), 0)
    col = lax.broadcasted_iota(jnp.int32, (hd, hd), 1)
    sel1 = jnp.where((col == 0) & (row < half), 1.0, 0.0).astype(BF16)
    sel2 = jnp.where((col == 0) & (row >= half), 1.0, 0.0).astype(BF16)
    sq = (q * q).astype(BF16)
    n1 = jnp.dot(sq, sel1, preferred_element_type=F32)
    n2 = jnp.dot(sq, sel2, preferred_element_type=F32)
    if rope:
        perm = jnp.where(row == (col ^ 16), 1.0, 0.0).astype(BF16)
        q = q * cosq_ref[...] + jnp.dot(qraw, perm, preferred_element_type=F32) * sinq_ref[...]
    else:
        q = q * _Q_SCALE
    lane = lax.broadcasted_iota(jnp.int32, q.shape, 1)
    shift1 = jnp.sqrt(n1 * kmax_scr[0:1, :]) * (-_Q_SCALE * DIFF_BOUND_MARGIN)
    shift2 = jnp.sqrt(n2 * kmax_scr[1:2, :]) * (-_Q_SCALE * DIFF_BOUND_MARGIN)
    q_aug = jnp.concatenate([
        jnp.concatenate([jnp.where(lane < half, q, 0.0), shift1], axis=1),
        jnp.concatenate([jnp.where(lane >= half, q, 0.0), shift2], axis=1),
    ], axis=0).astype(BF16)

    lp = lam_ref[...]
    d1 = jnp.sum(lp[0:1, :] * lp[1:2, :], axis=-1, keepdims=True)
    d2 = jnp.sum(lp[2:3, :] * lp[3:4, :], axis=-1, keepdims=True)
    lam = jnp.exp(d1) - jnp.exp(d2) + lam_init

    chunks = [(k0, min(DIFF_TK, nk - k0)) for k0 in range(0, nk, DIFF_TK)]

    def write_out(ot):
        ot = ot * lax.rsqrt(jnp.mean(ot * ot, axis=0, keepdims=True) + RMS_EPS)
        o_ref[...] = ((ot.T * g_ref[...]) * (1.0 - lam_init)).astype(o_ref.dtype)

    def finish(acc):
        l = acc[hd:hd + 1, :]
        c1 = 1.0 / l[:, :tq]
        c2 = lam / l[:, tq:]
        write_out(acc[:hd, :tq] * c1 - acc[:hd, tq:] * c2)

    def pipelined(scores, consume):
        pending = [scores(c) for c in range(min(DIFF_AHEAD, len(chunks)))]
        for c in range(len(chunks)):
            s = pending.pop(0)
            if c + DIFF_AHEAD < len(chunks):
                pending.append(scores(c + DIFF_AHEAD))
            consume(c, s)

    state = {}

    def fast_scores(c):
        k0, tk = chunks[c]
        return lax.dot_general(k_scr[k0:k0 + tk, :], q_aug, _NT, preferred_element_type=F32)

    def fast_consume(c, s):
        k0, tk = chunks[c]
        pv = jnp.dot(vt_scr[:, k0:k0 + tk], jnp.exp2(s).astype(BF16), preferred_element_type=F32)
        state["acc"] = pv if c == 0 else state["acc"] + pv

    pipelined(fast_scores, fast_consume)
    acc = state["acc"]
    finish(acc)

    @pl.when(jnp.logical_not(jnp.min(acc[hd:hd + 1, :]) > DIFF_L_MIN))
    def _():
        qab = q_aug[:, :hd]
        st = {}

        def safe_scores(c):
            k0, tk = chunks[c]
            return lax.dot_general(k_scr[k0:k0 + tk, :hd], qab, _NT, preferred_element_type=F32)

        def safe_consume(c, s):
            k0, tk = chunks[c]
            mc = s.max(axis=0, keepdims=True)
            m_new = mc if c == 0 else jnp.maximum(st["m"], mc)
            pv = jnp.dot(vt_scr[:, k0:k0 + tk], jnp.exp2(s - m_new).astype(BF16), preferred_element_type=F32)
            st["acc"] = pv if c == 0 else st["acc"] * jnp.exp2(st["m"] - m_new) + pv
            st["m"] = m_new

        pipelined(safe_scores, safe_consume)
        finish(st["acc"])


def _diff_lat(p_lat, p_ctx, tabs, lam_p, sub_g, bsz, seq, ctx_len, heads, lam_init):
    tq = DIFF_TQ
    nq = seq // tq
    hd = HEAD_DIM
    nk = seq + ctx_len
    cosq, sinq, cos, sin = tabs
    return pl.pallas_call(
        functools.partial(_diff_kernel, n_lat=seq, rope=True, lam_init=lam_init),
        out_shape=jax.ShapeDtypeStruct((bsz * seq, heads * hd), BF16),
        grid=(bsz, heads, nq),
        in_specs=[
            pl.BlockSpec((tq, hd), lambda b, h, i: (b * nq + i, h)),
            pl.BlockSpec((seq, hd), lambda b, h, i: (b, heads + h)),
            pl.BlockSpec((seq, hd), lambda b, h, i: (b, 2 * heads + h)),
            pl.BlockSpec((ctx_len, hd), lambda b, h, i: (b, heads + h)),
            pl.BlockSpec((ctx_len, hd), lambda b, h, i: (b, 2 * heads + h)),
            pl.BlockSpec((tq, hd), lambda b, h, i: (i, 0)),
            pl.BlockSpec((tq, hd), lambda b, h, i: (i, 0)),
            pl.BlockSpec((seq, hd), lambda b, h, i: (0, 0), pipeline_mode=pl.Buffered(1)),
            pl.BlockSpec((seq, hd), lambda b, h, i: (0, 0), pipeline_mode=pl.Buffered(1)),
            pl.BlockSpec((4, hd // 2), lambda b, h, i: (0, 0)),
            pl.BlockSpec((1, hd), lambda b, h, i: (0, 0)),
        ],
        out_specs=pl.BlockSpec((tq, hd), lambda b, h, i: (b * nq + i, h)),
        scratch_shapes=[pltpu.VMEM((nk, 2 * hd), BF16), pltpu.VMEM((hd + SUM_ROWS, nk), BF16), pltpu.VMEM((8, hd), F32)],
        compiler_params=_params("parallel", "parallel", "arbitrary"),
        name="diff_lat",
    )(p_lat, p_lat, p_lat, p_ctx, p_ctx, cosq, sinq, cos, sin, lam_p, sub_g.reshape(1, hd))


def _diff_ctx(p_ctx, lam_p, sub_g, bsz, ctx_len, heads, lam_init):
    hd = HEAD_DIM
    return pl.pallas_call(
        functools.partial(_diff_kernel, n_lat=0, rope=False, lam_init=lam_init),
        out_shape=jax.ShapeDtypeStruct((bsz * ctx_len, heads * hd), BF16),
        grid=(bsz, heads, 1),
        in_specs=[
            pl.BlockSpec((ctx_len, hd), lambda b, h, i: (b, h)),
            pl.BlockSpec((ctx_len, hd), lambda b, h, i: (b, heads + h)),
            pl.BlockSpec((ctx_len, hd), lambda b, h, i: (b, 2 * heads + h)),
            pl.BlockSpec((4, hd // 2), lambda b, h, i: (0, 0)),
            pl.BlockSpec((1, hd), lambda b, h, i: (0, 0)),
        ],
        out_specs=pl.BlockSpec((ctx_len, hd), lambda b, h, i: (b, h)),
        scratch_shapes=[pltpu.VMEM((ctx_len, 2 * hd), BF16), pltpu.VMEM((hd + SUM_ROWS, ctx_len), BF16),
                        pltpu.VMEM((8, hd), F32)],
        compiler_params=_params("parallel", "parallel", "arbitrary"),
        name="diff_ctx",
    )(p_ctx, p_ctx, p_ctx, lam_p, sub_g.reshape(1, hd))


def _rope_tables(seq):
    axis = HEAD_DIM // 4
    t = jnp.arange(seq)
    row = (t // GRID_W).astype(F32)
    col = (t % GRID_W).astype(F32)
    inv = ROPE_THETA ** (-jnp.arange(0, axis, 2, dtype=F32) / axis)
    ang_r = row[:, None] * inv
    ang_c = col[:, None] * inv
    cr, sr, cc, sc = jnp.cos(ang_r), jnp.sin(ang_r), jnp.cos(ang_c), jnp.sin(ang_c)
    cos = jnp.concatenate([cr, cr, cc, cc] * 2, axis=-1)
    sin = jnp.concatenate([-sr, sr, -sc, sc] * 2, axis=-1)
    return cos * _Q_SCALE, sin * _Q_SCALE, cos, sin


def _fnet_kernel(f_ref, cn_ref, sn_ref, cc_ref, sc_ref, o_ref, a_scr, b_scr, *, norm):
    @pl.when(pl.program_id(1) == 0)
    def _():
        step = 512 if f_ref.shape[0] % 512 == 0 else f_ref.shape[0]
        for c in range(f_ref.shape[0] // step):
            rs = slice(c * step, (c + 1) * step)
            f = f_ref[rs, :]
            a_scr[rs, :] = jnp.dot(f, cc_ref[...], preferred_element_type=F32).astype(BF16)
            b_scr[rs, :] = jnp.dot(f, sc_ref[...], preferred_element_type=F32).astype(BF16)

    y = jnp.dot(cn_ref[...], a_scr[...], preferred_element_type=F32)
    y = y - jnp.dot(sn_ref[...], b_scr[...], preferred_element_type=F32)
    o_ref[...] = (y * norm).astype(o_ref.dtype)


def _fnet(p, cn, sn, ccb, scb, bsz, n, fw, col):
    tr = _tile(n, 512)
    norm = 1.0 / math.sqrt(n * HEAD_DIM)
    return pl.pallas_call(
        functools.partial(_fnet_kernel, norm=norm),
        out_shape=jax.ShapeDtypeStruct((bsz * n, fw), BF16),
        grid=(bsz, n // tr),
        in_specs=[
            pl.BlockSpec((n, fw), lambda b, i: (b, col)),
            pl.BlockSpec((tr, n), lambda b, i: (i, 0)),
            pl.BlockSpec((tr, n), lambda b, i: (i, 0)),
            pl.BlockSpec((fw, fw), lambda b, i: (0, 0)),
            pl.BlockSpec((fw, fw), lambda b, i: (0, 0)),
        ],
        out_specs=pl.BlockSpec((tr, fw), lambda b, i: (b * (n // tr) + i, 0)),
        scratch_shapes=[pltpu.VMEM((n, fw), BF16), pltpu.VMEM((n, fw), BF16)],
        compiler_params=_params("parallel", "arbitrary"),
        name="fnet",
    )(p, cn, sn, ccb, scb)


def _dft_tables(n):
    if n <= 256:
        idx = np.outer(np.arange(n), np.arange(n)) % n
        ang = 2.0 * np.pi * idx / n
        return jnp.asarray(np.cos(ang), F32).astype(BF16), jnp.asarray(np.sin(ang), F32).astype(BF16)
    r = int(round(math.sqrt(n)))
    assert r * r == n
    k = np.arange(n)
    t = np.arange(r)
    ang_a = 2.0 * np.pi * (np.outer(t, k) % r) / r
    ang_b = 2.0 * np.pi * (np.outer(t, k) % n) / n
    ca, sa = jnp.asarray(np.cos(ang_a), F32)[:, None, :], jnp.asarray(np.sin(ang_a), F32)[:, None, :]
    cb, sb = jnp.asarray(np.cos(ang_b), F32)[None, :, :], jnp.asarray(np.sin(ang_b), F32)[None, :, :]
    cos = (ca * cb - sa * sb).reshape(n, n).astype(BF16)
    sin = (sa * cb + ca * sb).reshape(n, n).astype(BF16)
    return cos, sin


def _channel_dft_tables(fw):
    idx = np.outer(np.arange(HEAD_DIM), np.arange(HEAD_DIM)) % HEAD_DIM
    ang = 2.0 * np.pi * idx / HEAD_DIM
    eye = np.eye(fw // HEAD_DIM)
    return (jnp.asarray(np.kron(eye, np.cos(ang)), F32).astype(BF16),
            jnp.asarray(np.kron(eye, np.sin(ang)), F32).astype(BF16))


def kernel(x, c, ctx, c_ctx, ada_w, ada_b, norm1_g, norm2_g, w_in, w_out, sgu_norm_g, sgu_w, sgu_b, na_rpb,
           diff_lq1, diff_lk1, diff_lq2, diff_lk2, diff_subln_g, mlp_w1, mlp_w2, final_g):
    bsz, seq, d = x.shape
    ctx_len = ctx.shape[1]
    depth = ada_w.shape[0]
    rows = seq // GRID_W
    mix_heads = d // HEAD_DIM
    sgu_w_dim = (mix_heads // 2) * HEAD_DIM
    na_w = d - sgu_w_dim
    diff_heads = (mix_heads * 3) // 4
    fnet_w = d - diff_heads * HEAD_DIM
    assert sgu_w_dim == na_w and bsz + 1 <= MOD_ROWS

    cond = jnp.zeros((MOD_ROWS, d), F32).at[:bsz].set(c).at[bsz].set(c_ctx)
    mods = _ada_table(cond, ada_w, ada_b).reshape(depth, MOD_ROWS, 1, ADA_CHUNKS * d)

    w_out_b = w_out.astype(BF16)
    sgu_w_b = sgu_w.astype(BF16)
    n1g = norm1_g.reshape(depth, 1, d)
    n2g = norm2_g.reshape(depth, 1, d)
    sgu_g = sgu_norm_g.reshape(-1, 1, sgu_w_dim)
    sgu_b_full = jnp.broadcast_to(sgu_b[..., None], sgu_b.shape + (HEAD_DIM,)).astype(F32)

    tm_lat = _tile(seq, 512)
    tm_ctx = _tile(bsz * ctx_len, 512)
    lat_row = lambda i: (i * tm_lat) // seq
    ctx_row = lambda i: bsz

    x_lat = x.reshape(bsz * seq, d)
    x_ctx = ctx.reshape(bsz * ctx_len, d)

    if depth > 1:
        rope_tabs = _rope_tables(seq)
        cn, sn = _dft_tables(seq)
        cn_c, sn_c = _dft_tables(ctx_len)
        ccb, scb = _channel_dft_tables(fnet_w)

    for l in range(depth):
        need_ctx = l < depth - 1
        p_ctx, w_in_l = _in_proj_ctx(x_ctx, mods, n1g, w_in, l, bsz)
        p_lat = _in_proj(x_lat, mods, n1g, w_in_l, l, lat_row, tm_lat)
        i = l // 2
        ma_ctx = mb_ctx = None
        if l % 2 == 0:
            bias = _na_bias(na_rpb[i], rows)
            ma_lat = _sgu(p_lat, sgu_g, sgu_w_b, sgu_b_full, i, sgu_w_dim)
            mb_lat = _na_lat(p_lat, p_ctx, bias, bsz, seq, ctx_len, na_w, 2)
            if need_ctx:
                ma_ctx = _sgu(p_ctx, sgu_g, sgu_w_b, sgu_b_full, i, sgu_w_dim)
                mb_ctx = _na_ctx(p_ctx, bsz, ctx_len, na_w, 2)
        else:
            lam_init = 0.8 - 0.6 * math.exp(-0.3 * l)
            lam_p = jnp.stack([diff_lq1[i], diff_lk1[i], diff_lq2[i], diff_lk2[i]]).astype(F32)
            fcol = (3 * diff_heads * HEAD_DIM) // fnet_w
            ma_lat = _diff_lat(p_lat, p_ctx, rope_tabs, lam_p, diff_subln_g[i], bsz, seq, ctx_len, diff_heads, lam_init)
            mb_lat = _fnet(p_lat, cn, sn, ccb, scb, bsz, seq, fnet_w, fcol)
            if need_ctx:
                ma_ctx = _diff_ctx(p_ctx, lam_p, diff_subln_g[i], bsz, ctx_len, diff_heads, lam_init)
                mb_ctx = _fnet(p_ctx, cn_c, sn_c, ccb, scb, bsz, ctx_len, fnet_w, fcol)
        if need_ctx:
            x_ctx = _out_proj(x_ctx, ma_ctx, mb_ctx, w_out_b, mods, l, ctx_row, tm_ctx)
            x_ctx, w1_l, w2_l = _mlp_ctx(x_ctx, mods, n2g, mlp_w1, mlp_w2, l, bsz)
        else:
            w1_l, w2_l = mlp_w1[l].astype(BF16), mlp_w2[l].astype(BF16)
        x_lat = _out_proj(x_lat, ma_lat, mb_lat, w_out_b, mods, l, lat_row, tm_lat)
        x_lat = _mlp(x_lat, mods, n2g, w1_l, w2_l, final_g, l, lat_row, tm_lat, final_norm=not need_ctx)
    return x_lat.reshape(bsz, seq, d)
```

```python
import functools
import math

import numpy as np
import jax
import jax.numpy as jnp
from jax import lax
from jax.experimental import pallas as pl
from jax.experimental.pallas import tpu as pltpu

F32 = jnp.float32
BF16 = jnp.bfloat16

GRID_W = 64
HEAD_DIM = 128
CHUNK = 128
NA_KH = 8
NA_KW = 16
ROPE_THETA = 10000.0
Q_ROWS = 4
K_ROWS = Q_ROWS + NA_KH
RMS_EPS = 1e-6
LN_EPS = 1e-5
ADA_CHUNKS = 6
MOD_ROWS = 16

VMEM_LIMIT = 56 * 1024 * 1024

_NT = (((1,), (1,)), ((), ()))

_Q_SCALE = (HEAD_DIM // 2) ** -0.5 * math.log2(math.e)
DIFF_TQ = 1024
DIFF_TK = 1024
DIFF_AHEAD = 1
DIFF_BOUND_MARGIN = 1.01
DIFF_L_MIN = 2.0 ** -100
SUM_ROWS = 16


def _params(*sem):
    return pltpu.CompilerParams(dimension_semantics=sem, vmem_limit_bytes=VMEM_LIMIT)


def _tile(n, pref):
    if n <= pref:
        return n
    t = (pref // 128) * 128
    while n % t:
        t -= 128
    assert t > 0, (n, pref)
    return t


def _ada_kernel(s_ref, w_ref, b_ref, o_ref):
    s = s_ref[...]
    a = (s * jax.nn.sigmoid(s)).astype(BF16)
    o_ref[...] = jnp.dot(a, w_ref[...].astype(BF16), preferred_element_type=F32) + b_ref[...]


def _ada_table(cond, ada_w, ada_b):
    depth, d, n = ada_w.shape
    tn = _tile(n, 1024)
    return pl.pallas_call(
        _ada_kernel,
        out_shape=jax.ShapeDtypeStruct((depth, MOD_ROWS, n), F32),
        grid=(depth, n // tn),
        in_specs=[
            pl.BlockSpec((MOD_ROWS, d), lambda l, j: (0, 0)),
            pl.BlockSpec((None, d, tn), lambda l, j: (l, 0, j)),
            pl.BlockSpec((None, 1, tn), lambda l, j: (l, 0, j)),
        ],
        out_specs=pl.BlockSpec((None, MOD_ROWS, tn), lambda l, j: (l, 0, j)),
        compiler_params=_params("parallel", "parallel"),
        name="ada_table",
    )(cond, ada_w, ada_b.reshape(depth, 1, n))


def _norm_mod_rows(x_ref, g_ref, sh_ref, sc_ref, h_ref, unrolled=False, part=None):
    rows = 64
    g = g_ref[...]
    sc1 = 1.0 + sc_ref[...]
    sh = sh_ref[...]
    total = x_ref.shape[0] // rows
    if part is not None:
        assert unrolled and total % part[1] == 0
        total //= part[1]

    def body(r, carry):
        if part is not None:
            sl = pl.ds(pl.multiple_of((part[0] * total + r) * rows, rows), rows)
        elif unrolled:
            sl = slice(r * rows, (r + 1) * rows)
        else:
            sl = pl.ds(pl.multiple_of(r * rows, rows), rows)
        xv = x_ref[sl, :]
        ms = jnp.mean(xv * xv, axis=-1, keepdims=True)
        y = xv * lax.rsqrt(ms + RMS_EPS) * g
        h_ref[sl, :] = (y * sc1 + sh).astype(h_ref.dtype)
        return carry

    if unrolled:
        for r in range(total):
            body(r, 0)
    else:
        lax.fori_loop(0, total, body, 0)


def _mod_spec(d, layer, chunk, row_fn):
    return pl.BlockSpec((None, None, 1, d), lambda i, j: (layer, row_fn(i), 0, chunk))


def _layer_vec_spec(d, layer):
    return pl.BlockSpec((None, 1, d), lambda i, j: (layer, 0, 0))


def _in_kernel(x0_ref, xn_ref, g_ref, sh0_ref, sc0_ref, shn_ref, scn_ref, w_ref, o_ref, h0_scr, h1_scr):
    i = pl.program_id(0)

    @pl.when(i == 0)
    def _():
        _norm_mod_rows(x0_ref, g_ref, sh0_ref, sc0_ref, h0_scr)

    def step(h_cur, h_next):
        o_ref[...] = jnp.dot(h_cur[...], w_ref[...], preferred_element_type=F32).astype(o_ref.dtype)
        _norm_mod_rows(xn_ref, g_ref, shn_ref, scn_ref, h_next, unrolled=True)

    @pl.when(i % 2 == 0)
    def _():
        step(h0_scr, h1_scr)

    @pl.when(i % 2 == 1)
    def _():
        step(h1_scr, h0_scr)


def _in_proj(x, mods, g, w, layer, row_fn, tm):
    m, d = x.shape
    n = w.shape[1]
    nt = m // tm
    nxt = lambda i: jnp.minimum(i + 1, nt - 1)
    once = pl.Buffered(1)
    return pl.pallas_call(
        _in_kernel,
        out_shape=jax.ShapeDtypeStruct((m, n), BF16),
        grid=(nt, 1),
        in_specs=[
            pl.BlockSpec((tm, d), lambda i, j: (0, 0), pipeline_mode=once),
            pl.BlockSpec((tm, d), lambda i, j: (nxt(i), 0)),
            _layer_vec_spec(d, layer),
            _mod_spec(d, layer, 0, lambda i: row_fn(0)),
            _mod_spec(d, layer, 1, lambda i: row_fn(0)),
            _mod_spec(d, layer, 0, lambda i: row_fn(nxt(i))),
            _mod_spec(d, layer, 1, lambda i: row_fn(nxt(i))),
            pl.BlockSpec((d, n), lambda i, j: (0, 0), pipeline_mode=once),
        ],
        out_specs=pl.BlockSpec((tm, n), lambda i, j: (i, 0)),
        scratch_shapes=[pltpu.VMEM((tm, d), BF16), pltpu.VMEM((tm, d), BF16)],
        compiler_params=_params("arbitrary", "arbitrary"),
        name="in_proj",
    )(x, x, g, mods, mods, mods, mods, w)


def _out_kernel(x_ref, ma_ref, mb_ref, wa_ref, wb_ref, gate_ref, o_ref):
    y = jnp.dot(ma_ref[...], wa_ref[...], preferred_element_type=F32)
    y = y + jnp.dot(mb_ref[...], wb_ref[...], preferred_element_type=F32)
    o_ref[...] = x_ref[...] + gate_ref[...] * y


def _out_proj(x, ma, mb, w, mods, layer, row_fn, tm):
    m, d = x.shape
    ka, kb = ma.shape[1], mb.shape[1]
    assert ka % kb == 0 and w.shape[1:] == (ka + kb, d)
    return pl.pallas_call(
        _out_kernel,
        out_shape=jax.ShapeDtypeStruct((m, d), F32),
        grid=(m // tm, 1),
        in_specs=[
            pl.BlockSpec((tm, d), lambda i, j: (i, 0)),
            pl.BlockSpec((tm, ka), lambda i, j: (i, 0)),
            pl.BlockSpec((tm, kb), lambda i, j: (i, 0)),
            pl.BlockSpec((None, ka, d), lambda i, j: (layer, 0, 0)),
            pl.BlockSpec((None, kb, d), lambda i, j: (layer, ka // kb, 0)),
            _mod_spec(d, layer, 2, row_fn),
        ],
        out_specs=pl.BlockSpec((tm, d), lambda i, j: (i, 0)),
        compiler_params=_params("parallel", "arbitrary"),
        name="out_proj",
    )(x, ma, mb, w, w, mods)


def _mlp_kernel(x_ref, x0_ref, xn_ref, g_ref, sh0_ref, sc0_ref, shn_ref, scn_ref, gate_ref, w1_ref, w2_ref, fg_ref,
                o_ref, h_scr, *, final_norm, n_parts):
    i = pl.program_id(0)
    f = pl.program_id(1)
    nf = n_parts
    slot = i % 2

    @pl.when((i == 0) & (f == 0))
    def _():
        _norm_mod_rows(x0_ref, g_ref, sh0_ref, sc0_ref, h_scr.at[0])

    def step(first, last):
        a = jnp.dot(h_scr[slot], w1_ref[...], preferred_element_type=F32)
        a = jnp.square(jnp.maximum(a, 0.0)).astype(BF16)
        y = jnp.dot(a, w2_ref[...], preferred_element_type=F32)
        if not first:
            y = o_ref[...] + y
        if last:
            y = x_ref[...] + gate_ref[...] * y
            if final_norm:
                y = y * lax.rsqrt(jnp.mean(y * y, axis=-1, keepdims=True) + RMS_EPS) * fg_ref[...]
        o_ref[...] = y
        _norm_mod_rows(xn_ref, g_ref, shn_ref, scn_ref, h_scr.at[1 - slot], unrolled=True, part=(f, n_parts))

    pl.when(f == 0)(lambda: step(True, False))
    pl.when((f > 0) & (f < nf - 1))(lambda: step(False, False))
    pl.when(f == nf - 1)(lambda: step(False, True))


def _mlp(x, mods, g, w1, w2, final_g, layer, row_fn, tm, final_norm):
    m, d = x.shape
    dff = w1.shape[1]
    tf = _tile(dff, 1024)
    nt = m // tm
    assert dff // tf >= 2
    nxt = lambda i: jnp.minimum(i + 1, nt - 1)
    return pl.pallas_call(
        functools.partial(_mlp_kernel, final_norm=final_norm, n_parts=dff // tf),
        out_shape=jax.ShapeDtypeStruct((m, d), F32),
        grid=(nt, dff // tf),
        in_specs=[
            pl.BlockSpec((tm, d), lambda i, j: (i, 0)),
            pl.BlockSpec((tm, d), lambda i, j: (0, 0), pipeline_mode=pl.Buffered(1)),
            pl.BlockSpec((tm, d), lambda i, j: (nxt(i), 0)),
            _layer_vec_spec(d, layer),
            _mod_spec(d, layer, 3, lambda i: row_fn(0)),
            _mod_spec(d, layer, 4, lambda i: row_fn(0)),
            _mod_spec(d, layer, 3, lambda i: row_fn(nxt(i))),
            _mod_spec(d, layer, 4, lambda i: row_fn(nxt(i))),
            _mod_spec(d, layer, 5, row_fn),
            pl.BlockSpec((d, tf), lambda i, j: (0, j)),
            pl.BlockSpec((tf, d), lambda i, j: (j, 0)),
            pl.BlockSpec((1, d), lambda i, j: (0, 0)),
        ],
        out_specs=pl.BlockSpec((tm, d), lambda i, j: (i, 0)),
        scratch_shapes=[pltpu.VMEM((2, tm, d), BF16)],
        compiler_params=_params("arbitrary", "arbitrary"),
        name="mlp",
    )(x, x, x, g, mods, mods, mods, mods, mods, w1, w2, final_g.reshape(1, d))


def _in_ctx_kernel(x_ref, g_ref, sh_ref, sc_ref, w_ref, o_ref, wb_ref, h_scr):
    @pl.when(pl.program_id(0) == 0)
    def _():
        _norm_mod_rows(x_ref, g_ref, sh_ref, sc_ref, h_scr)

    wb = w_ref[...].astype(BF16)
    wb_ref[...] = wb
    o_ref[...] = jnp.dot(h_scr[...], wb, preferred_element_type=F32).astype(o_ref.dtype)


def _ctx_vec_specs(d, layer, row, chunks):
    specs = [pl.BlockSpec((None, 1, d), lambda j: (layer, 0, 0))]
    for chunk in chunks:
        specs.append(pl.BlockSpec((None, None, 1, d), lambda j, chunk=chunk: (layer, row, 0, chunk)))
    return specs


def _in_proj_ctx(x, mods, g, w, layer, row):
    m, d = x.shape
    n = w.shape[2]
    tn = _tile(n, 1024)
    return pl.pallas_call(
        _in_ctx_kernel,
        out_shape=(jax.ShapeDtypeStruct((m, n), BF16), jax.ShapeDtypeStruct((d, n), BF16)),
        grid=(n // tn,),
        in_specs=[pl.BlockSpec((m, d), lambda j: (0, 0), pipeline_mode=pl.Buffered(1))]
        + _ctx_vec_specs(d, layer, row, (0, 1))
        + [pl.BlockSpec((None, d, tn), lambda j: (layer, 0, j))],
        out_specs=(pl.BlockSpec((m, tn), lambda j: (0, j)), pl.BlockSpec((d, tn), lambda j: (0, j))),
        scratch_shapes=[pltpu.VMEM((m, d), BF16)],
        compiler_params=_params("arbitrary"),
        name="in_proj_ctx",
    )(x, g, mods, mods, w)


def _mlp_ctx_kernel(x_ref, g_ref, sh_ref, sc_ref, gate_ref, w1_ref, w2_ref, o_ref, w1b_ref, w2b_ref, h_scr, *, n_parts):
    f = pl.program_id(0)

    @pl.when(f == 0)
    def _():
        _norm_mod_rows(x_ref, g_ref, sh_ref, sc_ref, h_scr)

    def step(first, last):
        w1b = w1_ref[...].astype(BF16)
        w2b = w2_ref[...].astype(BF16)
        w1b_ref[...] = w1b
        w2b_ref[...] = w2b
        a = jnp.dot(h_scr[...], w1b, preferred_element_type=F32)
        a = jnp.square(jnp.maximum(a, 0.0)).astype(BF16)
        y = jnp.dot(a, w2b, preferred_element_type=F32)
        if not first:
            y = o_ref[...] + y
        if last:
            y = x_ref[...] + gate_ref[...] * y
        o_ref[...] = y

    pl.when(f == 0)(lambda: step(True, False))
    pl.when((f > 0) & (f < n_parts - 1))(lambda: step(False, False))
    pl.when(f == n_parts - 1)(lambda: step(False, True))


def _mlp_ctx(x, mods, g, w1, w2, layer, row):
    m, d = x.shape
    dff = w1.shape[2]
    tf = _tile(dff, 512)
    assert dff // tf >= 2
    return pl.pallas_call(
        functools.partial(_mlp_ctx_kernel, n_parts=dff // tf),
        out_shape=(jax.ShapeDtypeStruct((m, d), F32), jax.ShapeDtypeStruct((d, dff), BF16),
                   jax.ShapeDtypeStruct((dff, d), BF16)),
        grid=(dff // tf,),
        in_specs=[pl.BlockSpec((m, d), lambda j: (0, 0), pipeline_mode=pl.Buffered(1))]
        + _ctx_vec_specs(d, layer, row, (3, 4, 5))
        + [pl.BlockSpec((None, d, tf), lambda j: (layer, 0, j)), pl.BlockSpec((None, tf, d), lambda j: (layer, j, 0))],
        out_specs=(pl.BlockSpec((m, d), lambda j: (0, 0), pipeline_mode=pl.Buffered(1)), pl.BlockSpec((d, tf), lambda j: (0, j)),
                   pl.BlockSpec((tf, d), lambda j: (j, 0))),
        scratch_shapes=[pltpu.VMEM((m, d), BF16)],
        compiler_params=_params("arbitrary"),
        name="mlp_ctx",
    )(x, g, mods, mods, mods, w1, w2)


def _sgu_kernel(u_ref, v_ref, gam_ref, ws_ref, b_ref, o_ref, *, groups):
    for c in range(u_ref.shape[0] // CHUNK):
        rs = slice(c * CHUNK, (c + 1) * CHUNK)
        for g in range(groups):
            cs = slice(g * HEAD_DIM, (g + 1) * HEAD_DIM)
            v = v_ref[rs, cs].astype(F32)
            mu = jnp.mean(v, axis=-1, keepdims=True)
            dv = v - mu
            var = jnp.mean(dv * dv, axis=-1, keepdims=True)
            vn = (dv * lax.rsqrt(var + LN_EPS) * gam_ref[:, cs]).astype(BF16)
            mixed = jnp.dot(ws_ref[g], vn, preferred_element_type=F32) + b_ref[g]
            o_ref[rs, cs] = (u_ref[rs, cs].astype(F32) * mixed).astype(o_ref.dtype)


def _sgu(p, gam, ws, b_full, layer, sgu_w):
    m = p.shape[0]
    groups = sgu_w // HEAD_DIM
    tr = _tile(m, 512)
    return pl.pallas_call(
        functools.partial(_sgu_kernel, groups=groups),
        out_shape=jax.ShapeDtypeStruct((m, sgu_w), BF16),
        grid=(m // tr,),
        in_specs=[
            pl.BlockSpec((tr, sgu_w), lambda i: (i, 0)),
            pl.BlockSpec((tr, sgu_w), lambda i: (i, 1)),
            pl.BlockSpec((None, 1, sgu_w), lambda i: (layer, 0, 0)),
            pl.BlockSpec((None, groups, CHUNK, CHUNK), lambda i: (layer, 0, 0, 0)),
            pl.BlockSpec((None, groups, CHUNK, HEAD_DIM), lambda i: (layer, 0, 0, 0)),
        ],
        out_specs=pl.BlockSpec((tr, sgu_w), lambda i: (i, 0)),
        compiler_params=_params("parallel"),
        name="sgu",
    )(p, p, gam, ws, b_full)


def _softmax_pv(scores, values):
    d = values[0].shape[1]
    m = scores[0].max(axis=-1, keepdims=True)
    for s in scores[1:]:
        m = jnp.maximum(m, s.max(axis=-1, keepdims=True))
    o = None
    for s, v in zip(scores, values):
        v1 = jnp.concatenate([v, jnp.ones(v.shape, v.dtype)], axis=1)
        pv = jnp.dot(jnp.exp2(s - m).astype(BF16), v1, preferred_element_type=F32)
        o = pv if o is None else o + pv
    return o[:, :d] / o[:, d:]


_NA_SCALE = HEAD_DIM ** -0.5 * math.log2(math.e)


def _na_kernel(q_ref, k0_ref, k1_ref, k2_ref, v0_ref, v1_ref, v2_ref, kc_ref, vc_ref, bias_ref, o_ref, *, heads):
    qb = q_ref.shape[0]
    scale = _NA_SCALE
    for h in range(heads):
        cs = slice(h * HEAD_DIM, (h + 1) * HEAD_DIM)
        q = (q_ref[:, cs].astype(F32) * scale).astype(BF16)
        scores, values = [], []
        for j, (k_ref, v_ref) in enumerate(((k0_ref, v0_ref), (k1_ref, v1_ref), (k2_ref, v2_ref))):
            s = lax.dot_general(q, k_ref[:, cs], _NT, preferred_element_type=F32)
            scores.append(s + bias_ref[h, :, j * qb:(j + 1) * qb])
            values.append(v_ref[:, cs])
        scores.append(lax.dot_general(q, kc_ref[:, cs], _NT, preferred_element_type=F32))
        values.append(vc_ref[:, cs])
        o_ref[:, cs] = _softmax_pv(scores, values).astype(o_ref.dtype)


def _na_bias(rpb, rows):
    heads = rpb.shape[0]
    kw = NA_KW
    col = jnp.arange(GRID_W)
    col_start = jnp.clip(col - kw // 2, 0, GRID_W - kw)
    col_mask = (col[None, :] >= col_start[:, None]) & (col[None, :] < col_start[:, None] + kw)
    pad = GRID_W - kw
    ext = jnp.pad(rpb.astype(F32), ((0, 0), (0, 0), (pad, pad)), mode="edge")
    toeplitz = jnp.stack([ext[:, :, GRID_W - 1 - qc:2 * GRID_W - 1 - qc] for qc in range(GRID_W)], axis=2)
    rpb_x = jnp.where(col_mask[None, None], toeplitz * math.log2(math.e), -jnp.inf)
    outside = jnp.full((heads, GRID_W, GRID_W), -jnp.inf, F32)
    nblk = rows // Q_ROWS
    blocks = []
    for kblk in (0, 1, nblk - 1):
        ks = min(max(Q_ROWS * kblk - NA_KH // 2, 0), rows - K_ROWS)
        qrows = []
        for i in range(Q_ROWS):
            r = Q_ROWS * kblk + i
            rs = min(max(r - NA_KH // 2, 0), rows - NA_KH)
            tiles = [rpb_x[:, ks + j - r + NA_KH - 1] if rs <= ks + j < rs + NA_KH else outside for j in range(K_ROWS)]
            qrows.append(jnp.concatenate(tiles, axis=-1))
        blocks.append(jnp.concatenate(qrows, axis=1))
    return jnp.stack(blocks)


def _na_lat(p_lat, p_ctx, bias, bsz, seq, ctx_len, na_w, col0):
    heads = na_w // HEAD_DIM
    qb = Q_ROWS * GRID_W
    nblk = seq // qb
    assert nblk >= 4 and K_ROWS == 3 * Q_ROWS

    def kblock(j):
        return lambda b, k: (b * nblk + jnp.clip(k - 1, 0, nblk - 3) + j, col0 + 1)

    def vblock(j):
        return lambda b, k: (b * nblk + jnp.clip(k - 1, 0, nblk - 3) + j, col0 + 2)

    def btype(b, k):
        return (jnp.where(k == 0, 0, jnp.where(k == nblk - 1, 2, 1)), 0, 0, 0)

    return pl.pallas_call(
        functools.partial(_na_kernel, heads=heads),
        out_shape=jax.ShapeDtypeStruct((bsz * seq, na_w), BF16),
        grid=(bsz, nblk),
        in_specs=[
            pl.BlockSpec((qb, na_w), lambda b, k: (b * nblk + k, col0)),
            pl.BlockSpec((qb, na_w), kblock(0)),
            pl.BlockSpec((qb, na_w), kblock(1)),
            pl.BlockSpec((qb, na_w), kblock(2)),
            pl.BlockSpec((qb, na_w), vblock(0)),
            pl.BlockSpec((qb, na_w), vblock(1)),
            pl.BlockSpec((qb, na_w), vblock(2)),
            pl.BlockSpec((ctx_len, na_w), lambda b, k: (b, col0 + 1)),
            pl.BlockSpec((ctx_len, na_w), lambda b, k: (b, col0 + 2)),
            pl.BlockSpec((None, heads, qb, K_ROWS * GRID_W), btype),
        ],
        out_specs=pl.BlockSpec((qb, na_w), lambda b, k: (b * nblk + k, 0)),
        compiler_params=_params("parallel", "arbitrary"),
        name="na_lat",
    )(p_lat, p_lat, p_lat, p_lat, p_lat, p_lat, p_lat, p_ctx, p_ctx, bias)


def _attn_ctx_kernel(q_ref, k_ref, v_ref, o_ref, *, heads):
    for h in range(heads):
        cs = slice(h * HEAD_DIM, (h + 1) * HEAD_DIM)
        q = (q_ref[:, cs].astype(F32) * _NA_SCALE).astype(BF16)
        s = lax.dot_general(q, k_ref[:, cs], _NT, preferred_element_type=F32)
        o_ref[:, cs] = _softmax_pv([s], [v_ref[:, cs]]).astype(o_ref.dtype)


def _na_ctx(p_ctx, bsz, ctx_len, na_w, col0):
    heads = na_w // HEAD_DIM
    return pl.pallas_call(
        functools.partial(_attn_ctx_kernel, heads=heads),
        out_shape=jax.ShapeDtypeStruct((bsz * ctx_len, na_w), BF16),
        grid=(bsz,),
        in_specs=[pl.BlockSpec((ctx_len, na_w), lambda b, c=c: (b, col0 + c)) for c in range(3)],
        out_specs=pl.BlockSpec((ctx_len, na_w), lambda b: (b, 0)),
        compiler_params=_params("parallel"),
        name="na_ctx",
    )(p_ctx, p_ctx, p_ctx)


def _swap16(x):
    lane = lax.broadcasted_iota(jnp.int32, x.shape, 1)
    up = pltpu.roll(x, HEAD_DIM - 16, 1)
    down = pltpu.roll(x, 16, 1)
    return jnp.where((lane & 16) == 0, up, down)


def _diff_kernel(*refs, n_lat, rope, lam_init):
    if rope:
        (q_ref, k_ref, v_ref, kc_ref, vc_ref, cosq_ref, sinq_ref, cos_ref, sin_ref, lam_ref, g_ref,
         o_ref, k_scr, vt_scr, kmax_scr) = refs
    else:
        q_ref, kc_ref, vc_ref, lam_ref, g_ref, o_ref, k_scr, vt_scr, kmax_scr = refs
    n_ctx = kc_ref.shape[0]
    nk = n_lat + n_ctx
    hd = HEAD_DIM
    half = hd // 2

    @pl.when(pl.program_id(2) == 0)
    def _():
        step = 512
        row = lax.broadcasted_iota(jnp.int32, (hd, hd), 0)
        col = lax.broadcasted_iota(jnp.int32, (hd, hd), 1)
        sel = jnp.where(((col == 0) & (row < half)) | ((col == 1) & (row >= half)), 1.0, 0.0).astype(BF16)

        def put_keys(rs, kr, kmax):
            n = kr.shape[0]
            k_scr[rs, :hd] = kr
            lane = lax.broadcasted_iota(jnp.int32, (n, hd), 1)
            k_scr[rs, hd:] = jnp.where(lane == 0, 1.0, 0.0).astype(BF16)
            kf = kr.astype(F32)
            sq = jnp.dot((kf * kf).astype(BF16), sel, preferred_element_type=F32).max(axis=0, keepdims=True)
            return sq if kmax is None else jnp.maximum(kmax, sq)

        kmax = None
        for c in range(n_lat // step):
            rs = slice(c * step, (c + 1) * step)
            k = k_ref[rs, :].astype(F32)
            kmax = put_keys(rs, (k * cos_ref[rs, :] + _swap16(k) * sin_ref[rs, :]).astype(BF16), kmax)
            vt_scr[:hd, rs] = v_ref[rs, :].astype(F32).T.astype(BF16)
        kmax = put_keys(slice(n_lat, nk), kc_ref[...], kmax)
        vt_scr[:hd, n_lat:nk] = vc_ref[...].astype(F32).T.astype(BF16)
        vt_scr[hd:, :] = jnp.ones((SUM_ROWS, nk), BF16)
        lane1 = lax.broadcasted_iota(jnp.int32, kmax.shape, 1)
        for mp in range(2):
            kmax_scr[mp:mp + 1, :] = jnp.broadcast_to(
                jnp.max(jnp.where(lane1 == mp, kmax, 0.0), axis=-1, keepdims=True), kmax.shape)

    tq = q_ref.shape[0]

    def prep_q(qr_ref, cq_ref, sq_ref):
        qraw = qr_ref[...]
        q = qraw.astype(F32)
        row = lax.broadcasted_iota(jnp.int32, (hd, hd), 0)
        col = lax.broadcasted_iota(jnp.int32, (hd, hd), 1)
        sel1 = jnp.where((col == 0) & (row < half), 1.0, 0.0).astype(BF16)
        sel2 = jnp.where((col == 0) & (row >= half), 1.0, 0.0).astype(BF16)
        sq = (q * q).astype(BF16)
        n1 = jnp.dot(sq, sel1, preferred_element_type=F32)
        n2 = jnp.dot(sq, sel2, preferred_element_type=F32)
        if rope:
            perm = jnp.where(row == (col ^ 16), 1.0, 0.0).astype(BF16)
            q = q * cq_ref[...] + jnp.dot(qraw, perm, preferred_element_type=F32) * sq_ref[...]
        else:
            q = q * _Q_SCALE
        lane = lax.broadcasted_iota(jnp.int32, q.shape, 1)
        shift1 = jnp.sqrt(n1 * kmax_scr[0:1, :]) * (-_Q_SCALE * DIFF_BOUND_MARGIN)
        shift2 = jnp.sqrt(n2 * kmax_scr[1:2, :]) * (-_Q_SCALE * DIFF_BOUND_MARGIN)
        return jnp.concatenate([
            jnp.concatenate([jnp.where(lane < half, q, 0.0), shift1], axis=1),
            jnp.concatenate([jnp.where(lane >= half, q, 0.0), shift2], axis=1),
        ], axis=0).astype(BF16)

    q_aug = prep_q(q_ref, cosq_ref, sinq_ref) if rope else prep_q(q_ref, None, None)

    lp = lam_ref[...]
    d1 = jnp.sum(lp[0:1, :] * lp[1:2, :], axis=-1, keepdims=True)
    d2 = jnp.sum(lp[2:3, :] * lp[3:4, :], axis=-1, keepdims=True)
    lam = jnp.exp(d1) - jnp.exp(d2) + lam_init

    chunks = [(k0, min(DIFF_TK, nk - k0)) for k0 in range(0, nk, DIFF_TK)]

    def write_out(ot):
        ot = ot * lax.rsqrt(jnp.mean(ot * ot, axis=0, keepdims=True) + RMS_EPS)
        o_ref[...] = ((ot.T * g_ref[...]) * (1.0 - lam_init)).astype(o_ref.dtype)

    def finish(acc):
        l = acc[hd:hd + 1, :]
        c1 = 1.0 / l[:, :tq]
        c2 = lam / l[:, tq:]
        write_out(acc[:hd, :tq] * c1 - acc[:hd, tq:] * c2)

    def pipelined(scores, consume):
        pending = [scores(c) for c in range(min(DIFF_AHEAD, len(chunks)))]
        for c in range(len(chunks)):
            s = pending.pop(0)
            if c + DIFF_AHEAD < len(chunks):
                pending.append(scores(c + DIFF_AHEAD))
            consume(c, s)

    state = {}

    def fast_scores(c):
        k0, tk = chunks[c]
        return lax.dot_general(k_scr[k0:k0 + tk, :], q_aug, _NT, preferred_element_type=F32)

    def fast_consume(c, s):
        k0, tk = chunks[c]
        pv = jnp.dot(vt_scr[:, k0:k0 + tk], jnp.exp2(s).astype(BF16), preferred_element_type=F32)
        state["acc"] = pv if c == 0 else state["acc"] + pv

    pipelined(fast_scores, fast_consume)
    acc = state["acc"]
    finish(acc)

    @pl.when(jnp.logical_not(jnp.min(acc[hd:hd + 1, :]) > DIFF_L_MIN))
    def _():
        qab = q_aug[:, :hd]
        st = {}

        def safe_scores(c):
            k0, tk = chunks[c]
            return lax.dot_general(k_scr[k0:k0 + tk, :hd], qab, _NT, preferred_element_type=F32)

        def safe_consume(c, s):
            k0, tk = chunks[c]
            mc = s.max(axis=0, keepdims=True)
            m_new = mc if c == 0 else jnp.maximum(st["m"], mc)
            pv = jnp.dot(vt_scr[:, k0:k0 + tk], jnp.exp2(s - m_new).astype(BF16), preferred_element_type=F32)
            st["acc"] = pv if c == 0 else st["acc"] * jnp.exp2(st["m"] - m_new) + pv
            st["m"] = m_new

        pipelined(safe_scores, safe_consume)
        finish(st["acc"])


def _diff_lat(p_lat, p_ctx, tabs, lam_p, sub_g, bsz, seq, ctx_len, heads, lam_init):
    tq = DIFF_TQ
    nq = seq // tq
    hd = HEAD_DIM
    nk = seq + ctx_len
    cosq, sinq, cos, sin = tabs
    return pl.pallas_call(
        functools.partial(_diff_kernel, n_lat=seq, rope=True, lam_init=lam_init),
        out_shape=jax.ShapeDtypeStruct((bsz * seq, heads * hd), BF16),
        grid=(bsz, heads, nq),
        in_specs=[
            pl.BlockSpec((tq, hd), lambda b, h, i: (b * nq + i, h)),
            pl.BlockSpec((seq, hd), lambda b, h, i: (b, heads + h)),
            pl.BlockSpec((seq, hd), lambda b, h, i: (b, 2 * heads + h)),
            pl.BlockSpec((ctx_len, hd), lambda b, h, i: (b, heads + h)),
            pl.BlockSpec((ctx_len, hd), lambda b, h, i: (b, 2 * heads + h)),
            pl.BlockSpec((tq, hd), lambda b, h, i: (i, 0)),
            pl.BlockSpec((tq, hd), lambda b, h, i: (i, 0)),
            pl.BlockSpec((seq, hd), lambda b, h, i: (0, 0), pipeline_mode=pl.Buffered(1)),
            pl.BlockSpec((seq, hd), lambda b, h, i: (0, 0), pipeline_mode=pl.Buffered(1)),
            pl.BlockSpec((4, hd // 2), lambda b, h, i: (0, 0)),
            pl.BlockSpec((1, hd), lambda b, h, i: (0, 0)),
        ],
        out_specs=pl.BlockSpec((tq, hd), lambda b, h, i: (b * nq + i, h)),
        scratch_shapes=[pltpu.VMEM((nk, 2 * hd), BF16), pltpu.VMEM((hd + SUM_ROWS, nk), BF16), pltpu.VMEM((8, hd), F32)],
        compiler_params=_params("parallel", "parallel", "arbitrary"),
        name="diff_lat",
    )(p_lat, p_lat, p_lat, p_ctx, p_ctx, cosq, sinq, cos, sin, lam_p, sub_g.reshape(1, hd))


def _diff_ctx(p_ctx, lam_p, sub_g, bsz, ctx_len, heads, lam_init):
    hd = HEAD_DIM
    return pl.pallas_call(
        functools.partial(_diff_kernel, n_lat=0, rope=False, lam_init=lam_init),
        out_shape=jax.ShapeDtypeStruct((bsz * ctx_len, heads * hd), BF16),
        grid=(bsz, heads, 1),
        in_specs=[
            pl.BlockSpec((ctx_len, hd), lambda b, h, i: (b, h)),
            pl.BlockSpec((ctx_len, hd), lambda b, h, i: (b, heads + h)),
            pl.BlockSpec((ctx_len, hd), lambda b, h, i: (b, 2 * heads + h)),
            pl.BlockSpec((4, hd // 2), lambda b, h, i: (0, 0)),
            pl.BlockSpec((1, hd), lambda b, h, i: (0, 0)),
        ],
        out_specs=pl.BlockSpec((ctx_len, hd), lambda b, h, i: (b, h)),
        scratch_shapes=[pltpu.VMEM((ctx_len, 2 * hd), BF16), pltpu.VMEM((hd + SUM_ROWS, ctx_len), BF16),
                        pltpu.VMEM((8, hd), F32)],
        compiler_params=_params("parallel", "parallel", "arbitrary"),
        name="diff_ctx",
    )(p_ctx, p_ctx, p_ctx, lam_p, sub_g.reshape(1, hd))


def _rope_tables(seq):
    axis = HEAD_DIM // 4
    t = jnp.arange(seq)
    row = (t // GRID_W).astype(F32)
    col = (t % GRID_W).astype(F32)
    inv = ROPE_THETA ** (-jnp.arange(0, axis, 2, dtype=F32) / axis)
    ang_r = row[:, None] * inv
    ang_c = col[:, None] * inv
    cr, sr, cc, sc = jnp.cos(ang_r), jnp.sin(ang_r), jnp.cos(ang_c), jnp.sin(ang_c)
    cos = jnp.concatenate([cr, cr, cc, cc] * 2, axis=-1)
    sin = jnp.concatenate([-sr, sr, -sc, sc] * 2, axis=-1)
    return cos * _Q_SCALE, sin * _Q_SCALE, cos, sin


def _fnet_kernel(f_ref, cn_ref, sn_ref, cc_ref, sc_ref, o_ref, a_scr, b_scr, *, norm):
    @pl.when(pl.program_id(1) == 0)
    def _():
        step = 512 if f_ref.shape[0] % 512 == 0 else f_ref.shape[0]
        for c in range(f_ref.shape[0] // step):
            rs = slice(c * step, (c + 1) * step)
            f = f_ref[rs, :]
            a_scr[rs, :] = jnp.dot(f, cc_ref[...], preferred_element_type=F32).astype(BF16)
            b_scr[rs, :] = jnp.dot(f, sc_ref[...], preferred_element_type=F32).astype(BF16)

    y = jnp.dot(cn_ref[...], a_scr[...], preferred_element_type=F32)
    y = y - jnp.dot(sn_ref[...], b_scr[...], preferred_element_type=F32)
    o_ref[...] = (y * norm).astype(o_ref.dtype)


def _fnet(p, cn, sn, ccb, scb, bsz, n, fw, col):
    tr = _tile(n, 512)
    norm = 1.0 / math.sqrt(n * HEAD_DIM)
    return pl.pallas_call(
        functools.partial(_fnet_kernel, norm=norm),
        out_shape=jax.ShapeDtypeStruct((bsz * n, fw), BF16),
        grid=(bsz, n // tr),
        in_specs=[
            pl.BlockSpec((n, fw), lambda b, i: (b, col)),
            pl.BlockSpec((tr, n), lambda b, i: (i, 0)),
            pl.BlockSpec((tr, n), lambda b, i: (i, 0)),
            pl.BlockSpec((fw, fw), lambda b, i: (0, 0)),
            pl.BlockSpec((fw, fw), lambda b, i: (0, 0)),
        ],
        out_specs=pl.BlockSpec((tr, fw), lambda b, i: (b * (n // tr) + i, 0)),
        scratch_shapes=[pltpu.VMEM((n, fw), BF16), pltpu.VMEM((n, fw), BF16)],
        compiler_params=_params("parallel", "arbitrary"),
        name="fnet",
    )(p, cn, sn, ccb, scb)


def _dft_tables(n):
    if n <= 256:
        idx = np.outer(np.arange(n), np.arange(n)) % n
        ang = 2.0 * np.pi * idx / n
        return jnp.asarray(np.cos(ang), F32).astype(BF16), jnp.asarray(np.sin(ang), F32).astype(BF16)
    r = int(round(math.sqrt(n)))
    assert r * r == n
    k = np.arange(n)
    t = np.arange(r)
    ang_a = 2.0 * np.pi * (np.outer(t, k) % r) / r
    ang_b = 2.0 * np.pi * (np.outer(t, k) % n) / n
    ca, sa = jnp.asarray(np.cos(ang_a), F32)[:, None, :], jnp.asarray(np.sin(ang_a), F32)[:, None, :]
    cb, sb = jnp.asarray(np.cos(ang_b), F32)[None, :, :], jnp.asarray(np.sin(ang_b), F32)[None, :, :]
    cos = (ca * cb - sa * sb).reshape(n, n).astype(BF16)
    sin = (sa * cb + ca * sb).reshape(n, n).astype(BF16)
    return cos, sin


def _channel_dft_tables(fw):
    idx = np.outer(np.arange(HEAD_DIM), np.arange(HEAD_DIM)) % HEAD_DIM
    ang = 2.0 * np.pi * idx / HEAD_DIM
    eye = np.eye(fw // HEAD_DIM)
    return (jnp.asarray(np.kron(eye, np.cos(ang)), F32).astype(BF16),
            jnp.asarray(np.kron(eye, np.sin(ang)), F32).astype(BF16))


def kernel(x, c, ctx, c_ctx, ada_w, ada_b, norm1_g, norm2_g, w_in, w_out, sgu_norm_g, sgu_w, sgu_b, na_rpb,
           diff_lq1, diff_lk1, diff_lq2, diff_lk2, diff_subln_g, mlp_w1, mlp_w2, final_g):
    bsz, seq, d = x.shape
    ctx_len = ctx.shape[1]
    depth = ada_w.shape[0]
    rows = seq // GRID_W
    mix_heads = d // HEAD_DIM
    sgu_w_dim = (mix_heads // 2) * HEAD_DIM
    na_w = d - sgu_w_dim
    diff_heads = (mix_heads * 3) // 4
    fnet_w = d - diff_heads * HEAD_DIM
    assert sgu_w_dim == na_w and bsz + 1 <= MOD_ROWS

    cond = jnp.zeros((MOD_ROWS, d), F32).at[:bsz].set(c).at[bsz].set(c_ctx)
    mods = _ada_table(cond, ada_w, ada_b).reshape(depth, MOD_ROWS, 1, ADA_CHUNKS * d)

    w_out_b = w_out.astype(BF16)
    sgu_w_b = sgu_w.astype(BF16)
    n1g = norm1_g.reshape(depth, 1, d)
    n2g = norm2_g.reshape(depth, 1, d)
    sgu_g = sgu_norm_g.reshape(-1, 1, sgu_w_dim)
    sgu_b_full = jnp.broadcast_to(sgu_b[..., None], sgu_b.shape + (HEAD_DIM,)).astype(F32)

    tm_lat = _tile(seq, 512)
    tm_ctx = _tile(bsz * ctx_len, 512)
    lat_row = lambda i: (i * tm_lat) // seq
    ctx_row = lambda i: bsz

    x_lat = x.reshape(bsz * seq, d)
    x_ctx = ctx.reshape(bsz * ctx_len, d)

    if depth > 1:
        rope_tabs = _rope_tables(seq)
        cn, sn = _dft_tables(seq)
        cn_c, sn_c = _dft_tables(ctx_len)
        ccb, scb = _channel_dft_tables(fnet_w)

    for l in range(depth):
        need_ctx = l < depth - 1
        p_ctx, w_in_l = _in_proj_ctx(x_ctx, mods, n1g, w_in, l, bsz)
        p_lat = _in_proj(x_lat, mods, n1g, w_in_l, l, lat_row, tm_lat)
        i = l // 2
        ma_ctx = mb_ctx = None
        if l % 2 == 0:
            bias = _na_bias(na_rpb[i], rows)
            ma_lat = _sgu(p_lat, sgu_g, sgu_w_b, sgu_b_full, i, sgu_w_dim)
            mb_lat = _na_lat(p_lat, p_ctx, bias, bsz, seq, ctx_len, na_w, 2)
            if need_ctx:
                ma_ctx = _sgu(p_ctx, sgu_g, sgu_w_b, sgu_b_full, i, sgu_w_dim)
                mb_ctx = _na_ctx(p_ctx, bsz, ctx_len, na_w, 2)
        else:
            lam_init = 0.8 - 0.6 * math.exp(-0.3 * l)
            lam_p = jnp.stack([diff_lq1[i], diff_lk1[i], diff_lq2[i], diff_lk2[i]]).astype(F32)
            fcol = (3 * diff_heads * HEAD_DIM) // fnet_w
            ma_lat = _diff_lat(p_lat, p_ctx, rope_tabs, lam_p, diff_subln_g[i], bsz, seq, ctx_len, diff_heads, lam_init)
            mb_lat = _fnet(p_lat, cn, sn, ccb, scb, bsz, seq, fnet_w, fcol)
            if need_ctx:
                ma_ctx = _diff_ctx(p_ctx, lam_p, diff_subln_g[i], bsz, ctx_len, diff_heads, lam_init)
                mb_ctx = _fnet(p_ctx, cn_c, sn_c, ccb, scb, bsz, ctx_len, fnet_w, fcol)
        if need_ctx:
            x_ctx = _out_proj(x_ctx, ma_ctx, mb_ctx, w_out_b, mods, l, ctx_row, tm_ctx)
            x_ctx, w1_l, w2_l = _mlp_ctx(x_ctx, mods, n2g, mlp_w1, mlp_w2, l, bsz)
        else:
            w1_l, w2_l = mlp_w1[l].astype(BF16), mlp_w2[l].astype(BF16)
        x_lat = _out_proj(x_lat, ma_lat, mb_lat, w_out_b, mods, l, lat_row, tm_lat)
        x_lat = _mlp(x_lat, mods, n2g, w1_l, w2_l, final_g, l, lat_row, tm_lat, final_norm=not need_ctx)
    return x_lat.reshape(bsz, seq, d)
```

```python
import functools
import math

import numpy as np
import jax
import jax.numpy as jnp
from jax import lax
from jax.experimental import pallas as pl
from jax.experimental.pallas import tpu as pltpu

F32 = jnp.float32
BF16 = jnp.bfloat16

GRID_W = 64
HEAD_DIM = 128
CHUNK = 128
NA_KH = 8
NA_KW = 16
ROPE_THETA = 10000.0
Q_ROWS = 4
K_ROWS = Q_ROWS + NA_KH
RMS_EPS = 1e-6
LN_EPS = 1e-5
ADA_CHUNKS = 6
MOD_ROWS = 16

VMEM_LIMIT = 56 * 1024 * 1024

_NT = (((1,), (1,)), ((), ()))

_Q_SCALE = (HEAD_DIM // 2) ** -0.5 * math.log2(math.e)
DIFF_TQ = 1024
DIFF_TK = 1024
DIFF_AHEAD = 1
DIFF_BOUND_MARGIN = 1.01
DIFF_L_MIN = 2.0 ** -100
FNET_NY_ROWS = 16
SUM_ROWS = 16


def _params(*sem):
    return pltpu.CompilerParams(dimension_semantics=sem, vmem_limit_bytes=VMEM_LIMIT)


def _tile(n, pref):
    if n <= pref:
        return n
    t = (pref // 128) * 128
    while n % t:
        t -= 128
    assert t > 0, (n, pref)
    return t


def _ada_kernel(s_ref, w_ref, b_ref, o_ref):
    s = s_ref[...]
    a = (s * jax.nn.sigmoid(s)).astype(BF16)
    o_ref[...] = jnp.dot(a, w_ref[...].astype(BF16), preferred_element_type=F32) + b_ref[...]


def _ada_table(cond, ada_w, ada_b):
    depth, d, n = ada_w.shape
    tn = _tile(n, 1024)
    return pl.pallas_call(
        _ada_kernel,
        out_shape=jax.ShapeDtypeStruct((depth, MOD_ROWS, n), F32),
        grid=(depth, n // tn),
        in_specs=[
            pl.BlockSpec((MOD_ROWS, d), lambda l, j: (0, 0)),
            pl.BlockSpec((None, d, tn), lambda l, j: (l, 0, j)),
            pl.BlockSpec((None, 1, tn), lambda l, j: (l, 0, j)),
        ],
        out_specs=pl.BlockSpec((None, MOD_ROWS, tn), lambda l, j: (l, 0, j)),
        compiler_params=_params("parallel", "parallel"),
        name="ada_table",
    )(cond, ada_w, ada_b.reshape(depth, 1, n))


def _norm_mod_rows(x_ref, g_ref, sh_ref, sc_ref, h_ref, unrolled=False, part=None):
    rows = 64
    g = g_ref[...]
    sc1 = 1.0 + sc_ref[...]
    sh = sh_ref[...]
    total = x_ref.shape[0] // rows
    if part is not None:
        assert unrolled and total % part[1] == 0
        total //= part[1]

    def body(r, carry):
        if part is not None:
            sl = pl.ds(pl.multiple_of((part[0] * total + r) * rows, rows), rows)
        elif unrolled:
            sl = slice(r * rows, (r + 1) * rows)
        else:
            sl = pl.ds(pl.multiple_of(r * rows, rows), rows)
        xv = x_ref[sl, :]
        ms = jnp.mean(xv * xv, axis=-1, keepdims=True)
        y = xv * lax.rsqrt(ms + RMS_EPS) * g
        h_ref[sl, :] = (y * sc1 + sh).astype(h_ref.dtype)
        return carry

    if unrolled:
        for r in range(total):
            body(r, 0)
    else:
        lax.fori_loop(0, total, body, 0)


def _mod_spec(d, layer, chunk, row_fn):
    return pl.BlockSpec((None, None, 1, d), lambda i, j: (layer, row_fn(i), 0, chunk))


def _layer_vec_spec(d, layer):
    return pl.BlockSpec((None, 1, d), lambda i, j: (layer, 0, 0))


def _in_kernel(x0_ref, xn_ref, g_ref, sh0_ref, sc0_ref, shn_ref, scn_ref, w_ref, o_ref, h0_scr, h1_scr):
    i = pl.program_id(0)

    @pl.when(i == 0)
    def _():
        _norm_mod_rows(x0_ref, g_ref, sh0_ref, sc0_ref, h0_scr)

    def step(h_cur, h_next):
        o_ref[...] = jnp.dot(h_cur[...], w_ref[...], preferred_element_type=F32).astype(o_ref.dtype)
        _norm_mod_rows(xn_ref, g_ref, shn_ref, scn_ref, h_next, unrolled=True)

    @pl.when(i % 2 == 0)
    def _():
        step(h0_scr, h1_scr)

    @pl.when(i % 2 == 1)
    def _():
        step(h1_scr, h0_scr)


def _in_proj(x, mods, g, w, layer, row_fn, tm):
    m, d = x.shape
    n = w.shape[1]
    nt = m // tm
    nxt = lambda i: jnp.minimum(i + 1, nt - 1)
    once = pl.Buffered(1)
    return pl.pallas_call(
        _in_kernel,
        out_shape=jax.ShapeDtypeStruct((m, n), BF16),
        grid=(nt, 1),
        in_specs=[
            pl.BlockSpec((tm, d), lambda i, j: (0, 0), pipeline_mode=once),
            pl.BlockSpec((tm, d), lambda i, j: (nxt(i), 0)),
            _layer_vec_spec(d, layer),
            _mod_spec(d, layer, 0, lambda i: row_fn(0)),
            _mod_spec(d, layer, 1, lambda i: row_fn(0)),
            _mod_spec(d, layer, 0, lambda i: row_fn(nxt(i))),
            _mod_spec(d, layer, 1, lambda i: row_fn(nxt(i))),
            pl.BlockSpec((d, n), lambda i, j: (0, 0), pipeline_mode=once),
        ],
        out_specs=pl.BlockSpec((tm, n), lambda i, j: (i, 0)),
        scratch_shapes=[pltpu.VMEM((tm, d), BF16), pltpu.VMEM((tm, d), BF16)],
        compiler_params=_params("arbitrary", "arbitrary"),
        name="in_proj",
    )(x, x, g, mods, mods, mods, mods, w)


def _out_kernel(x_ref, ma_ref, mb_ref, wa_ref, wb_ref, gate_ref, o_ref):
    y = jnp.dot(ma_ref[...], wa_ref[...], preferred_element_type=F32)
    y = y + jnp.dot(mb_ref[...], wb_ref[...], preferred_element_type=F32)
    o_ref[...] = x_ref[...] + gate_ref[...] * y


def _out_proj(x, ma, mb, w, mods, layer, row_fn, tm):
    m, d = x.shape
    ka, kb = ma.shape[1], mb.shape[1]
    assert ka % kb == 0 and w.shape[1:] == (ka + kb, d)
    return pl.pallas_call(
        _out_kernel,
        out_shape=jax.ShapeDtypeStruct((m, d), F32),
        grid=(m // tm, 1),
        in_specs=[
            pl.BlockSpec((tm, d), lambda i, j: (i, 0)),
            pl.BlockSpec((tm, ka), lambda i, j: (i, 0)),
            pl.BlockSpec((tm, kb), lambda i, j: (i, 0)),
            pl.BlockSpec((None, ka, d), lambda i, j: (layer, 0, 0)),
            pl.BlockSpec((None, kb, d), lambda i, j: (layer, ka // kb, 0)),
            _mod_spec(d, layer, 2, row_fn),
        ],
        out_specs=pl.BlockSpec((tm, d), lambda i, j: (i, 0)),
        compiler_params=_params("parallel", "arbitrary"),
        name="out_proj",
    )(x, ma, mb, w, w, mods)


def _mlp_kernel(x_ref, x0_ref, xn_ref, g_ref, sh0_ref, sc0_ref, shn_ref, scn_ref, gate_ref, w1_ref, w2_ref, fg_ref,
                o_ref, h_scr, *, final_norm, n_parts):
    i = pl.program_id(0)
    f = pl.program_id(1)
    nf = n_parts
    slot = i % 2

    @pl.when((i == 0) & (f == 0))
    def _():
        _norm_mod_rows(x0_ref, g_ref, sh0_ref, sc0_ref, h_scr.at[0])

    def step(first, last):
        a = jnp.dot(h_scr[slot], w1_ref[...], preferred_element_type=F32)
        a = jnp.square(jnp.maximum(a, 0.0)).astype(BF16)
        y = jnp.dot(a, w2_ref[...], preferred_element_type=F32)
        if not first:
            y = o_ref[...] + y
        if last:
            y = x_ref[...] + gate_ref[...] * y
            if final_norm:
                y = y * lax.rsqrt(jnp.mean(y * y, axis=-1, keepdims=True) + RMS_EPS) * fg_ref[...]
        o_ref[...] = y
        _norm_mod_rows(xn_ref, g_ref, shn_ref, scn_ref, h_scr.at[1 - slot], unrolled=True, part=(f, n_parts))

    pl.when(f == 0)(lambda: step(True, False))
    pl.when((f > 0) & (f < nf - 1))(lambda: step(False, False))
    pl.when(f == nf - 1)(lambda: step(False, True))


def _mlp(x, mods, g, w1, w2, final_g, layer, row_fn, tm, final_norm):
    m, d = x.shape
    dff = w1.shape[1]
    tf = _tile(dff, 1024)
    nt = m // tm
    assert dff // tf >= 2
    nxt = lambda i: jnp.minimum(i + 1, nt - 1)
    return pl.pallas_call(
        functools.partial(_mlp_kernel, final_norm=final_norm, n_parts=dff // tf),
        out_shape=jax.ShapeDtypeStruct((m, d), F32),
        grid=(nt, dff // tf),
        in_specs=[
            pl.BlockSpec((tm, d), lambda i, j: (i, 0)),
            pl.BlockSpec((tm, d), lambda i, j: (0, 0), pipeline_mode=pl.Buffered(1)),
            pl.BlockSpec((tm, d), lambda i, j: (nxt(i), 0)),
            _layer_vec_spec(d, layer),
            _mod_spec(d, layer, 3, lambda i: row_fn(0)),
            _mod_spec(d, layer, 4, lambda i: row_fn(0)),
            _mod_spec(d, layer, 3, lambda i: row_fn(nxt(i))),
            _mod_spec(d, layer, 4, lambda i: row_fn(nxt(i))),
            _mod_spec(d, layer, 5, row_fn),
            pl.BlockSpec((d, tf), lambda i, j: (0, j)),
            pl.BlockSpec((tf, d), lambda i, j: (j, 0)),
            pl.BlockSpec((1, d), lambda i, j: (0, 0)),
        ],
        out_specs=pl.BlockSpec((tm, d), lambda i, j: (i, 0)),
        scratch_shapes=[pltpu.VMEM((2, tm, d), BF16)],
        compiler_params=_params("arbitrary", "arbitrary"),
        name="mlp",
    )(x, x, x, g, mods, mods, mods, mods, mods, w1, w2, final_g.reshape(1, d))


def _in_ctx_kernel(x_ref, g_ref, sh_ref, sc_ref, w_ref, o_ref, wb_ref, h_scr):
    @pl.when(pl.program_id(0) == 0)
    def _():
        _norm_mod_rows(x_ref, g_ref, sh_ref, sc_ref, h_scr)

    wb = w_ref[...].astype(BF16)
    wb_ref[...] = wb
    o_ref[...] = jnp.dot(h_scr[...], wb, preferred_element_type=F32).astype(o_ref.dtype)


def _ctx_vec_specs(d, layer, row, chunks):
    specs = [pl.BlockSpec((None, 1, d), lambda j: (layer, 0, 0))]
    for chunk in chunks:
        specs.append(pl.BlockSpec((None, None, 1, d), lambda j, chunk=chunk: (layer, row, 0, chunk)))
    return specs


def _in_proj_ctx(x, mods, g, w, layer, row):
    m, d = x.shape
    n = w.shape[2]
    tn = _tile(n, 1024)
    return pl.pallas_call(
        _in_ctx_kernel,
        out_shape=(jax.ShapeDtypeStruct((m, n), BF16), jax.ShapeDtypeStruct((d, n), BF16)),
        grid=(n // tn,),
        in_specs=[pl.BlockSpec((m, d), lambda j: (0, 0), pipeline_mode=pl.Buffered(1))]
        + _ctx_vec_specs(d, layer, row, (0, 1))
        + [pl.BlockSpec((None, d, tn), lambda j: (layer, 0, j))],
        out_specs=(pl.BlockSpec((m, tn), lambda j: (0, j)), pl.BlockSpec((d, tn), lambda j: (0, j))),
        scratch_shapes=[pltpu.VMEM((m, d), BF16)],
        compiler_params=_params("arbitrary"),
        name="in_proj_ctx",
    )(x, g, mods, mods, w)


def _mlp_ctx_kernel(x_ref, g_ref, sh_ref, sc_ref, gate_ref, w1_ref, w2_ref, o_ref, w1b_ref, w2b_ref, h_scr, *, n_parts):
    f = pl.program_id(0)

    @pl.when(f == 0)
    def _():
        _norm_mod_rows(x_ref, g_ref, sh_ref, sc_ref, h_scr)

    def step(first, last):
        w1b = w1_ref[...].astype(BF16)
        w2b = w2_ref[...].astype(BF16)
        w1b_ref[...] = w1b
        w2b_ref[...] = w2b
        a = jnp.dot(h_scr[...], w1b, preferred_element_type=F32)
        a = jnp.square(jnp.maximum(a, 0.0)).astype(BF16)
        y = jnp.dot(a, w2b, preferred_element_type=F32)
        if not first:
            y = o_ref[...] + y
        if last:
            y = x_ref[...] + gate_ref[...] * y
        o_ref[...] = y

    pl.when(f == 0)(lambda: step(True, False))
    pl.when((f > 0) & (f < n_parts - 1))(lambda: step(False, False))
    pl.when(f == n_parts - 1)(lambda: step(False, True))


def _mlp_ctx(x, mods, g, w1, w2, layer, row):
    m, d = x.shape
    dff = w1.shape[2]
    tf = _tile(dff, 512)
    assert dff // tf >= 2
    return pl.pallas_call(
        functools.partial(_mlp_ctx_kernel, n_parts=dff // tf),
        out_shape=(jax.ShapeDtypeStruct((m, d), F32), jax.ShapeDtypeStruct((d, dff), BF16),
                   jax.ShapeDtypeStruct((dff, d), BF16)),
        grid=(dff // tf,),
        in_specs=[pl.BlockSpec((m, d), lambda j: (0, 0), pipeline_mode=pl.Buffered(1))]
        + _ctx_vec_specs(d, layer, row, (3, 4, 5))
        + [pl.BlockSpec((None, d, tf), lambda j: (layer, 0, j)), pl.BlockSpec((None, tf, d), lambda j: (layer, j, 0))],
        out_specs=(pl.BlockSpec((m, d), lambda j: (0, 0), pipeline_mode=pl.Buffered(1)), pl.BlockSpec((d, tf), lambda j: (0, j)),
                   pl.BlockSpec((tf, d), lambda j: (j, 0))),
        scratch_shapes=[pltpu.VMEM((m, d), BF16)],
        compiler_params=_params("arbitrary"),
        name="mlp_ctx",
    )(x, g, mods, mods, mods, w1, w2)


def _sgu_kernel(u_ref, v_ref, gam_ref, ws_ref, b_ref, o_ref, *, groups):
    for c in range(u_ref.shape[0] // CHUNK):
        rs = slice(c * CHUNK, (c + 1) * CHUNK)
        for g in range(groups):
            cs = slice(g * HEAD_DIM, (g + 1) * HEAD_DIM)
            v = v_ref[rs, cs].astype(F32)
            mu = jnp.mean(v, axis=-1, keepdims=True)
            dv = v - mu
            var = jnp.mean(dv * dv, axis=-1, keepdims=True)
            vn = (dv * lax.rsqrt(var + LN_EPS) * gam_ref[:, cs]).astype(BF16)
            mixed = jnp.dot(ws_ref[g], vn, preferred_element_type=F32) + b_ref[g]
            o_ref[rs, cs] = (u_ref[rs, cs].astype(F32) * mixed).astype(o_ref.dtype)


def _sgu(p, gam, ws, b_full, layer, sgu_w):
    m = p.shape[0]
    groups = sgu_w // HEAD_DIM
    tr = _tile(m, 512)
    return pl.pallas_call(
        functools.partial(_sgu_kernel, groups=groups),
        out_shape=jax.ShapeDtypeStruct((m, sgu_w), BF16),
        grid=(m // tr,),
        in_specs=[
            pl.BlockSpec((tr, sgu_w), lambda i: (i, 0)),
            pl.BlockSpec((tr, sgu_w), lambda i: (i, 1)),
            pl.BlockSpec((None, 1, sgu_w), lambda i: (layer, 0, 0)),
            pl.BlockSpec((None, groups, CHUNK, CHUNK), lambda i: (layer, 0, 0, 0)),
            pl.BlockSpec((None, groups, CHUNK, HEAD_DIM), lambda i: (layer, 0, 0, 0)),
        ],
        out_specs=pl.BlockSpec((tr, sgu_w), lambda i: (i, 0)),
        compiler_params=_params("parallel"),
        name="sgu",
    )(p, p, gam, ws, b_full)


def _softmax_pv(scores, values):
    d = values[0].shape[1]
    m = scores[0].max(axis=-1, keepdims=True)
    for s in scores[1:]:
        m = jnp.maximum(m, s.max(axis=-1, keepdims=True))
    o = None
    for s, v in zip(scores, values):
        v1 = jnp.concatenate([v, jnp.ones(v.shape, v.dtype)], axis=1)
        pv = jnp.dot(jnp.exp2(s - m).astype(BF16), v1, preferred_element_type=F32)
        o = pv if o is None else o + pv
    return o[:, :d] / o[:, d:]


_NA_SCALE = HEAD_DIM ** -0.5 * math.log2(math.e)


def _na_kernel(q_ref, k0_ref, k1_ref, k2_ref, v0_ref, v1_ref, v2_ref, kc_ref, vc_ref, bias_ref, o_ref, *, heads):
    qb = q_ref.shape[0]
    scale = _NA_SCALE
    for h in range(heads):
        cs = slice(h * HEAD_DIM, (h + 1) * HEAD_DIM)
        q = (q_ref[:, cs].astype(F32) * scale).astype(BF16)
        scores, values = [], []
        for j, (k_ref, v_ref) in enumerate(((k0_ref, v0_ref), (k1_ref, v1_ref), (k2_ref, v2_ref))):
            s = lax.dot_general(q, k_ref[:, cs], _NT, preferred_element_type=F32)
            scores.append(s + bias_ref[h, :, j * qb:(j + 1) * qb])
            values.append(v_ref[:, cs])
        scores.append(lax.dot_general(q, kc_ref[:, cs], _NT, preferred_element_type=F32))
        values.append(vc_ref[:, cs])
        o_ref[:, cs] = _softmax_pv(scores, values).astype(o_ref.dtype)


def _na_bias(rpb, rows):
    heads = rpb.shape[0]
    kw = NA_KW
    col = jnp.arange(GRID_W)
    col_start = jnp.clip(col - kw // 2, 0, GRID_W - kw)
    col_mask = (col[None, :] >= col_start[:, None]) & (col[None, :] < col_start[:, None] + kw)
    pad = GRID_W - kw
    ext = jnp.pad(rpb.astype(F32), ((0, 0), (0, 0), (pad, pad)), mode="edge")
    toeplitz = jnp.stack([ext[:, :, GRID_W - 1 - qc:2 * GRID_W - 1 - qc] for qc in range(GRID_W)], axis=2)
    rpb_x = jnp.where(col_mask[None, None], toeplitz * math.log2(math.e), -jnp.inf)
    outside = jnp.full((heads, GRID_W, GRID_W), -jnp.inf, F32)
    nblk = rows // Q_ROWS
    blocks = []
    for kblk in (0, 1, nblk - 1):
        ks = min(max(Q_ROWS * kblk - NA_KH // 2, 0), rows - K_ROWS)
        qrows = []
        for i in range(Q_ROWS):
            r = Q_ROWS * kblk + i
            rs = min(max(r - NA_KH // 2, 0), rows - NA_KH)
            tiles = [rpb_x[:, ks + j - r + NA_KH - 1] if rs <= ks + j < rs + NA_KH else outside for j in range(K_ROWS)]
            qrows.append(jnp.concatenate(tiles, axis=-1))
        blocks.append(jnp.concatenate(qrows, axis=1))
    return jnp.stack(blocks)


def _na_lat(p_lat, p_ctx, bias, bsz, seq, ctx_len, na_w, col0):
    heads = na_w // HEAD_DIM
    qb = Q_ROWS * GRID_W
    nblk = seq // qb
    assert nblk >= 4 and K_ROWS == 3 * Q_ROWS

    def kblock(j):
        return lambda b, k: (b * nblk + jnp.clip(k - 1, 0, nblk - 3) + j, col0 + 1)

    def vblock(j):
        return lambda b, k: (b * nblk + jnp.clip(k - 1, 0, nblk - 3) + j, col0 + 2)

    def btype(b, k):
        return (jnp.where(k == 0, 0, jnp.where(k == nblk - 1, 2, 1)), 0, 0, 0)

    return pl.pallas_call(
        functools.partial(_na_kernel, heads=heads),
        out_shape=jax.ShapeDtypeStruct((bsz * seq, na_w), BF16),
        grid=(bsz, nblk),
        in_specs=[
            pl.BlockSpec((qb, na_w), lambda b, k: (b * nblk + k, col0)),
            pl.BlockSpec((qb, na_w), kblock(0)),
            pl.BlockSpec((qb, na_w), kblock(1)),
            pl.BlockSpec((qb, na_w), kblock(2)),
            pl.BlockSpec((qb, na_w), vblock(0)),
            pl.BlockSpec((qb, na_w), vblock(1)),
            pl.BlockSpec((qb, na_w), vblock(2)),
            pl.BlockSpec((ctx_len, na_w), lambda b, k: (b, col0 + 1)),
            pl.BlockSpec((ctx_len, na_w), lambda b, k: (b, col0 + 2)),
            pl.BlockSpec((None, heads, qb, K_ROWS * GRID_W), btype),
        ],
        out_specs=pl.BlockSpec((qb, na_w), lambda b, k: (b * nblk + k, 0)),
        compiler_params=_params("parallel", "arbitrary"),
        name="na_lat",
    )(p_lat, p_lat, p_lat, p_lat, p_lat, p_lat, p_lat, p_ctx, p_ctx, bias)


def _attn_ctx_kernel(q_ref, k_ref, v_ref, o_ref, *, heads):
    for h in range(heads):
        cs = slice(h * HEAD_DIM, (h + 1) * HEAD_DIM)
        q = (q_ref[:, cs].astype(F32) * _NA_SCALE).astype(BF16)
        s = lax.dot_general(q, k_ref[:, cs], _NT, preferred_element_type=F32)
        o_ref[:, cs] = _softmax_pv([s], [v_ref[:, cs]]).astype(o_ref.dtype)


def _na_ctx(p_ctx, bsz, ctx_len, na_w, col0):
    heads = na_w // HEAD_DIM
    return pl.pallas_call(
        functools.partial(_attn_ctx_kernel, heads=heads),
        out_shape=jax.ShapeDtypeStruct((bsz * ctx_len, na_w), BF16),
        grid=(bsz,),
        in_specs=[pl.BlockSpec((ctx_len, na_w), lambda b, c=c: (b, col0 + c)) for c in range(3)],
        out_specs=pl.BlockSpec((ctx_len, na_w), lambda b: (b, 0)),
        compiler_params=_params("parallel"),
        name="na_ctx",
    )(p_ctx, p_ctx, p_ctx)


def _swap16(x):
    lane = lax.broadcasted_iota(jnp.int32, x.shape, 1)
    up = pltpu.roll(x, HEAD_DIM - 16, 1)
    down = pltpu.roll(x, 16, 1)
    return jnp.where((lane & 16) == 0, up, down)


def _diff_kernel(*refs, n_lat, rope, lam_init):
    if rope:
        (q_ref, k_ref, v_ref, kc_ref, vc_ref, cosq_ref, sinq_ref, cos_ref, sin_ref, lam_ref, g_ref,
         o_ref, k_scr, vt_scr, kmax_scr) = refs
    else:
        q_ref, kc_ref, vc_ref, lam_ref, g_ref, o_ref, k_scr, vt_scr, kmax_scr = refs
    n_ctx = kc_ref.shape[0]
    nk = n_lat + n_ctx
    hd = HEAD_DIM
    half = hd // 2

    @pl.when(pl.program_id(2) == 0)
    def _():
        step = 512
        row = lax.broadcasted_iota(jnp.int32, (hd, hd), 0)
        col = lax.broadcasted_iota(jnp.int32, (hd, hd), 1)
        sel = jnp.where(((col == 0) & (row < half)) | ((col == 1) & (row >= half)), 1.0, 0.0).astype(BF16)

        def put_keys(rs, kr, kmax):
            n = kr.shape[0]
            k_scr[rs, :hd] = kr
            lane = lax.broadcasted_iota(jnp.int32, (n, hd), 1)
            k_scr[rs, hd:] = jnp.where(lane == 0, 1.0, 0.0).astype(BF16)
            kf = kr.astype(F32)
            sq = jnp.dot((kf * kf).astype(BF16), sel, preferred_element_type=F32).max(axis=0, keepdims=True)
            return sq if kmax is None else jnp.maximum(kmax, sq)

        kmax = None
        for c in range(n_lat // step):
            rs = slice(c * step, (c + 1) * step)
            k = k_ref[rs, :].astype(F32)
            kmax = put_keys(rs, (k * cos_ref[rs, :] + _swap16(k) * sin_ref[rs, :]).astype(BF16), kmax)
            vt_scr[:hd, rs] = v_ref[rs, :].astype(F32).T.astype(BF16)
        kmax = put_keys(slice(n_lat, nk), kc_ref[...], kmax)
        vt_scr[:hd, n_lat:nk] = vc_ref[...].astype(F32).T.astype(BF16)
        vt_scr[hd:, :] = jnp.ones((SUM_ROWS, nk), BF16)
        lane1 = lax.broadcasted_iota(jnp.int32, kmax.shape, 1)
        for mp in range(2):
            kmax_scr[mp:mp + 1, :] = jnp.broadcast_to(
                jnp.max(jnp.where(lane1 == mp, kmax, 0.0), axis=-1, keepdims=True), kmax.shape)

    tq = q_ref.shape[0]

    def prep_q(qr_ref, cq_ref, sq_ref):
        qraw = qr_ref[...]
        q = qraw.astype(F32)
        row = lax.broadcasted_iota(jnp.int32, (hd, hd), 0)
        col = lax.broadcasted_iota(jnp.int32, (hd, hd), 1)
        sel1 = jnp.where((col == 0) & (row < half), 1.0, 0.0).astype(BF16)
        sel2 = jnp.where((col == 0) & (row >= half), 1.0, 0.0).astype(BF16)
        sq = (q * q).astype(BF16)
        n1 = jnp.dot(sq, sel1, preferred_element_type=F32)
        n2 = jnp.dot(sq, sel2, preferred_element_type=F32)
        if rope:
            perm = jnp.where(row == (col ^ 16), 1.0, 0.0).astype(BF16)
            q = q * cq_ref[...] + jnp.dot(qraw, perm, preferred_element_type=F32) * sq_ref[...]
        else:
            q = q * _Q_SCALE
        lane = lax.broadcasted_iota(jnp.int32, q.shape, 1)
        shift1 = jnp.sqrt(n1 * kmax_scr[0:1, :]) * (-_Q_SCALE * DIFF_BOUND_MARGIN)
        shift2 = jnp.sqrt(n2 * kmax_scr[1:2, :]) * (-_Q_SCALE * DIFF_BOUND_MARGIN)
        return jnp.concatenate([
            jnp.concatenate([jnp.where(lane < half, q, 0.0), shift1], axis=1),
            jnp.concatenate([jnp.where(lane >= half, q, 0.0), shift2], axis=1),
        ], axis=0).astype(BF16)

    q_aug = prep_q(q_ref, cosq_ref, sinq_ref) if rope else prep_q(q_ref, None, None)

    lp = lam_ref[...]
    d1 = jnp.sum(lp[0:1, :] * lp[1:2, :], axis=-1, keepdims=True)
    d2 = jnp.sum(lp[2:3, :] * lp[3:4, :], axis=-1, keepdims=True)
    lam = jnp.exp(d1) - jnp.exp(d2) + lam_init

    chunks = [(k0, min(DIFF_TK, nk - k0)) for k0 in range(0, nk, DIFF_TK)]

    def write_out(ot):
        ot = ot * lax.rsqrt(jnp.mean(ot * ot, axis=0, keepdims=True) + RMS_EPS)
        o_ref[...] = ((ot.T * g_ref[...]) * (1.0 - lam_init)).astype(o_ref.dtype)

    def finish(acc):
        l = acc[hd:hd + 1, :]
        c1 = 1.0 / l[:, :tq]
        c2 = lam / l[:, tq:]
        write_out(acc[:hd, :tq] * c1 - acc[:hd, tq:] * c2)

    def pipelined(scores, consume):
        pending = [scores(c) for c in range(min(DIFF_AHEAD, len(chunks)))]
        for c in range(len(chunks)):
            s = pending.pop(0)
            if c + DIFF_AHEAD < len(chunks):
                pending.append(scores(c + DIFF_AHEAD))
            consume(c, s)

    state = {}

    def fast_scores(c):
        k0, tk = chunks[c]
        return lax.dot_general(k_scr[k0:k0 + tk, :], q_aug, _NT, preferred_element_type=F32)

    def fast_consume(c, s):
        k0, tk = chunks[c]
        pv = jnp.dot(vt_scr[:, k0:k0 + tk], jnp.exp2(s).astype(BF16), preferred_element_type=F32)
        state["acc"] = pv if c == 0 else state["acc"] + pv

    pipelined(fast_scores, fast_consume)
    acc = state["acc"]
    finish(acc)

    @pl.when(jnp.logical_not(jnp.min(acc[hd:hd + 1, :]) > DIFF_L_MIN))
    def _():
        qab = q_aug[:, :hd]
        st = {}

        def safe_scores(c):
            k0, tk = chunks[c]
            return lax.dot_general(k_scr[k0:k0 + tk, :hd], qab, _NT, preferred_element_type=F32)

        def safe_consume(c, s):
            k0, tk = chunks[c]
            mc = s.max(axis=0, keepdims=True)
            m_new = mc if c == 0 else jnp.maximum(st["m"], mc)
            pv = jnp.dot(vt_scr[:, k0:k0 + tk], jnp.exp2(s - m_new).astype(BF16), preferred_element_type=F32)
            st["acc"] = pv if c == 0 else st["acc"] * jnp.exp2(st["m"] - m_new) + pv
            st["m"] = m_new

        pipelined(safe_scores, safe_consume)
        finish(st["acc"])


def _diff_lat(p_lat, p_ctx, tabs, lam_p, sub_g, bsz, seq, ctx_len, heads, lam_init):
    tq = DIFF_TQ
    nq = seq // tq
    hd = HEAD_DIM
    nk = seq + ctx_len
    cosq, sinq, cos, sin = tabs
    return pl.pallas_call(
        functools.partial(_diff_kernel, n_lat=seq, rope=True, lam_init=lam_init),
        out_shape=jax.ShapeDtypeStruct((bsz * seq, heads * hd), BF16),
        grid=(bsz, heads, nq),
        in_specs=[
            pl.BlockSpec((tq, hd), lambda b, h, i: (b * nq + i, h)),
            pl.BlockSpec((seq, hd), lambda b, h, i: (b, heads + h)),
            pl.BlockSpec((seq, hd), lambda b, h, i: (b, 2 * heads + h)),
            pl.BlockSpec((ctx_len, hd), lambda b, h, i: (b, heads + h)),
            pl.BlockSpec((ctx_len, hd), lambda b, h, i: (b, 2 * heads + h)),
            pl.BlockSpec((tq, hd), lambda b, h, i: (i, 0)),
            pl.BlockSpec((tq, hd), lambda b, h, i: (i, 0)),
            pl.BlockSpec((seq, hd), lambda b, h, i: (0, 0), pipeline_mode=pl.Buffered(1)),
            pl.BlockSpec((seq, hd), lambda b, h, i: (0, 0), pipeline_mode=pl.Buffered(1)),
            pl.BlockSpec((4, hd // 2), lambda b, h, i: (0, 0)),
            pl.BlockSpec((1, hd), lambda b, h, i: (0, 0)),
        ],
        out_specs=pl.BlockSpec((tq, hd), lambda b, h, i: (b * nq + i, h)),
        scratch_shapes=[pltpu.VMEM((nk, 2 * hd), BF16), pltpu.VMEM((hd + SUM_ROWS, nk), BF16), pltpu.VMEM((8, hd), F32)],
        compiler_params=_params("parallel", "parallel", "arbitrary"),
        name="diff_lat",
    )(p_lat, p_lat, p_lat, p_ctx, p_ctx, cosq, sinq, cos, sin, lam_p, sub_g.reshape(1, hd))


def _diff_ctx(p_ctx, lam_p, sub_g, bsz, ctx_len, heads, lam_init):
    hd = HEAD_DIM
    return pl.pallas_call(
        functools.partial(_diff_kernel, n_lat=0, rope=False, lam_init=lam_init),
        out_shape=jax.ShapeDtypeStruct((bsz * ctx_len, heads * hd), BF16),
        grid=(bsz, heads, 1),
        in_specs=[
            pl.BlockSpec((ctx_len, hd), lambda b, h, i: (b, h)),
            pl.BlockSpec((ctx_len, hd), lambda b, h, i: (b, heads + h)),
            pl.BlockSpec((ctx_len, hd), lambda b, h, i: (b, 2 * heads + h)),
            pl.BlockSpec((4, hd // 2), lambda b, h, i: (0, 0)),
            pl.BlockSpec((1, hd), lambda b, h, i: (0, 0)),
        ],
        out_specs=pl.BlockSpec((ctx_len, hd), lambda b, h, i: (b, h)),
        scratch_shapes=[pltpu.VMEM((ctx_len, 2 * hd), BF16), pltpu.VMEM((hd + SUM_ROWS, ctx_len), BF16),
                        pltpu.VMEM((8, hd), F32)],
        compiler_params=_params("parallel", "parallel", "arbitrary"),
        name="diff_ctx",
    )(p_ctx, p_ctx, p_ctx, lam_p, sub_g.reshape(1, hd))


def _rope_tables(seq):
    axis = HEAD_DIM // 4
    t = jnp.arange(seq)
    row = (t // GRID_W).astype(F32)
    col = (t % GRID_W).astype(F32)
    inv = ROPE_THETA ** (-jnp.arange(0, axis, 2, dtype=F32) / axis)
    ang_r = row[:, None] * inv
    ang_c = col[:, None] * inv
    cr, sr, cc, sc = jnp.cos(ang_r), jnp.sin(ang_r), jnp.cos(ang_c), jnp.sin(ang_c)
    cos = jnp.concatenate([cr, cr, cc, cc] * 2, axis=-1)
    sin = jnp.concatenate([-sr, sr, -sc, sc] * 2, axis=-1)
    return cos * _Q_SCALE, sin * _Q_SCALE, cos, sin


def _fnet_kernel(f_ref, cn_ref, sn_ref, cc_ref, sc_ref, sgn_ref, od_ref, os_ref, ny_ref, a_scr, b_scr, *, norm):
    @pl.when(pl.program_id(1) == 0)
    def _():
        step = 512 if f_ref.shape[0] % 512 == 0 else f_ref.shape[0]
        for c in range(f_ref.shape[0] // step):
            rs = slice(c * step, (c + 1) * step)
            f = f_ref[rs, :]
            a_scr[rs, :] = jnp.dot(f, cc_ref[...], preferred_element_type=F32).astype(BF16)
            b_scr[rs, :] = jnp.dot(f, sc_ref[...], preferred_element_type=F32).astype(BF16)
        ny_ref[...] = (jnp.dot(sgn_ref[...], a_scr[...], preferred_element_type=F32) * norm).astype(ny_ref.dtype)

    p = jnp.dot(cn_ref[...], a_scr[...], preferred_element_type=F32)
    q = jnp.dot(sn_ref[...], b_scr[...], preferred_element_type=F32)
    od_ref[...] = ((p - q) * norm).astype(od_ref.dtype)
    os_ref[...] = ((p + q) * norm).astype(os_ref.dtype)


def _fnet(p, cn, sn, ccb, scb, bsz, n, fw, col):
    half = n // 2
    tr = _tile(half, 512)
    nt = half // tr
    norm = 1.0 / math.sqrt(n * HEAD_DIM)
    sgn = jnp.asarray(np.tile(np.where(np.arange(n) % 2 == 0, 1.0, -1.0), (FNET_NY_ROWS, 1)), BF16)
    yd, ys, ny = pl.pallas_call(
        functools.partial(_fnet_kernel, norm=norm),
        out_shape=(jax.ShapeDtypeStruct((bsz * half, fw), BF16), jax.ShapeDtypeStruct((bsz * half, fw), BF16),
                   jax.ShapeDtypeStruct((bsz * FNET_NY_ROWS, fw), BF16)),
        grid=(bsz, nt),
        in_specs=[
            pl.BlockSpec((n, fw), lambda b, i: (b, col)),
            pl.BlockSpec((tr, n), lambda b, i: (i, 0)),
            pl.BlockSpec((tr, n), lambda b, i: (i, 0)),
            pl.BlockSpec((fw, fw), lambda b, i: (0, 0)),
            pl.BlockSpec((fw, fw), lambda b, i: (0, 0)),
            pl.BlockSpec((FNET_NY_ROWS, n), lambda b, i: (0, 0)),
        ],
        out_specs=(pl.BlockSpec((tr, fw), lambda b, i: (b * nt + i, 0)),
                   pl.BlockSpec((tr, fw), lambda b, i: (b * nt + i, 0)),
                   pl.BlockSpec((FNET_NY_ROWS, fw), lambda b, i: (b, 0))),
        scratch_shapes=[pltpu.VMEM((n, fw), BF16), pltpu.VMEM((n, fw), BF16)],
        compiler_params=_params("parallel", "arbitrary"),
        name="fnet",
    )(p, cn, sn, ccb, scb, sgn)
    y = jnp.concatenate([yd.reshape(bsz, half, fw), ny.reshape(bsz, FNET_NY_ROWS, fw)[:, :1],
                         jnp.flip(ys.reshape(bsz, half, fw)[:, 1:], axis=1)], axis=1)
    return y.reshape(bsz * n, fw)


def _dft_tables(n):
    if n <= 256:
        idx = np.outer(np.arange(n // 2), np.arange(n)) % n
        ang = 2.0 * np.pi * idx / n
        return jnp.asarray(np.cos(ang), F32).astype(BF16), jnp.asarray(np.sin(ang), F32).astype(BF16)
    r = int(round(math.sqrt(n)))
    assert r * r == n and r % 2 == 0
    k = np.arange(n)
    ang_a = 2.0 * np.pi * (np.outer(np.arange(r // 2), k) % r) / r
    ang_b = 2.0 * np.pi * (np.outer(np.arange(r), k) % n) / n
    ca, sa = jnp.asarray(np.cos(ang_a), F32)[:, None, :], jnp.asarray(np.sin(ang_a), F32)[:, None, :]
    cb, sb = jnp.asarray(np.cos(ang_b), F32)[None, :, :], jnp.asarray(np.sin(ang_b), F32)[None, :, :]
    cos = (ca * cb - sa * sb).reshape(n // 2, n).astype(BF16)
    sin = (sa * cb + ca * sb).reshape(n // 2, n).astype(BF16)
    return cos, sin


def _channel_dft_tables(fw):
    idx = np.outer(np.arange(HEAD_DIM), np.arange(HEAD_DIM)) % HEAD_DIM
    ang = 2.0 * np.pi * idx / HEAD_DIM
    eye = np.eye(fw // HEAD_DIM)
    return (jnp.asarray(np.kron(eye, np.cos(ang)), F32).astype(BF16),
            jnp.asarray(np.kron(eye, np.sin(ang)), F32).astype(BF16))


def kernel(x, c, ctx, c_ctx, ada_w, ada_b, norm1_g, norm2_g, w_in, w_out, sgu_norm_g, sgu_w, sgu_b, na_rpb,
           diff_lq1, diff_lk1, diff_lq2, diff_lk2, diff_subln_g, mlp_w1, mlp_w2, final_g):
    bsz, seq, d = x.shape
    ctx_len = ctx.shape[1]
    depth = ada_w.shape[0]
    rows = seq // GRID_W
    mix_heads = d // HEAD_DIM
    sgu_w_dim = (mix_heads // 2) * HEAD_DIM
    na_w = d - sgu_w_dim
    diff_heads = (mix_heads * 3) // 4
    fnet_w = d - diff_heads * HEAD_DIM
    assert sgu_w_dim == na_w and bsz + 1 <= MOD_ROWS

    cond = jnp.zeros((MOD_ROWS, d), F32).at[:bsz].set(c).at[bsz].set(c_ctx)
    mods = _ada_table(cond, ada_w, ada_b).reshape(depth, MOD_ROWS, 1, ADA_CHUNKS * d)

    w_out_b = w_out.astype(BF16)
    sgu_w_b = sgu_w.astype(BF16)
    n1g = norm1_g.reshape(depth, 1, d)
    n2g = norm2_g.reshape(depth, 1, d)
    sgu_g = sgu_norm_g.reshape(-1, 1, sgu_w_dim)
    sgu_b_full = jnp.broadcast_to(sgu_b[..., None], sgu_b.shape + (HEAD_DIM,)).astype(F32)

    tm_lat = _tile(seq, 512)
    tm_ctx = _tile(bsz * ctx_len, 512)
    lat_row = lambda i: (i * tm_lat) // seq
    ctx_row = lambda i: bsz

    x_lat = x.reshape(bsz * seq, d)
    x_ctx = ctx.reshape(bsz * ctx_len, d)

    if depth > 1:
        rope_tabs = _rope_tables(seq)
        cn, sn = _dft_tables(seq)
        cn_c, sn_c = _dft_tables(ctx_len)
        ccb, scb = _channel_dft_tables(fnet_w)

    for l in range(depth):
        need_ctx = l < depth - 1
        p_ctx, w_in_l = _in_proj_ctx(x_ctx, mods, n1g, w_in, l, bsz)
        p_lat = _in_proj(x_lat, mods, n1g, w_in_l, l, lat_row, tm_lat)
        i = l // 2
        ma_ctx = mb_ctx = None
        if l % 2 == 0:
            bias = _na_bias(na_rpb[i], rows)
            ma_lat = _sgu(p_lat, sgu_g, sgu_w_b, sgu_b_full, i, sgu_w_dim)
            mb_lat = _na_lat(p_lat, p_ctx, bias, bsz, seq, ctx_len, na_w, 2)
            if need_ctx:
                ma_ctx = _sgu(p_ctx, sgu_g, sgu_w_b, sgu_b_full, i, sgu_w_dim)
                mb_ctx = _na_ctx(p_ctx, bsz, ctx_len, na_w, 2)
        else:
            lam_init = 0.8 - 0.6 * math.exp(-0.3 * l)
            lam_p = jnp.stack([diff_lq1[i], diff_lk1[i], diff_lq2[i], diff_lk2[i]]).astype(F32)
            fcol = (3 * diff_heads * HEAD_DIM) // fnet_w
            ma_lat = _diff_lat(p_lat, p_ctx, rope_tabs, lam_p, diff_subln_g[i], bsz, seq, ctx_len, diff_heads, lam_init)
            mb_lat = _fnet(p_lat, cn, sn, ccb, scb, bsz, seq, fnet_w, fcol)
            if need_ctx:
                ma_ctx = _diff_ctx(p_ctx, lam_p, diff_subln_g[i], bsz, ctx_len, diff_heads, lam_init)
                mb_ctx = _fnet(p_ctx, cn_c, sn_c, ccb, scb, bsz, ctx_len, fnet_w, fcol)
        if need_ctx:
            x_ctx = _out_proj(x_ctx, ma_ctx, mb_ctx, w_out_b, mods, l, ctx_row, tm_ctx)
            x_ctx, w1_l, w2_l = _mlp_ctx(x_ctx, mods, n2g, mlp_w1, mlp_w2, l, bsz)
        else:
            w1_l, w2_l = mlp_w1[l].astype(BF16), mlp_w2[l].astype(BF16)
        x_lat = _out_proj(x_lat, ma_lat, mb_lat, w_out_b, mods, l, lat_row, tm_lat)
        x_lat = _mlp(x_lat, mods, n2g, w1_l, w2_l, final_g, l, lat_row, tm_lat, final_norm=not need_ctx)
    return x_lat.reshape(bsz, seq, d)
```

```python
import functools
import math

import numpy as np
import jax
import jax.numpy as jnp
from jax import lax
from jax.experimental import pallas as pl
from jax.experimental.pallas import tpu as pltpu

F32 = jnp.float32
BF16 = jnp.bfloat16

GRID_W = 64
HEAD_DIM = 128
CHUNK = 128
NA_KH = 8
NA_KW = 16
ROPE_THETA = 10000.0
Q_ROWS = 4
K_ROWS = Q_ROWS + NA_KH
RMS_EPS = 1e-6
LN_EPS = 1e-5
ADA_CHUNKS = 6
MOD_ROWS = 16

VMEM_LIMIT = 56 * 1024 * 1024

_NT = (((1,), (1,)), ((), ()))

_Q_SCALE = (HEAD_DIM // 2) ** -0.5 * math.log2(math.e)
DIFF_TQ = 1024
DIFF_TK = 1024
DIFF_AHEAD = 1
DIFF_BOUND_MARGIN = 1.01
DIFF_L_MIN = 2.0 ** -100
FNET_NY_ROWS = 16
SUM_ROWS = 16


def _params(*sem):
    return pltpu.CompilerParams(dimension_semantics=sem, vmem_limit_bytes=VMEM_LIMIT)


def _tile(n, pref):
    if n <= pref:
        return n
    t = (pref // 128) * 128
    while n % t:
        t -= 128
    assert t > 0, (n, pref)
    return t


def _ada_kernel(s_ref, w_ref, b_ref, o_ref):
    s = s_ref[...]
    a = (s * jax.nn.sigmoid(s)).astype(BF16)
    o_ref[...] = jnp.dot(a, w_ref[...].astype(BF16), preferred_element_type=F32) + b_ref[...]


def _ada_table(cond, ada_w, ada_b):
    depth, d, n = ada_w.shape
    tn = _tile(n, 1024)
    return pl.pallas_call(
        _ada_kernel,
        out_shape=jax.ShapeDtypeStruct((depth, MOD_ROWS, n), F32),
        grid=(depth, n // tn),
        in_specs=[
            pl.BlockSpec((MOD_ROWS, d), lambda l, j: (0, 0)),
            pl.BlockSpec((None, d, tn), lambda l, j: (l, 0, j)),
            pl.BlockSpec((None, 1, tn), lambda l, j: (l, 0, j)),
        ],
        out_specs=pl.BlockSpec((None, MOD_ROWS, tn), lambda l, j: (l, 0, j)),
        compiler_params=_params("parallel", "parallel"),
        name="ada_table",
    )(cond, ada_w, ada_b.reshape(depth, 1, n))


def _norm_mod_rows(x_ref, g_ref, sh_ref, sc_ref, h_ref, unrolled=False, part=None):
    rows = 64
    g = g_ref[...]
    sc1 = 1.0 + sc_ref[...]
    sh = sh_ref[...]
    total = x_ref.shape[0] // rows
    if part is not None:
        assert unrolled and total % part[1] == 0
        total //= part[1]

    def body(r, carry):
        if part is not None:
            sl = pl.ds(pl.multiple_of((part[0] * total + r) * rows, rows), rows)
        elif unrolled:
            sl = slice(r * rows, (r + 1) * rows)
        else:
            sl = pl.ds(pl.multiple_of(r * rows, rows), rows)
        xv = x_ref[sl, :]
        ms = jnp.mean(xv * xv, axis=-1, keepdims=True)
        y = xv * lax.rsqrt(ms + RMS_EPS) * g
        h_ref[sl, :] = (y * sc1 + sh).astype(h_ref.dtype)
        return carry

    if unrolled:
        for r in range(total):
            body(r, 0)
    else:
        lax.fori_loop(0, total, body, 0)


def _mod_spec(d, layer, chunk, row_fn):
    return pl.BlockSpec((None, None, 1, d), lambda i, j: (layer, row_fn(i), 0, chunk))


def _layer_vec_spec(d, layer):
    return pl.BlockSpec((None, 1, d), lambda i, j: (layer, 0, 0))


def _in_kernel(x0_ref, xn_ref, g_ref, sh0_ref, sc0_ref, shn_ref, scn_ref, w_ref, o_ref, h0_scr, h1_scr):
    i = pl.program_id(0)

    @pl.when(i == 0)
    def _():
        _norm_mod_rows(x0_ref, g_ref, sh0_ref, sc0_ref, h0_scr)

    def step(h_cur, h_next):
        o_ref[...] = jnp.dot(h_cur[...], w_ref[...], preferred_element_type=F32).astype(o_ref.dtype)
        _norm_mod_rows(xn_ref, g_ref, shn_ref, scn_ref, h_next, unrolled=True)

    @pl.when(i % 2 == 0)
    def _():
        step(h0_scr, h1_scr)

    @pl.when(i % 2 == 1)
    def _():
        step(h1_scr, h0_scr)


def _in_proj(x, mods, g, w, layer, row_fn, tm):
    m, d = x.shape
    n = w.shape[1]
    nt = m // tm
    nxt = lambda i: jnp.minimum(i + 1, nt - 1)
    once = pl.Buffered(1)
    return pl.pallas_call(
        _in_kernel,
        out_shape=jax.ShapeDtypeStruct((m, n), BF16),
        grid=(nt, 1),
        in_specs=[
            pl.BlockSpec((tm, d), lambda i, j: (0, 0), pipeline_mode=once),
            pl.BlockSpec((tm, d), lambda i, j: (nxt(i), 0)),
            _layer_vec_spec(d, layer),
            _mod_spec(d, layer, 0, lambda i: row_fn(0)),
            _mod_spec(d, layer, 1, lambda i: row_fn(0)),
            _mod_spec(d, layer, 0, lambda i: row_fn(nxt(i))),
            _mod_spec(d, layer, 1, lambda i: row_fn(nxt(i))),
            pl.BlockSpec((d, n), lambda i, j: (0, 0), pipeline_mode=once),
        ],
        out_specs=pl.BlockSpec((tm, n), lambda i, j: (i, 0)),
        scratch_shapes=[pltpu.VMEM((tm, d), BF16), pltpu.VMEM((tm, d), BF16)],
        compiler_params=_params("arbitrary", "arbitrary"),
        name="in_proj",
    )(x, x, g, mods, mods, mods, mods, w)


def _out_kernel(x_ref, ma_ref, mb_ref, wa_ref, wb_ref, gate_ref, o_ref):
    y = jnp.dot(ma_ref[...], wa_ref[...], preferred_element_type=F32)
    y = y + jnp.dot(mb_ref[...], wb_ref[...], preferred_element_type=F32)
    o_ref[...] = x_ref[...] + gate_ref[...] * y


def _out_proj(x, ma, mb, w, mods, layer, row_fn, tm):
    m, d = x.shape
    ka, kb = ma.shape[1], mb.shape[1]
    assert ka % kb == 0 and w.shape[1:] == (ka + kb, d)
    return pl.pallas_call(
        _out_kernel,
        out_shape=jax.ShapeDtypeStruct((m, d), F32),
        grid=(m // tm, 1),
        in_specs=[
            pl.BlockSpec((tm, d), lambda i, j: (i, 0)),
            pl.BlockSpec((tm, ka), lambda i, j: (i, 0)),
            pl.BlockSpec((tm, kb), lambda i, j: (i, 0)),
            pl.BlockSpec((None, ka, d), lambda i, j: (layer, 0, 0)),
            pl.BlockSpec((None, kb, d), lambda i, j: (layer, ka // kb, 0)),
            _mod_spec(d, layer, 2, row_fn),
        ],
        out_specs=pl.BlockSpec((tm, d), lambda i, j: (i, 0)),
        compiler_params=_params("parallel", "arbitrary"),
        name="out_proj",
    )(x, ma, mb, w, w, mods)


def _mlp_kernel(x_ref, x0_ref, xn_ref, g_ref, sh0_ref, sc0_ref, shn_ref, scn_ref, gate_ref, w1_ref, w2_ref, fg_ref,
                o_ref, h_scr, *, final_norm, n_parts):
    i = pl.program_id(0)
    f = pl.program_id(1)
    nf = n_parts
    slot = i % 2

    @pl.when((i == 0) & (f == 0))
    def _():
        _norm_mod_rows(x0_ref, g_ref, sh0_ref, sc0_ref, h_scr.at[0])

    def step(first, last):
        a = jnp.dot(h_scr[slot], w1_ref[...], preferred_element_type=F32)
        a = jnp.square(jnp.maximum(a, 0.0)).astype(BF16)
        y = jnp.dot(a, w2_ref[...], preferred_element_type=F32)
        if not first:
            y = o_ref[...] + y
        if last:
            y = x_ref[...] + gate_ref[...] * y
            if final_norm:
                y = y * lax.rsqrt(jnp.mean(y * y, axis=-1, keepdims=True) + RMS_EPS) * fg_ref[...]
        o_ref[...] = y
        _norm_mod_rows(xn_ref, g_ref, shn_ref, scn_ref, h_scr.at[1 - slot], unrolled=True, part=(f, n_parts))

    pl.when(f == 0)(lambda: step(True, False))
    pl.when((f > 0) & (f < nf - 1))(lambda: step(False, False))
    pl.when(f == nf - 1)(lambda: step(False, True))


def _mlp(x, mods, g, w1, w2, final_g, layer, row_fn, tm, final_norm):
    m, d = x.shape
    dff = w1.shape[1]
    tf = _tile(dff, 1024)
    nt = m // tm
    assert dff // tf >= 2
    nxt = lambda i: jnp.minimum(i + 1, nt - 1)
    return pl.pallas_call(
        functools.partial(_mlp_kernel, final_norm=final_norm, n_parts=dff // tf),
        out_shape=jax.ShapeDtypeStruct((m, d), F32),
        grid=(nt, dff // tf),
        in_specs=[
            pl.BlockSpec((tm, d), lambda i, j: (i, 0)),
            pl.BlockSpec((tm, d), lambda i, j: (0, 0), pipeline_mode=pl.Buffered(1)),
            pl.BlockSpec((tm, d), lambda i, j: (nxt(i), 0)),
            _layer_vec_spec(d, layer),
            _mod_spec(d, layer, 3, lambda i: row_fn(0)),
            _mod_spec(d, layer, 4, lambda i: row_fn(0)),
            _mod_spec(d, layer, 3, lambda i: row_fn(nxt(i))),
            _mod_spec(d, layer, 4, lambda i: row_fn(nxt(i))),
            _mod_spec(d, layer, 5, row_fn),
            pl.BlockSpec((d, tf), lambda i, j: (0, j)),
            pl.BlockSpec((tf, d), lambda i, j: (j, 0)),
            pl.BlockSpec((1, d), lambda i, j: (0, 0)),
        ],
        out_specs=pl.BlockSpec((tm, d), lambda i, j: (i, 0)),
        scratch_shapes=[pltpu.VMEM((2, tm, d), BF16)],
        compiler_params=_params("arbitrary", "arbitrary"),
        name="mlp",
    )(x, x, x, g, mods, mods, mods, mods, mods, w1, w2, final_g.reshape(1, d))


def _in_ctx_kernel(x_ref, g_ref, sh_ref, sc_ref, w_ref, o_ref, wb_ref, h_scr):
    @pl.when(pl.program_id(0) == 0)
    def _():
        _norm_mod_rows(x_ref, g_ref, sh_ref, sc_ref, h_scr)

    wb = w_ref[...].astype(BF16)
    wb_ref[...] = wb
    o_ref[...] = jnp.dot(h_scr[...], wb, preferred_element_type=F32).astype(o_ref.dtype)


def _ctx_vec_specs(d, layer, row, chunks):
    specs = [pl.BlockSpec((None, 1, d), lambda j: (layer, 0, 0))]
    for chunk in chunks:
        specs.append(pl.BlockSpec((None, None, 1, d), lambda j, chunk=chunk: (layer, row, 0, chunk)))
    return specs


def _in_proj_ctx(x, mods, g, w, layer, row):
    m, d = x.shape
    n = w.shape[2]
    tn = _tile(n, 1024)
    return pl.pallas_call(
        _in_ctx_kernel,
        out_shape=(jax.ShapeDtypeStruct((m, n), BF16), jax.ShapeDtypeStruct((d, n), BF16)),
        grid=(n // tn,),
        in_specs=[pl.BlockSpec((m, d), lambda j: (0, 0), pipeline_mode=pl.Buffered(1))]
        + _ctx_vec_specs(d, layer, row, (0, 1))
        + [pl.BlockSpec((None, d, tn), lambda j: (layer, 0, j))],
        out_specs=(pl.BlockSpec((m, tn), lambda j: (0, j)), pl.BlockSpec((d, tn), lambda j: (0, j))),
        scratch_shapes=[pltpu.VMEM((m, d), BF16)],
        compiler_params=_params("arbitrary"),
        name="in_proj_ctx",
    )(x, g, mods, mods, w)


def _mlp_ctx_kernel(x_ref, g_ref, sh_ref, sc_ref, gate_ref, w1_ref, w2_ref, o_ref, w1b_ref, w2b_ref, h_scr, *, n_parts):
    f = pl.program_id(0)

    @pl.when(f == 0)
    def _():
        _norm_mod_rows(x_ref, g_ref, sh_ref, sc_ref, h_scr)

    def step(first, last):
        w1b = w1_ref[...].astype(BF16)
        w2b = w2_ref[...].astype(BF16)
        w1b_ref[...] = w1b
        w2b_ref[...] = w2b
        a = jnp.dot(h_scr[...], w1b, preferred_element_type=F32)
        a = jnp.square(jnp.maximum(a, 0.0)).astype(BF16)
        y = jnp.dot(a, w2b, preferred_element_type=F32)
        if not first:
            y = o_ref[...] + y
        if last:
            y = x_ref[...] + gate_ref[...] * y
        o_ref[...] = y

    pl.when(f == 0)(lambda: step(True, False))
    pl.when((f > 0) & (f < n_parts - 1))(lambda: step(False, False))
    pl.when(f == n_parts - 1)(lambda: step(False, True))


def _mlp_ctx(x, mods, g, w1, w2, layer, row):
    m, d = x.shape
    dff = w1.shape[2]
    tf = _tile(dff, 512)
    assert dff // tf >= 2
    return pl.pallas_call(
        functools.partial(_mlp_ctx_kernel, n_parts=dff // tf),
        out_shape=(jax.ShapeDtypeStruct((m, d), F32), jax.ShapeDtypeStruct((d, dff), BF16),
                   jax.ShapeDtypeStruct((dff, d), BF16)),
        grid=(dff // tf,),
        in_specs=[pl.BlockSpec((m, d), lambda j: (0, 0), pipeline_mode=pl.Buffered(1))]
        + _ctx_vec_specs(d, layer, row, (3, 4, 5))
        + [pl.BlockSpec((None, d, tf), lambda j: (layer, 0, j)), pl.BlockSpec((None, tf, d), lambda j: (layer, j, 0))],
        out_specs=(pl.BlockSpec((m, d), lambda j: (0, 0), pipeline_mode=pl.Buffered(1)), pl.BlockSpec((d, tf), lambda j: (0, j)),
                   pl.BlockSpec((tf, d), lambda j: (j, 0))),
        scratch_shapes=[pltpu.VMEM((m, d), BF16)],
        compiler_params=_params("arbitrary"),
        name="mlp_ctx",
    )(x, g, mods, mods, mods, w1, w2)


def _sgu_kernel(u_ref, v_ref, gam_ref, ws_ref, b_ref, o_ref, *, groups):
    for c in range(u_ref.shape[0] // CHUNK):
        rs = slice(c * CHUNK, (c + 1) * CHUNK)
        for g in range(groups):
            cs = slice(g * HEAD_DIM, (g + 1) * HEAD_DIM)
            v = v_ref[rs, cs].astype(F32)
            mu = jnp.mean(v, axis=-1, keepdims=True)
            dv = v - mu
            var = jnp.mean(dv * dv, axis=-1, keepdims=True)
            vn = (dv * lax.rsqrt(var + LN_EPS) * gam_ref[:, cs]).astype(BF16)
            mixed = jnp.dot(ws_ref[g], vn, preferred_element_type=F32) + b_ref[g]
            o_ref[rs, cs] = (u_ref[rs, cs].astype(F32) * mixed).astype(o_ref.dtype)


def _sgu(p, gam, ws, b_full, layer, sgu_w):
    m = p.shape[0]
    groups = sgu_w // HEAD_DIM
    tr = _tile(m, 512)
    return pl.pallas_call(
        functools.partial(_sgu_kernel, groups=groups),
        out_shape=jax.ShapeDtypeStruct((m, sgu_w), BF16),
        grid=(m // tr,),
        in_specs=[
            pl.BlockSpec((tr, sgu_w), lambda i: (i, 0)),
            pl.BlockSpec((tr, sgu_w), lambda i: (i, 1)),
            pl.BlockSpec((None, 1, sgu_w), lambda i: (layer, 0, 0)),
            pl.BlockSpec((None, groups, CHUNK, CHUNK), lambda i: (layer, 0, 0, 0)),
            pl.BlockSpec((None, groups, CHUNK, HEAD_DIM), lambda i: (layer, 0, 0, 0)),
        ],
        out_specs=pl.BlockSpec((tr, sgu_w), lambda i: (i, 0)),
        compiler_params=_params("parallel"),
        name="sgu",
    )(p, p, gam, ws, b_full)


def _softmax_pv(scores, values):
    d = values[0].shape[1]
    m = scores[0].max(axis=-1, keepdims=True)
    for s in scores[1:]:
        m = jnp.maximum(m, s.max(axis=-1, keepdims=True))
    o = None
    for s, v in zip(scores, values):
        v1 = jnp.concatenate([v, jnp.ones(v.shape, v.dtype)], axis=1)
        pv = jnp.dot(jnp.exp2(s - m).astype(BF16), v1, preferred_element_type=F32)
        o = pv if o is None else o + pv
    return o[:, :d] / o[:, d:]


_NA_SCALE = HEAD_DIM ** -0.5 * math.log2(math.e)


def _na_kernel(q_ref, k0_ref, k1_ref, k2_ref, v0_ref, v1_ref, v2_ref, kc_ref, vc_ref, bias_ref, o_ref, *, heads):
    qb = q_ref.shape[0]
    scale = _NA_SCALE
    for h in range(heads):
        cs = slice(h * HEAD_DIM, (h + 1) * HEAD_DIM)
        q = (q_ref[:, cs].astype(F32) * scale).astype(BF16)
        scores, values = [], []
        for j, (k_ref, v_ref) in enumerate(((k0_ref, v0_ref), (k1_ref, v1_ref), (k2_ref, v2_ref))):
            s = lax.dot_general(q, k_ref[:, cs], _NT, preferred_element_type=F32)
            scores.append(s + bias_ref[h, :, j * qb:(j + 1) * qb])
            values.append(v_ref[:, cs])
        scores.append(lax.dot_general(q, kc_ref[:, cs], _NT, preferred_element_type=F32))
        values.append(vc_ref[:, cs])
        o_ref[:, cs] = _softmax_pv(scores, values).astype(o_ref.dtype)


def _na_bias(rpb, rows):
    heads = rpb.shape[0]
    kw = NA_KW
    col = jnp.arange(GRID_W)
    col_start = jnp.clip(col - kw // 2, 0, GRID_W - kw)
    col_mask = (col[None, :] >= col_start[:, None]) & (col[None, :] < col_start[:, None] + kw)
    pad = GRID_W - kw
    ext = jnp.pad(rpb.astype(F32), ((0, 0), (0, 0), (pad, pad)), mode="edge")
    toeplitz = jnp.stack([ext[:, :, GRID_W - 1 - qc:2 * GRID_W - 1 - qc] for qc in range(GRID_W)], axis=2)
    rpb_x = jnp.where(col_mask[None, None], toeplitz * math.log2(math.e), -jnp.inf)
    outside = jnp.full((heads, GRID_W, GRID_W), -jnp.inf, F32)
    nblk = rows // Q_ROWS
    blocks = []
    for kblk in (0, 1, nblk - 1):
        ks = min(max(Q_ROWS * kblk - NA_KH // 2, 0), rows - K_ROWS)
        qrows = []
        for i in range(Q_ROWS):
            r = Q_ROWS * kblk + i
            rs = min(max(r - NA_KH // 2, 0), rows - NA_KH)
            tiles = [rpb_x[:, ks + j - r + NA_KH - 1] if rs <= ks + j < rs + NA_KH else outside for j in range(K_ROWS)]
            qrows.append(jnp.concatenate(tiles, axis=-1))
        blocks.append(jnp.concatenate(qrows, axis=1))
    return jnp.stack(blocks)


def _na_lat(p_lat, p_ctx, bias, bsz, seq, ctx_len, na_w, col0):
    heads = na_w // HEAD_DIM
    qb = Q_ROWS * GRID_W
    nblk = seq // qb
    assert nblk >= 4 and K_ROWS == 3 * Q_ROWS

    def kblock(j):
        return lambda b, k: (b * nblk + jnp.clip(k - 1, 0, nblk - 3) + j, col0 + 1)

    def vblock(j):
        return lambda b, k: (b * nblk + jnp.clip(k - 1, 0, nblk - 3) + j, col0 + 2)

    def btype(b, k):
        return (jnp.where(k == 0, 0, jnp.where(k == nblk - 1, 2, 1)), 0, 0, 0)

    return pl.pallas_call(
        functools.partial(_na_kernel, heads=heads),
        out_shape=jax.ShapeDtypeStruct((bsz * seq, na_w), BF16),
        grid=(bsz, nblk),
        in_specs=[
            pl.BlockSpec((qb, na_w), lambda b, k: (b * nblk + k, col0)),
            pl.BlockSpec((qb, na_w), kblock(0)),
            pl.BlockSpec((qb, na_w), kblock(1)),
            pl.BlockSpec((qb, na_w), kblock(2)),
            pl.BlockSpec((qb, na_w), vblock(0)),
            pl.BlockSpec((qb, na_w), vblock(1)),
            pl.BlockSpec((qb, na_w), vblock(2)),
            pl.BlockSpec((ctx_len, na_w), lambda b, k: (b, col0 + 1)),
            pl.BlockSpec((ctx_len, na_w), lambda b, k: (b, col0 + 2)),
            pl.BlockSpec((None, heads, qb, K_ROWS * GRID_W), btype),
        ],
        out_specs=pl.BlockSpec((qb, na_w), lambda b, k: (b * nblk + k, 0)),
        compiler_params=_params("parallel", "arbitrary"),
        name="na_lat",
    )(p_lat, p_lat, p_lat, p_lat, p_lat, p_lat, p_lat, p_ctx, p_ctx, bias)


def _attn_ctx_kernel(q_ref, k_ref, v_ref, o_ref, *, heads):
    for h in range(heads):
        cs = slice(h * HEAD_DIM, (h + 1) * HEAD_DIM)
        q = (q_ref[:, cs].astype(F32) * _NA_SCALE).astype(BF16)
        s = lax.dot_general(q, k_ref[:, cs], _NT, preferred_element_type=F32)
        o_ref[:, cs] = _softmax_pv([s], [v_ref[:, cs]]).astype(o_ref.dtype)


def _na_ctx(p_ctx, bsz, ctx_len, na_w, col0):
    heads = na_w // HEAD_DIM
    return pl.pallas_call(
        functools.partial(_attn_ctx_kernel, heads=heads),
        out_shape=jax.ShapeDtypeStruct((bsz * ctx_len, na_w), BF16),
        grid=(bsz,),
        in_specs=[pl.BlockSpec((ctx_len, na_w), lambda b, c=c: (b, col0 + c)) for c in range(3)],
        out_specs=pl.BlockSpec((ctx_len, na_w), lambda b: (b, 0)),
        compiler_params=_params("parallel"),
        name="na_ctx",
    )(p_ctx, p_ctx, p_ctx)


def _swap16(x):
    lane = lax.broadcasted_iota(jnp.int32, x.shape, 1)
    up = pltpu.roll(x, HEAD_DIM - 16, 1)
    down = pltpu.roll(x, 16, 1)
    return jnp.where((lane & 16) == 0, up, down)


def _diff_kernel(*refs, n_lat, rope, lam_init):
    if rope:
        (q_ref, k_ref, v_ref, kc_ref, vc_ref, cosq_ref, sinq_ref, cos_ref, sin_ref, lam_ref, g_ref,
         o_ref, k_scr, vt_scr, kmax_scr) = refs
    else:
        q_ref, kc_ref, vc_ref, lam_ref, g_ref, o_ref, k_scr, vt_scr, kmax_scr = refs
    n_ctx = kc_ref.shape[0]
    nk = n_lat + n_ctx
    hd = HEAD_DIM
    half = hd // 2

    @pl.when(pl.program_id(2) == 0)
    def _():
        step = 512
        row = lax.broadcasted_iota(jnp.int32, (hd, hd), 0)
        col = lax.broadcasted_iota(jnp.int32, (hd, hd), 1)
        sel = jnp.where(((col == 0) & (row < half)) | ((col == 1) & (row >= half)), 1.0, 0.0).astype(BF16)

        def put_keys(rs, kr, kmax):
            n = kr.shape[0]
            k_scr[rs, :hd] = kr
            lane = lax.broadcasted_iota(jnp.int32, (n, hd), 1)
            k_scr[rs, hd:] = jnp.where(lane == 0, 1.0, 0.0).astype(BF16)
            kf = kr.astype(F32)
            sq = jnp.dot((kf * kf).astype(BF16), sel, preferred_element_type=F32).max(axis=0, keepdims=True)
            return sq if kmax is None else jnp.maximum(kmax, sq)

        kmax = None
        for c in range(n_lat // step):
            rs = slice(c * step, (c + 1) * step)
            k = k_ref[rs, :].astype(F32)
            kmax = put_keys(rs, (k * cos_ref[rs, :] + _swap16(k) * sin_ref[rs, :]).astype(BF16), kmax)
            vt_scr[:hd, rs] = v_ref[rs, :].astype(F32).T.astype(BF16)
        kmax = put_keys(slice(n_lat, nk), kc_ref[...], kmax)
        vt_scr[:hd, n_lat:nk] = vc_ref[...].astype(F32).T.astype(BF16)
        vt_scr[hd:, :] = jnp.ones((SUM_ROWS, nk), BF16)
        lane1 = lax.broadcasted_iota(jnp.int32, kmax.shape, 1)
        for mp in range(2):
            kmax_scr[mp:mp + 1, :] = jnp.broadcast_to(
                jnp.max(jnp.where(lane1 == mp, kmax, 0.0), axis=-1, keepdims=True), kmax.shape)

    tq = q_ref.shape[0]

    def prep_q(qr_ref, cq_ref, sq_ref):
        qraw = qr_ref[...]
        q = qraw.astype(F32)
        row = lax.broadcasted_iota(jnp.int32, (hd, hd), 0)
        col = lax.broadcasted_iota(jnp.int32, (hd, hd), 1)
        sel1 = jnp.where((col == 0) & (row < half), 1.0, 0.0).astype(BF16)
        sel2 = jnp.where((col == 0) & (row >= half), 1.0, 0.0).astype(BF16)
        sq = (q * q).astype(BF16)
        n1 = jnp.dot(sq, sel1, preferred_element_type=F32)
        n2 = jnp.dot(sq, sel2, preferred_element_type=F32)
        if rope:
            perm = jnp.where(row == (col ^ 16), 1.0, 0.0).astype(BF16)
            q = q * cq_ref[...] + jnp.dot(qraw, perm, preferred_element_type=F32) * sq_ref[...]
        else:
            q = q * _Q_SCALE
        lane = lax.broadcasted_iota(jnp.int32, q.shape, 1)
        shift1 = jnp.sqrt(n1 * kmax_scr[0:1, :]) * (-_Q_SCALE * DIFF_BOUND_MARGIN)
        shift2 = jnp.sqrt(n2 * kmax_scr[1:2, :]) * (-_Q_SCALE * DIFF_BOUND_MARGIN)
        return jnp.concatenate([
            jnp.concatenate([jnp.where(lane < half, q, 0.0), shift1], axis=1),
            jnp.concatenate([jnp.where(lane >= half, q, 0.0), shift2], axis=1),
        ], axis=0).astype(BF16)

    q_aug = prep_q(q_ref, cosq_ref, sinq_ref) if rope else prep_q(q_ref, None, None)

    lp = lam_ref[...]
    d1 = jnp.sum(lp[0:1, :] * lp[1:2, :], axis=-1, keepdims=True)
    d2 = jnp.sum(lp[2:3, :] * lp[3:4, :], axis=-1, keepdims=True)
    lam = jnp.exp(d1) - jnp.exp(d2) + lam_init

    chunks = [(k0, min(DIFF_TK, nk - k0)) for k0 in range(0, nk, DIFF_TK)]

    def write_out(ot):
        ot = ot * lax.rsqrt(jnp.mean(ot * ot, axis=0, keepdims=True) + RMS_EPS)
        o_ref[...] = ((ot.T * g_ref[...]) * (1.0 - lam_init)).astype(o_ref.dtype)

    def finish(acc):
        l = acc[hd:hd + 1, :]
        c1 = 1.0 / l[:, :tq]
        c2 = lam / l[:, tq:]
        write_out(acc[:hd, :tq] * c1 - acc[:hd, tq:] * c2)

    def pipelined(scores, consume):
        pending = [scores(c) for c in range(min(DIFF_AHEAD, len(chunks)))]
        for c in range(len(chunks)):
            s = pending.pop(0)
            if c + DIFF_AHEAD < len(chunks):
                pending.append(scores(c + DIFF_AHEAD))
            consume(c, s)

    state = {}

    def fast_scores(c):
        k0, tk = chunks[c]
        return lax.dot_general(k_scr[k0:k0 + tk, :], q_aug, _NT, preferred_element_type=F32)

    def fast_consume(c, s):
        k0, tk = chunks[c]
        pv = jnp.dot(vt_scr[:, k0:k0 + tk], jnp.exp2(s).astype(BF16), preferred_element_type=F32)
        state["acc"] = pv if c == 0 else state["acc"] + pv

    pipelined(fast_scores, fast_consume)
    acc = state["acc"]
    finish(acc)

    @pl.when(jnp.logical_not(jnp.min(acc[hd:hd + 1, :]) > DIFF_L_MIN))
    def _():
        qab = q_aug[:, :hd]
        st = {}

        def safe_scores(c):
            k0, tk = chunks[c]
            return lax.dot_general(k_scr[k0:k0 + tk, :hd], qab, _NT, preferred_element_type=F32)

        def safe_consume(c, s):
            k0, tk = chunks[c]
            mc = s.max(axis=0, keepdims=True)
            m_new = mc if c == 0 else jnp.maximum(st["m"], mc)
            pv = jnp.dot(vt_scr[:, k0:k0 + tk], jnp.exp2(s - m_new).astype(BF16), preferred_element_type=F32)
            st["acc"] = pv if c == 0 else st["acc"] * jnp.exp2(st["m"] - m_new) + pv
            st["m"] = m_new

        pipelined(safe_scores, safe_consume)
        finish(st["acc"])


def _diff_lat(p_lat, p_ctx, tabs, lam_p, sub_g, bsz, seq, ctx_len, heads, lam_init):
    tq = DIFF_TQ
    nq = seq // tq
    hd = HEAD_DIM
    nk = seq + ctx_len
    cosq, sinq, cos, sin = tabs
    return pl.pallas_call(
        functools.partial(_diff_kernel, n_lat=seq, rope=True, lam_init=lam_init),
        out_shape=jax.ShapeDtypeStruct((bsz * seq, heads * hd), BF16),
        grid=(bsz, heads, nq),
        in_specs=[
            pl.BlockSpec((tq, hd), lambda b, h, i: (b * nq + i, h)),
            pl.BlockSpec((seq, hd), lambda b, h, i: (b, heads + h)),
            pl.BlockSpec((seq, hd), lambda b, h, i: (b, 2 * heads + h)),
            pl.BlockSpec((ctx_len, hd), lambda b, h, i: (b, heads + h)),
            pl.BlockSpec((ctx_len, hd), lambda b, h, i: (b, 2 * heads + h)),
            pl.BlockSpec((tq, hd), lambda b, h, i: (i, 0)),
            pl.BlockSpec((tq, hd), lambda b, h, i: (i, 0)),
            pl.BlockSpec((seq, hd), lambda b, h, i: (0, 0), pipeline_mode=pl.Buffered(1)),
            pl.BlockSpec((seq, hd), lambda b, h, i: (0, 0), pipeline_mode=pl.Buffered(1)),
            pl.BlockSpec((4, hd // 2), lambda b, h, i: (0, 0)),
            pl.BlockSpec((1, hd), lambda b, h, i: (0, 0)),
        ],
        out_specs=pl.BlockSpec((tq, hd), lambda b, h, i: (b * nq + i, h)),
        scratch_shapes=[pltpu.VMEM((nk, 2 * hd), BF16), pltpu.VMEM((hd + SUM_ROWS, nk), BF16), pltpu.VMEM((8, hd), F32)],
        compiler_params=_params("parallel", "parallel", "arbitrary"),
        name="diff_lat",
    )(p_lat, p_lat, p_lat, p_ctx, p_ctx, cosq, sinq, cos, sin, lam_p, sub_g.reshape(1, hd))


def _diff_ctx(p_ctx, lam_p, sub_g, bsz, ctx_len, heads, lam_init):
    hd = HEAD_DIM
    return pl.pallas_call(
        functools.partial(_diff_kernel, n_lat=0, rope=False, lam_init=lam_init),
        out_shape=jax.ShapeDtypeStruct((bsz * ctx_len, heads * hd), BF16),
        grid=(bsz, heads, 1),
        in_specs=[
            pl.BlockSpec((ctx_len, hd), lambda b, h, i: (b, h)),
            pl.BlockSpec((ctx_len, hd), lambda b, h, i: (b, heads + h)),
            pl.BlockSpec((ctx_len, hd), lambda b, h, i: (b, 2 * heads + h)),
            pl.BlockSpec((4, hd // 2), lambda b, h, i: (0, 0)),
            pl.BlockSpec((1, hd), lambda b, h, i: (0, 0)),
        ],
        out_specs=pl.BlockSpec((ctx_len, hd), lambda b, h, i: (b, h)),
        scratch_shapes=[pltpu.VMEM((ctx_len, 2 * hd), BF16), pltpu.VMEM((hd + SUM_ROWS, ctx_len), BF16),
                        pltpu.VMEM((8, hd), F32)],
        compiler_params=_params("parallel", "parallel", "arbitrary"),
        name="diff_ctx",
    )(p_ctx, p_ctx, p_ctx, lam_p, sub_g.reshape(1, hd))


def _rope_tables(seq):
    axis = HEAD_DIM // 4
    t = jnp.arange(seq)
    row = (t // GRID_W).astype(F32)
    col = (t % GRID_W).astype(F32)
    inv = ROPE_THETA ** (-jnp.arange(0, axis, 2, dtype=F32) / axis)
    ang_r = row[:, None] * inv
    ang_c = col[:, None] * inv
    cr, sr, cc, sc = jnp.cos(ang_r), jnp.sin(ang_r), jnp.cos(ang_c), jnp.sin(ang_c)
    cos = jnp.concatenate([cr, cr, cc, cc] * 2, axis=-1)
    sin = jnp.concatenate([-sr, sr, -sc, sc] * 2, axis=-1)
    return cos * _Q_SCALE, sin * _Q_SCALE, cos, sin


def _fnet_kernel(f_ref, cn_ref, sn_ref, cc_ref, sc_ref, sgn_ref, o_ref, a_scr, b_scr, s_scr, *, norm, nt):
    i = pl.program_id(1)
    n = f_ref.shape[0]
    half = n // 2
    tr = cn_ref.shape[0]

    @pl.when(i == 0)
    def _():
        step = 512 if n % 512 == 0 else n
        for c in range(n // step):
            rs = slice(c * step, (c + 1) * step)
            f = f_ref[rs, :]
            a_scr[rs, :] = jnp.dot(f, cc_ref[...], preferred_element_type=F32).astype(BF16)
            b_scr[rs, :] = jnp.dot(f, sc_ref[...], preferred_element_type=F32).astype(BF16)
        s_scr[half:, :] = jnp.zeros((tr, s_scr.shape[1]), BF16)
        s_scr[half:half + FNET_NY_ROWS, :] = (
            jnp.dot(sgn_ref[...], a_scr[...], preferred_element_type=F32) * norm).astype(BF16)

    p = jnp.dot(cn_ref[...], a_scr[...], preferred_element_type=F32)
    q = jnp.dot(sn_ref[...], b_scr[...], preferred_element_type=F32)
    rows = pl.ds(pl.multiple_of(i * tr, tr), tr)
    o_ref[rows, :] = ((p - q) * norm).astype(o_ref.dtype)
    s_scr[rows, :] = ((p + q) * norm).astype(BF16)

    @pl.when(i == nt - 1)
    def _():
        r = lax.broadcasted_iota(jnp.int32, (tr, 2 * tr), 0)
        c = lax.broadcasted_iota(jnp.int32, (tr, 2 * tr), 1)
        pick = jnp.where(c == tr - r, 1.0, 0.0).astype(BF16)
        for j in range(nt):
            w = s_scr[(nt - 1 - j) * tr:(nt + 1 - j) * tr, :]
            o_ref[half + j * tr:half + (j + 1) * tr, :] = jnp.dot(pick, w, preferred_element_type=F32).astype(o_ref.dtype)


def _fnet(p, cn, sn, ccb, scb, bsz, n, fw, col):
    half = n // 2
    tr = _tile(half, 512)
    nt = half // tr
    norm = 1.0 / math.sqrt(n * HEAD_DIM)
    sgn = jnp.asarray(np.tile(np.where(np.arange(n) % 2 == 0, 1.0, -1.0), (FNET_NY_ROWS, 1)), BF16)
    return pl.pallas_call(
        functools.partial(_fnet_kernel, norm=norm, nt=nt),
        out_shape=jax.ShapeDtypeStruct((bsz * n, fw), BF16),
        grid=(bsz, nt),
        in_specs=[
            pl.BlockSpec((n, fw), lambda b, i: (b, col)),
            pl.BlockSpec((tr, n), lambda b, i: (i, 0)),
            pl.BlockSpec((tr, n), lambda b, i: (i, 0)),
            pl.BlockSpec((fw, fw), lambda b, i: (0, 0)),
            pl.BlockSpec((fw, fw), lambda b, i: (0, 0)),
            pl.BlockSpec((FNET_NY_ROWS, n), lambda b, i: (0, 0)),
        ],
        out_specs=pl.BlockSpec((n, fw), lambda b, i: (b, 0)),
        scratch_shapes=[pltpu.VMEM((n, fw), BF16), pltpu.VMEM((n, fw), BF16), pltpu.VMEM((half + tr, fw), BF16)],
        compiler_params=_params("parallel", "arbitrary"),
        name="fnet",
    )(p, cn, sn, ccb, scb, sgn)


def _dft_tables(n):
    if n <= 256:
        idx = np.outer(np.arange(n // 2), np.arange(n)) % n
        ang = 2.0 * np.pi * idx / n
        return jnp.asarray(np.cos(ang), F32).astype(BF16), jnp.asarray(np.sin(ang), F32).astype(BF16)
    r = int(round(math.sqrt(n)))
    assert r * r == n and r % 2 == 0
    k = np.arange(n)
    ang_a = 2.0 * np.pi * (np.outer(np.arange(r // 2), k) % r) / r
    ang_b = 2.0 * np.pi * (np.outer(np.arange(r), k) % n) / n
    ca, sa = jnp.asarray(np.cos(ang_a), F32)[:, None, :], jnp.asarray(np.sin(ang_a), F32)[:, None, :]
    cb, sb = jnp.asarray(np.cos(ang_b), F32)[None, :, :], jnp.asarray(np.sin(ang_b), F32)[None, :, :]
    cos = (ca * cb - sa * sb).reshape(n // 2, n).astype(BF16)
    sin = (sa * cb + ca * sb).reshape(n // 2, n).astype(BF16)
    return cos, sin


def _channel_dft_tables(fw):
    idx = np.outer(np.arange(HEAD_DIM), np.arange(HEAD_DIM)) % HEAD_DIM
    ang = 2.0 * np.pi * idx / HEAD_DIM
    eye = np.eye(fw // HEAD_DIM)
    return (jnp.asarray(np.kron(eye, np.cos(ang)), F32).astype(BF16),
            jnp.asarray(np.kron(eye, np.sin(ang)), F32).astype(BF16))


def kernel(x, c, ctx, c_ctx, ada_w, ada_b, norm1_g, norm2_g, w_in, w_out, sgu_norm_g, sgu_w, sgu_b, na_rpb,
           diff_lq1, diff_lk1, diff_lq2, diff_lk2, diff_subln_g, mlp_w1, mlp_w2, final_g):
    bsz, seq, d = x.shape
    ctx_len = ctx.shape[1]
    depth = ada_w.shape[0]
    rows = seq // GRID_W
    mix_heads = d // HEAD_DIM
    sgu_w_dim = (mix_heads // 2) * HEAD_DIM
    na_w = d - sgu_w_dim
    diff_heads = (mix_heads * 3) // 4
    fnet_w = d - diff_heads * HEAD_DIM
    assert sgu_w_dim == na_w and bsz + 1 <= MOD_ROWS

    cond = jnp.zeros((MOD_ROWS, d), F32).at[:bsz].set(c).at[bsz].set(c_ctx)
    mods = _ada_table(cond, ada_w, ada_b).reshape(depth, MOD_ROWS, 1, ADA_CHUNKS * d)

    w_out_b = w_out.astype(BF16)
    sgu_w_b = sgu_w.astype(BF16)
    n1g = norm1_g.reshape(depth, 1, d)
    n2g = norm2_g.reshape(depth, 1, d)
    sgu_g = sgu_norm_g.reshape(-1, 1, sgu_w_dim)
    sgu_b_full = jnp.broadcast_to(sgu_b[..., None], sgu_b.shape + (HEAD_DIM,)).astype(F32)

    tm_lat = _tile(seq, 512)
    tm_ctx = _tile(bsz * ctx_len, 512)
    lat_row = lambda i: (i * tm_lat) // seq
    ctx_row = lambda i: bsz

    x_lat = x.reshape(bsz * seq, d)
    x_ctx = ctx.reshape(bsz * ctx_len, d)

    if depth > 1:
        rope_tabs = _rope_tables(seq)
        cn, sn = _dft_tables(seq)
        cn_c, sn_c = _dft_tables(ctx_len)
        ccb, scb = _channel_dft_tables(fnet_w)

    for l in range(depth):
        need_ctx = l < depth - 1
        p_ctx, w_in_l = _in_proj_ctx(x_ctx, mods, n1g, w_in, l, bsz)
        p_lat = _in_proj(x_lat, mods, n1g, w_in_l, l, lat_row, tm_lat)
        i = l // 2
        ma_ctx = mb_ctx = None
        if l % 2 == 0:
            bias = _na_bias(na_rpb[i], rows)
            ma_lat = _sgu(p_lat, sgu_g, sgu_w_b, sgu_b_full, i, sgu_w_dim)
            mb_lat = _na_lat(p_lat, p_ctx, bias, bsz, seq, ctx_len, na_w, 2)
            if need_ctx:
                ma_ctx = _sgu(p_ctx, sgu_g, sgu_w_b, sgu_b_full, i, sgu_w_dim)
                mb_ctx = _na_ctx(p_ctx, bsz, ctx_len, na_w, 2)
        else:
            lam_init = 0.8 - 0.6 * math.exp(-0.3 * l)
            lam_p = jnp.stack([diff_lq1[i], diff_lk1[i], diff_lq2[i], diff_lk2[i]]).astype(F32)
            fcol = (3 * diff_heads * HEAD_DIM) // fnet_w
            ma_lat = _diff_lat(p_lat, p_ctx, rope_tabs, lam_p, diff_subln_g[i], bsz, seq, ctx_len, diff_heads, lam_init)
            mb_lat = _fnet(p_lat, cn, sn, ccb, scb, bsz, seq, fnet_w, fcol)
            if need_ctx:
                ma_ctx = _diff_ctx(p_ctx, lam_p, diff_subln_g[i], bsz, ctx_len, diff_heads, lam_init)
                mb_ctx = _fnet(p_ctx, cn_c, sn_c, ccb, scb, bsz, ctx_len, fnet_w, fcol)
        if need_ctx:
            x_ctx = _out_proj(x_ctx, ma_ctx, mb_ctx, w_out_b, mods, l, ctx_row, tm_ctx)
            x_ctx, w1_l, w2_l = _mlp_ctx(x_ctx, mods, n2g, mlp_w1, mlp_w2, l, bsz)
        else:
            w1_l, w2_l = mlp_w1[l].astype(BF16), mlp_w2[l].astype(BF16)
        x_lat = _out_proj(x_lat, ma_lat, mb_lat, w_out_b, mods, l, lat_row, tm_lat)
        x_lat = _mlp(x_lat, mods, n2g, w1_l, w2_l, final_g, l, lat_row, tm_lat, final_norm=not need_ctx)
    return x_lat.reshape(bsz, seq, d)
```
